```python
import math
import jax, jax.numpy as jnp
from jax import lax
import numpy as np

D_MODEL = 2048
BATCH = 2
SEQ = 4096
DEPTH = 4
DEC_BATCH = 8
DEC_SEQ = 4
PAST_LEN = 16384
PAGE_SIZE = 128

N_MIXERS = 2
N_ATTN_LAYERS = (DEPTH + 1) // 2
N_RET_LAYERS = DEPTH // 2
N_HEADS = 16
HEAD_DIM = D_MODEL // N_HEADS
IDX_HEADS = 16
IDX_DIM = 128
TOPK_MAX = 256
Q_BLOCK = 64
REL_BUCKETS = 32
REL_MAX_DIST = 128
RET_HEADS = 8
RET_DK = D_MODEL // RET_HEADS
RET_DV = 2 * D_MODEL // RET_HEADS
RET_VW = RET_HEADS * RET_DV
RET_CHUNK = 128
ROPE_BASE = 10000.0
D_FF = 5632
N_EXPERTS = 8
TOP_K_EXPERTS = 2
D_FF_EXPERT = 2816
N_DENSE_LAYERS = (DEPTH + 1) // 2
N_MOE_LAYERS = DEPTH // 2
DN_ALPHA = (2 * DEPTH) ** 0.25
DN_BETA = (8 * DEPTH) ** -0.25
LN_EPS = 1e-5
NEG = -1e30
ATTN_COLS = 3 * D_MODEL + IDX_HEADS * IDX_DIM + IDX_DIM + IDX_HEADS
RET_COLS = 2 * D_MODEL + 2 * RET_VW

kernel_name = 'dsa_retention_hybrid_step'

F32 = jnp.float32


def layer_norm(x, g, b):
    xf = x.astype(F32)
    mu = xf.mean(-1, keepdims=True)
    var = jnp.square(xf - mu).mean(-1, keepdims=True)
    return ((xf - mu) * lax.rsqrt(var + LN_EPS) * g + b).astype(x.dtype)


def rel_bucket(dist):
    max_exact = REL_BUCKETS // 2
    large = max_exact + (jnp.log(jnp.maximum(dist, 1).astype(F32) / max_exact)
                         / math.log(REL_MAX_DIST / max_exact) * (REL_BUCKETS - max_exact)).astype(jnp.int32)
    large = jnp.minimum(large, REL_BUCKETS - 1)
    return jnp.where(dist < max_exact, dist, large)


def attn_project(x, w_in):
    b, t, _ = x.shape
    h = x @ w_in
    o1, o2, o3 = D_MODEL, 2 * D_MODEL, 3 * D_MODEL
    o4 = o3 + IDX_HEADS * IDX_DIM
    o5 = o4 + IDX_DIM
    q = h[..., :o1].reshape(b, t, N_HEADS, HEAD_DIM)
    k = h[..., o1:o2].reshape(b, t, N_HEADS, HEAD_DIM)
    v = h[..., o2:o3].reshape(b, t, N_HEADS, HEAD_DIM)
    iq = h[..., o3:o4].reshape(b, t, IDX_HEADS, IDX_DIM)
    ik = h[..., o4:o5]
    iw = h[..., o5:o5 + IDX_HEADS] * IDX_HEADS ** -0.5
    return q, k, v, iq, ik, iw


def indexer_scores(iq, iw, ik, q_pos, k_pos):
    dots = jnp.einsum('bthd,bsd->bths', iq.astype(F32), ik.astype(F32)) * IDX_DIM ** -0.5
    sc = jnp.einsum('bths,bth->bts', jax.nn.relu(dots), iw.astype(F32))
    return jnp.where(k_pos[None, None, :] <= q_pos[None, :, None], sc, NEG)


def sparse_attend(q, k_sel, v_sel, sel_pos, q_pos, rel_bias):
    dist = q_pos[None, :, None] - sel_pos
    valid = dist >= 0
    bias = jnp.swapaxes(rel_bias.astype(F32)[rel_bucket(jnp.maximum(dist, 0))], -1, -2)
    logits = jnp.einsum('bthd,btkhd->bthk', q.astype(F32), k_sel.astype(F32)) * HEAD_DIM ** -0.5 + bias
    logits = jnp.where(valid[:, :, None, :], logits, NEG)
    p = jax.nn.softmax(logits, axis=-1)
    out = jnp.einsum('bthk,btkhd->bthd', p, v_sel.astype(F32))
    return out.astype(q.dtype)


def gather_rows(a, idx):
    return jax.vmap(lambda ab, ib: ab[ib])(a, idx)


def attn_prompt(x, w_in, w_out, rel_bias):
    b, t, _ = x.shape
    q, k, v, iq, ik, iw = attn_project(x, w_in)
    n_sel = min(TOPK_MAX, t // 4)
    pos = jnp.arange(t, dtype=jnp.int32)
    qb_len = min(Q_BLOCK, t)

    def block(i):
        t0 = i * qb_len
        qb = lax.dynamic_slice_in_dim(q, t0, qb_len, axis=1)
        iqb = lax.dynamic_slice_in_dim(iq, t0, qb_len, axis=1)
        iwb = lax.dynamic_slice_in_dim(iw, t0, qb_len, axis=1)
        qpos = t0 + jnp.arange(qb_len, dtype=jnp.int32)
        sc = indexer_scores(iqb, iwb, ik, qpos, pos)
        _, sel = lax.top_k(sc, n_sel)
        sel = sel.astype(jnp.int32)
        return sparse_attend(qb, gather_rows(k, sel), gather_rows(v, sel), sel, qpos, rel_bias)

    o = lax.map(block, jnp.arange(t // qb_len))
    o = jnp.moveaxis(o, 0, 1).reshape(b, t, D_MODEL)
    return o @ w_out, k, v, ik


def attn_sample(x, layer, cache_k, cache_v, cache_idx_k, page_table, w_in, w_out, rel_bias):
    b, t, _ = x.shape
    q, k, v, iq, ik, iw = attn_project(x, w_in)
    past = page_table.shape[1] * PAGE_SIZE
    n_keys = past + t
    n_sel = min(TOPK_MAX, n_keys // 4)
    ik_past = cache_idx_k[layer, page_table].reshape(b, past, IDX_DIM)
    ik_all = jnp.concatenate([ik_past.astype(ik.dtype), ik], axis=1)
    qpos = past + jnp.arange(t, dtype=jnp.int32)
    kpos = jnp.arange(n_keys, dtype=jnp.int32)
    sc = indexer_scores(iq, iw, ik_all, qpos, kpos)
    _, sel = lax.top_k(sc, n_sel)
    sel = sel.astype(jnp.int32)
    in_past = (sel < past)[..., None, None]
    ps = jnp.minimum(sel, past - 1)
    phys = jnp.take_along_axis(page_table, (ps // PAGE_SIZE).reshape(b, -1), axis=1).reshape(sel.shape)
    slot = ps % PAGE_SIZE
    ns = jnp.clip(sel - past, 0, t - 1)
    k_sel = jnp.where(in_past, cache_k[layer, phys, slot].astype(k.dtype), gather_rows(k, ns))
    v_sel = jnp.where(in_past, cache_v[layer, phys, slot].astype(v.dtype), gather_rows(v, ns))
    o = sparse_attend(q, k_sel, v_sel, sel, qpos, rel_bias).reshape(b, t, D_MODEL)
    return o @ w_out, k, v, ik


def rotary(x, pos):
    half = x.shape[-1] // 2
    inv = jnp.exp(-math.log(ROPE_BASE) * jnp.arange(half, dtype=F32) / half)
    ang = pos.astype(F32)[:, None] * inv[None, :]
    cos = jnp.cos(ang)[None, :, None, :]
    sin = jnp.sin(ang)[None, :, None, :]
    x = x.astype(F32)
    x1, x2 = x[..., :half], x[..., half:]
    return jnp.concatenate([x1 * cos - x2 * sin, x1 * sin + x2 * cos], axis=-1)


def ret_log_gamma():
    return jnp.log1p(-jnp.exp2(-5.0 - jnp.arange(RET_HEADS, dtype=F32)))


def retention_chunk(s_prev, qkv):
    q, k, v = qkv
    c = q.shape[1]
    lg = ret_log_gamma()
    idx = jnp.arange(c, dtype=F32)
    rel = idx[:, None] - idx[None, :]
    decay = jnp.where(rel >= 0, jnp.exp(lg[:, None, None] * jnp.maximum(rel, 0.0)), 0.0)
    scores = jnp.einsum('bchd,bmhd->bhcm', q, k) * decay
    inner = jnp.einsum('bhcm,bmhe->bche', scores, v)
    cross = jnp.einsum('bchd,bhde->bche', q, s_prev) * jnp.exp(lg[None, :] * (idx[:, None] + 1.0))[None, :, :, None]
    w_state = jnp.exp(lg[None, :] * (c - 1.0 - idx[:, None]))
    s_new = s_prev * jnp.exp(lg * c)[None, :, None, None] + jnp.einsum('bmhd,bmhe,mh->bhde', k, v, w_state)
    return s_new, inner + cross


def retention_mixer(x, s0, pos, w_in, gn_g, w_out, chunk):
    b, t, _ = x.shape
    h = x @ w_in
    q = rotary(h[..., :D_MODEL].reshape(b, t, RET_HEADS, RET_DK), pos)
    k = rotary(h[..., D_MODEL:2 * D_MODEL].reshape(b, t, RET_HEADS, RET_DK), pos) * RET_DK ** -0.5
    v = h[..., 2 * D_MODEL:2 * D_MODEL + RET_VW].reshape(b, t, RET_HEADS, RET_DV).astype(F32)
    g = h[..., 2 * D_MODEL + RET_VW:]
    nc = t // chunk
    split = lambda a: jnp.moveaxis(a.reshape(b, nc, chunk, *a.shape[2:]), 1, 0)
    s_fin, o = lax.scan(retention_chunk, s0.astype(F32), (split(q), split(k), split(v)))
    o = jnp.moveaxis(o, 0, 1).reshape(b, t, RET_HEADS, RET_DV)
    mu = o.mean(-1, keepdims=True)
    var = jnp.square(o - mu).mean(-1, keepdims=True)
    o = ((o - mu) * lax.rsqrt(var + LN_EPS)).reshape(b, t, RET_VW) * gn_g
    y = (jax.nn.silu(g.astype(F32)) * o).astype(x.dtype) @ w_out
    return y, s_fin


def swiglu(x, w_gu, w_down):
    gate, up = jnp.split(x @ w_gu, 2, axis=-1)
    return (jax.nn.silu(gate) * up) @ w_down


def moe(x, w_router, w_gu, w_down):
    logits = (x @ w_router).astype(F32)
    top_v, top_i = lax.top_k(logits, TOP_K_EXPERTS)
    gates = jax.nn.softmax(top_v, axis=-1)
    dense_g = jnp.sum(jax.nn.one_hot(top_i, N_EXPERTS, dtype=F32) * gates[..., None], axis=-2)
    out = jnp.zeros(x.shape, F32)
    for e in range(N_EXPERTS):
        out = out + dense_g[..., e:e + 1] * swiglu(x, w_gu[e], w_down[e]).astype(F32)
    return out.astype(x.dtype)


def setup_inputs(seed: int = 0) -> dict:
    key = jax.random.key(seed)
    ks = jax.random.split(key, 20)
    n_pages = PAST_LEN // PAGE_SIZE
    n_phys = (DEC_BATCH * n_pages * 5) // 4
    nrm = lambda k, shape, scale: jax.random.normal(k, shape, F32) * scale
    page_table = jax.random.permutation(ks[6], n_phys)[:DEC_BATCH * n_pages]
    page_table = page_table.reshape(DEC_BATCH, n_pages).astype(jnp.int32)
    return {
        'x_prompt': nrm(ks[0], (BATCH, SEQ, D_MODEL), 1.0),
        'x_sample': nrm(ks[1], (DEC_BATCH, DEC_SEQ, D_MODEL), 1.0),
        'cache_k': nrm(ks[2], (N_ATTN_LAYERS, n_phys, PAGE_SIZE, N_HEADS, HEAD_DIM), 1.0),
        'cache_v': nrm(ks[3], (N_ATTN_LAYERS, n_phys, PAGE_SIZE, N_HEADS, HEAD_DIM), 1.0),
        'cache_idx_k': nrm(ks[4], (N_ATTN_LAYERS, n_phys, PAGE_SIZE, IDX_DIM), 1.0),
        'state_ret': nrm(ks[5], (N_RET_LAYERS, DEC_BATCH, RET_HEADS, RET_DK, RET_DV), 0.5),
        'page_table': page_table,
        'rel_bias': nrm(ks[7], (REL_BUCKETS, N_HEADS), 0.5),
        'w_in_attn': nrm(ks[8], (N_ATTN_LAYERS, D_MODEL, ATTN_COLS), D_MODEL ** -0.5),
        'w_out_attn': nrm(ks[9], (N_ATTN_LAYERS, D_MODEL, D_MODEL), DN_BETA * D_MODEL ** -0.5),
        'w_in_ret': nrm(ks[10], (N_RET_LAYERS, D_MODEL, RET_COLS), D_MODEL ** -0.5),
        'ret_gn_g': 1.0 + nrm(ks[11], (N_RET_LAYERS, RET_VW), 0.02),
        'w_out_ret': nrm(ks[12], (N_RET_LAYERS, RET_VW, D_MODEL), DN_BETA * RET_VW ** -0.5),
        'w_ffn_gu': nrm(ks[13], (N_DENSE_LAYERS, D_MODEL, 2 * D_FF), D_MODEL ** -0.5),
        'w_ffn_down': nrm(ks[14], (N_DENSE_LAYERS, D_FF, D_MODEL), DN_BETA * D_FF ** -0.5),
        'w_router': nrm(ks[15], (N_MOE_LAYERS, D_MODEL, N_EXPERTS), D_MODEL ** -0.5),
        'w_exp_gu': nrm(ks[16], (N_MOE_LAYERS, N_EXPERTS, D_MODEL, 2 * D_FF_EXPERT), D_MODEL ** -0.5),
        'w_exp_down': nrm(ks[17], (N_MOE_LAYERS, N_EXPERTS, D_FF_EXPERT, D_MODEL), DN_BETA * D_FF_EXPERT ** -0.5),
        'ln_g': 1.0 + nrm(ks[18], (DEPTH, 2, D_MODEL), 0.02),
        'ln_b': nrm(ks[19], (DEPTH, 2, D_MODEL), 0.02),
    }


def reference(x_prompt, x_sample, cache_k, cache_v, cache_idx_k, state_ret, page_table, rel_bias,
              w_in_attn, w_out_attn, w_in_ret, ret_gn_g, w_out_ret, w_ffn_gu, w_ffn_down,
              w_router, w_exp_gu, w_exp_down, ln_g, ln_b):
    xp, xs = x_prompt, x_sample
    bp, tp = xp.shape[0], xp.shape[1]
    ts = xs.shape[1]
    past = page_table.shape[1] * PAGE_SIZE
    pos_p = jnp.arange(tp, dtype=jnp.int32)
    pos_s = past + jnp.arange(ts, dtype=jnp.int32)
    kp_l, vp_l, ikp_l, sp_l = [], [], [], []
    ks_l, vs_l, iks_l, ss_l = [], [], [], []
    for i in range(DEPTH):
        j = i // N_MIXERS
        if i % N_MIXERS == 0:
            yp, kp, vp, ikp = attn_prompt(xp, w_in_attn[j], w_out_attn[j], rel_bias)
            ys, k_s, v_s, iks = attn_sample(xs, j, cache_k, cache_v, cache_idx_k, page_table,
                                            w_in_attn[j], w_out_attn[j], rel_bias)
            kp_l.append(kp); vp_l.append(vp); ikp_l.append(ikp)
            ks_l.append(k_s); vs_l.append(v_s); iks_l.append(iks)
        else:
            s0 = jnp.zeros((bp, RET_HEADS, RET_DK, RET_DV), F32)
            yp, sp = retention_mixer(xp, s0, pos_p, w_in_ret[j], ret_gn_g[j], w_out_ret[j], min(RET_CHUNK, tp))
            ys, ss = retention_mixer(xs, state_ret[j], pos_s, w_in_ret[j], ret_gn_g[j], w_out_ret[j], ts)
            sp_l.append(sp); ss_l.append(ss)
        xp = layer_norm(DN_ALPHA * xp + yp, ln_g[i, 0], ln_b[i, 0])
        xs = layer_norm(DN_ALPHA * xs + ys, ln_g[i, 0], ln_b[i, 0])
        f = i // 2
        if i % 2 == 0:
            fp = swiglu(xp, w_ffn_gu[f], w_ffn_down[f])
            fs = swiglu(xs, w_ffn_gu[f], w_ffn_down[f])
        else:
            fp = moe(xp, w_router[f], w_exp_gu[f], w_exp_down[f])
            fs = moe(xs, w_router[f], w_exp_gu[f], w_exp_down[f])
        xp = layer_norm(DN_ALPHA * xp + fp, ln_g[i, 1], ln_b[i, 1])
        xs = layer_norm(DN_ALPHA * xs + fs, ln_g[i, 1], ln_b[i, 1])
    return (xp, xs, jnp.stack(kp_l), jnp.stack(vp_l), jnp.stack(ikp_l), jnp.stack(sp_l),
            jnp.stack(ks_l), jnp.stack(vs_l), jnp.stack(iks_l), jnp.stack(ss_l))
```

```python
import functools
import math

import jax
import jax.numpy as jnp
from jax import lax
from jax.experimental import pallas as pl
from jax.experimental.pallas import tpu as pltpu

F32 = jnp.float32
BF16 = jnp.bfloat16

D_MODEL = 2048
PAGE_SIZE = 128
N_HEADS = 16
HEAD_DIM = 128
IDX_HEADS = 16
IDX_DIM = 128
TOPK_MAX = 256
REL_BUCKETS = 32
REL_MAX_DIST = 128
RET_HEADS = 8
RET_DK = 256
RET_DV = 512
RET_VW = RET_HEADS * RET_DV
RET_CHUNK = 128
ROPE_BASE = 10000.0
D_FF = 5632
N_EXPERTS = 8
D_FF_EXPERT = 2816
DEPTH = 4
DN_ALPHA = (2 * DEPTH) ** 0.25
LN_EPS = 1e-5
NEG = -1e30

LANES = 128
VMEM_LIMIT = 56 * 1024 * 1024
INT_MIN = -2 ** 31


def _cp(n_axes, vmem=VMEM_LIMIT):
    return pltpu.CompilerParams(dimension_semantics=("arbitrary",) * n_axes,
                                vmem_limit_bytes=vmem)


def _silu(x):
    return x * (1.0 / (1.0 + jnp.exp(-x)))


def _dot(a, b):
    return jnp.dot(a, b, preferred_element_type=F32)


def _dot_nt(a, b):
    return lax.dot_general(a, b, (((1,), (1,)), ((), ())), preferred_element_type=F32)


def _dot_tn(a, b):
    return lax.dot_general(a, b, (((0,), (0,)), ((), ())), preferred_element_type=F32)


def _proj_kernel(a_ref, w_ref, *rest, n_out, scale):
    out_refs = rest[:n_out]
    wb_ref = rest[n_out]

    @pl.when(pl.program_id(1) == 0)
    def _():
        wb_ref[...] = w_ref[...].astype(BF16)

    acc = _dot(a_ref[...], wb_ref[...])
    if scale != 1.0:
        acc = acc * scale
    for o in out_refs:
        o[...] = acc.astype(o.dtype)


def proj(a, w, layer, col0, ncols, tn, tm, out_dtypes, scale=1.0, name="proj"):
    m, k = a.shape
    tm = min(tm, m)
    assert col0 % tn == 0 and ncols % tn == 0 and m % tm == 0
    off = col0 // tn
    if w.ndim == 3:
        w_spec = pl.BlockSpec((None, k, tn), lambda j, i: (layer, 0, j + off))
    else:
        w_spec = pl.BlockSpec((k, tn), lambda j, i: (0, j + off))
    outs = pl.pallas_call(
        functools.partial(_proj_kernel, n_out=len(out_dtypes), scale=scale),
        out_shape=[jax.ShapeDtypeStruct((m, ncols), dt) for dt in out_dtypes],
        grid=(ncols // tn, m // tm),
        in_specs=[pl.BlockSpec((tm, k), lambda j, i: (i, 0)), w_spec],
        out_specs=[pl.BlockSpec((tm, tn), lambda j, i: (i, j)) for _ in out_dtypes],
        scratch_shapes=[pltpu.VMEM((k, tn), BF16)],
        compiler_params=_cp(2),
        name=name,
    )(a, w)
    return outs


def _proj_rot_kernel(a_ref, w_ref, cos_ref, sin_ref, o_ref, wb_ref, *, heads_per_tile, q_tiles):
    j = pl.program_id(0)

    @pl.when(pl.program_id(1) == 0)
    def _():
        wb_ref[...] = w_ref[...].astype(BF16)

    acc = _dot(a_ref[...], wb_ref[...])
    scale = jnp.where(j >= q_tiles, RET_DK ** -0.5, 1.0).astype(F32)
    c = cos_ref[...]
    s = sin_ref[...]
    half = RET_DK // 2
    for hh in range(heads_per_tile):
        x1 = acc[:, hh * RET_DK: hh * RET_DK + half]
        x2 = acc[:, hh * RET_DK + half: (hh + 1) * RET_DK]
        o_ref[:, hh * RET_DK: hh * RET_DK + half] = ((x1 * c - x2 * s) * scale).astype(BF16)
        o_ref[:, hh * RET_DK + half: (hh + 1) * RET_DK] = ((x1 * s + x2 * c) * scale).astype(BF16)


def proj_rot(a, w, layer, cos, sin, tn, tm):
    m, k = a.shape
    tm = min(tm, m)
    ncols = 2 * D_MODEL
    return pl.pallas_call(
        functools.partial(_proj_rot_kernel, heads_per_tile=tn // RET_DK, q_tiles=D_MODEL // tn),
        out_shape=jax.ShapeDtypeStruct((m, ncols), BF16),
        grid=(ncols // tn, m // tm),
        in_specs=[pl.BlockSpec((tm, k), lambda j, i: (i, 0)),
                  pl.BlockSpec((None, k, tn), lambda j, i: (layer, 0, j)),
                  pl.BlockSpec((tm, RET_DK // 2), lambda j, i: (i, 0)),
                  pl.BlockSpec((tm, RET_DK // 2), lambda j, i: (i, 0))],
        out_specs=pl.BlockSpec((tm, tn), lambda j, i: (i, j)),
        scratch_shapes=[pltpu.VMEM((k, tn), BF16)],
        compiler_params=_cp(2),
        name="proj_rot",
    )(a, w, cos, sin)


def _gu_kernel(a_ref, wg_ref, wu_ref, *rest, gated, inner_axis):
    if gated:
        gate_ref, o_ref, wgb_ref, wub_ref = rest
    else:
        o_ref, wgb_ref, wub_ref = rest

    @pl.when(pl.program_id(inner_axis) == 0)
    def _():
        wgb_ref[...] = wg_ref[...].astype(BF16)
        wub_ref[...] = wu_ref[...].astype(BF16)

    a = a_ref[...]
    h = _silu(_dot(a, wgb_ref[...])) * _dot(a, wub_ref[...])
    if gated:
        h = h * gate_ref[...]
    o_ref[...] = h.astype(BF16)


def ffn_gu(a, w_gu, layer, tn, tm):
    m, k = a.shape
    tm = min(tm, m)
    nj = D_FF // tn
    return pl.pallas_call(
        functools.partial(_gu_kernel, gated=False, inner_axis=1),
        out_shape=jax.ShapeDtypeStruct((m, D_FF), BF16),
        grid=(nj, m // tm),
        in_specs=[pl.BlockSpec((tm, k), lambda j, i: (i, 0)),
                  pl.BlockSpec((None, k, tn), lambda j, i: (layer, 0, j)),
                  pl.BlockSpec((None, k, tn), lambda j, i: (layer, 0, j + nj))],
        out_specs=pl.BlockSpec((tm, tn), lambda j, i: (i, j)),
        scratch_shapes=[pltpu.VMEM((k, tn), BF16), pltpu.VMEM((k, tn), BF16)],
        compiler_params=_cp(2),
        name="ffn_gu",
    )(a, w_gu, w_gu)


def moe_gu(a, w_gu, layer, gates_t, tn, tm):
    m, k = a.shape
    tm = min(tm, m)
    nj = D_FF_EXPERT // tn
    return pl.pallas_call(
        functools.partial(_gu_kernel, gated=True, inner_axis=2),
        out_shape=jax.ShapeDtypeStruct((m, N_EXPERTS * D_FF_EXPERT), BF16),
        grid=(N_EXPERTS, nj, m // tm),
        in_specs=[pl.BlockSpec((tm, k), lambda e, j, i: (i, 0)),
                  pl.BlockSpec((None, None, k, tn), lambda e, j, i: (layer, e, 0, j)),
                  pl.BlockSpec((None, None, k, tn), lambda e, j, i: (layer, e, 0, j + nj)),
                  pl.BlockSpec((None, tm, 1), lambda e, j, i: (e, i, 0))],
        out_specs=pl.BlockSpec((tm, tn), lambda e, j, i: (i, e * nj + j)),
        scratch_shapes=[pltpu.VMEM((k, tn), BF16), pltpu.VMEM((k, tn), BF16)],
        compiler_params=_cp(3),
        name="moe_gu",
    )(a, w_gu, w_gu, gates_t)


def _down_ln_kernel(a_ref, w_ref, x_ref, g_ref, b_ref, of_ref, ob_ref, row_ref, *, nj, nk, tn):
    j = pl.program_id(1)
    k = pl.program_id(2)
    y = _dot(a_ref[...], w_ref[...].astype(BF16))

    @pl.when(k == 0)
    def _():
        row_ref[j] = y

    @pl.when(k > 0)
    def _():
        row_ref[j] = row_ref[j] + y

    @pl.when((j == nj - 1) & (k == nk - 1))
    def _():
        zs = [DN_ALPHA * x_ref[:, jj * tn:(jj + 1) * tn] + row_ref[jj] for jj in range(nj)]
        tot = zs[0].sum(axis=-1, keepdims=True)
        for z in zs[1:]:
            tot = tot + z.sum(axis=-1, keepdims=True)
        mu = tot * (1.0 / D_MODEL)
        sq = None
        for z in zs:
            t = jnp.square(z - mu).sum(axis=-1, keepdims=True)
            sq = t if sq is None else sq + t
        rstd = lax.rsqrt(sq * (1.0 / D_MODEL) + LN_EPS)
        for jj, z in enumerate(zs):
            sl = slice(jj * tn, (jj + 1) * tn)
            o = (z - mu) * rstd * g_ref[:, sl] + b_ref[:, sl]
            of_ref[:, sl] = o
            ob_ref[:, sl] = o.astype(BF16)


def down_ln(a, w, layer, x, g, b, tm, tn, tk):
    m, kk = a.shape
    tm = min(tm, m)
    n = D_MODEL
    assert kk % tk == 0 and n % tn == 0 and m % tm == 0
    nj, nk = n // tn, kk // tk
    return pl.pallas_call(
        functools.partial(_down_ln_kernel, nj=nj, nk=nk, tn=tn),
        out_shape=[jax.ShapeDtypeStruct((m, n), F32), jax.ShapeDtypeStruct((m, n), BF16)],
        grid=(m // tm, nj, nk),
        in_specs=[pl.BlockSpec((tm, tk), lambda i, j, k: (i, k)),
                  pl.BlockSpec((None, tk, tn), lambda i, j, k: (layer, k, j)),
                  pl.BlockSpec((tm, n), lambda i, j, k: (i, 0)),
                  pl.BlockSpec((1, n), lambda i, j, k: (0, 0)),
                  pl.BlockSpec((1, n), lambda i, j, k: (0, 0))],
        out_specs=[pl.BlockSpec((tm, n), lambda i, j, k: (i, 0)),
                   pl.BlockSpec((tm, n), lambda i, j, k: (i, 0))],
        scratch_shapes=[pltpu.VMEM((nj, tm, tn), F32)],
        compiler_params=_cp(3),
        name="down_ln",
    )(a, w, x, g, b)


def _router_kernel(x_ref, w_ref, o_ref):
    x = x_ref[...]
    w = w_ref[...]
    xh = x.astype(BF16)
    xl = (x - xh.astype(F32)).astype(BF16)
    wh = w.astype(BF16)
    wl = (w - wh.astype(F32)).astype(BF16)
    lg = _dot(xh, wh) + (_dot(xh, wl) + _dot(xl, wh))
    lane = lax.broadcasted_iota(jnp.int32, lg.shape, 1)
    lg = jnp.where(lane < N_EXPERTS, lg, -jnp.inf)
    m1 = lg.max(axis=-1, keepdims=True)
    i1 = jnp.where(lg == m1, lane, LANES).min(axis=-1, keepdims=True)
    lg2 = jnp.where(lane == i1, -jnp.inf, lg)
    m2 = lg2.max(axis=-1, keepdims=True)
    i2 = jnp.where(lg2 == m2, lane, LANES).min(axis=-1, keepdims=True)
    e2 = jnp.exp(m2 - m1)
    den = 1.0 + e2
    o_ref[...] = jnp.where(lane == i1, 1.0 / den, 0.0) + jnp.where(lane == i2, e2 / den, 0.0)


def router(x, w_pad, tm):
    m, k = x.shape
    tm = min(tm, m)
    return pl.pallas_call(
        _router_kernel,
        out_shape=jax.ShapeDtypeStruct((m, LANES), F32),
        grid=(m // tm,),
        in_specs=[pl.BlockSpec((tm, k), lambda i: (i, 0)),
                  pl.BlockSpec((k, LANES), lambda i: (0, 0))],
        out_specs=pl.BlockSpec((tm, LANES), lambda i: (i, 0)),
        compiler_params=_cp(1),
        name="router",
    )(x, w_pad)


def _sort_key(x):
    k = pltpu.bitcast(x, jnp.int32)
    return jnp.where(k < 0, k ^ jnp.int32(0x7FFFFFFF), k)


def _kth_largest_key(count_ge, n_sel, shape):
    zero = jnp.zeros(shape, jnp.int32)
    t0 = jnp.where(count_ge(zero) >= n_sel, zero, jnp.full(shape, INT_MIN, jnp.int32))

    def body(i, t):
        cand = t | jnp.left_shift(jnp.int32(1), 30 - i)
        return jnp.where(count_ge(cand) >= n_sel, cand, t)

    return lax.fori_loop(0, 31, body, t0)


def _idx_mask_kernel(iq_ref, iw_ref, ikt_ref, o_ref, key_ref, *, tq, n_sel):
    i = pl.program_id(1)
    s_len = ikt_ref.shape[1]
    wf = iw_ref[...] * (IDX_DIM ** -0.5)
    ikt = ikt_ref[...]
    sc = None
    for h in range(IDX_HEADS):
        d = _dot(iq_ref[:, h * IDX_DIM:(h + 1) * IDX_DIM], ikt)
        t = jnp.maximum(d, 0.0) * wf[:, h:h + 1]
        sc = t if sc is None else sc + t
    qpos = i * tq + lax.broadcasted_iota(jnp.int32, (tq, s_len), 0)
    kpos = lax.broadcasted_iota(jnp.int32, (tq, s_len), 1)
    valid = kpos <= qpos
    sc = jnp.where(valid, sc, NEG)
    key_ref[...] = _sort_key(sc)

    def count_ge(cand):
        return jnp.where(key_ref[...] >= cand, 1.0, 0.0).sum(axis=-1, keepdims=True)

    thr = _kth_largest_key(count_ge, float(n_sel), (tq, 1))
    sel = (key_ref[...] >= thr) & valid
    o_ref[...] = jnp.where(sel, 0.0, NEG).astype(o_ref.dtype)


def idx_mask(iq, iw, ikt, n_batch, tq):
    m = iq.shape[0]
    s_len = ikt.shape[2]
    nq = s_len // tq
    n_sel = min(TOPK_MAX, s_len // 4)
    return pl.pallas_call(
        functools.partial(_idx_mask_kernel, tq=tq, n_sel=n_sel),
        out_shape=jax.ShapeDtypeStruct((n_batch, s_len, s_len), BF16),
        grid=(n_batch, nq),
        in_specs=[pl.BlockSpec((tq, IDX_HEADS * IDX_DIM), lambda b, i: (b * nq + i, 0)),
                  pl.BlockSpec((tq, IDX_HEADS), lambda b, i: (b * nq + i, 0)),
                  pl.BlockSpec((None, IDX_DIM, s_len), lambda b, i: (b, 0, 0))],
        out_specs=pl.BlockSpec((None, tq, s_len), lambda b, i: (b, i, 0)),
        scratch_shapes=[pltpu.VMEM((tq, s_len), jnp.int32)],
        compiler_params=_cp(2),
        name="idx_mask",
    )(iq, iw, ikt)


def _rel_bucket(dist):
    max_exact = REL_BUCKETS // 2
    large = max_exact + (jnp.log(jnp.maximum(dist, 1).astype(F32) / max_exact)
                         / math.log(REL_MAX_DIST / max_exact) * (REL_BUCKETS - max_exact)).astype(jnp.int32)
    large = jnp.minimum(large, REL_BUCKETS - 1)
    return jnp.where(dist < max_exact, dist, large)


def _bias_tiles_kernel(rb_ref, o_ref, *, t):
    d = pl.program_id(0)
    ii = lax.broadcasted_iota(jnp.int32, (t, t), 0)
    jj = lax.broadcasted_iota(jnp.int32, (t, t), 1)
    bucket = _rel_bucket(jnp.maximum(d * t + ii - jj, 0))
    for h in range(N_HEADS):
        acc = jnp.zeros((t, t), F32)
        for bkt in range(REL_BUCKETS):
            acc = jnp.where(bucket == bkt, rb_ref[bkt, h], acc)
        o_ref[h] = acc


def bias_tiles(rel_bias, t):
    assert 2 * t - (t - 1) >= REL_MAX_DIST
    return pl.pallas_call(
        functools.partial(_bias_tiles_kernel, t=t),
        out_shape=jax.ShapeDtypeStruct((3, N_HEADS, t, t), F32),
        grid=(3,),
        in_specs=[pl.BlockSpec(memory_space=pltpu.SMEM)],
        out_specs=pl.BlockSpec((None, N_HEADS, t, t), lambda d: (d, 0, 0, 0)),
        compiler_params=_cp(1),
        name="bias_tiles",
    )(rel_bias)


def _attn_kernel(q_ref, k_ref, v_ref, mask_ref, tz_ref, o_ref, m_ref, l_ref, acc_ref):
    i = pl.program_id(1)
    j = pl.program_id(2)

    @pl.when(j == 0)
    def _():
        m_ref[...] = jnp.full(m_ref.shape, -jnp.inf, F32)
        l_ref[...] = jnp.zeros(l_ref.shape, F32)
        acc_ref[...] = jnp.zeros(acc_ref.shape, F32)

    @pl.when(j <= i)
    def _():
        maskf = mask_ref[...].astype(F32)
        for h in range(N_HEADS):
            sl = slice(h * HEAD_DIM, (h + 1) * HEAD_DIM)
            s = _dot_nt(q_ref[:, sl], k_ref[:, sl]) + tz_ref[h] + maskf
            m_prev = m_ref[h][:, :1]
            m_new = jnp.maximum(m_prev, s.max(axis=-1, keepdims=True))
            alpha = jnp.exp(m_prev - m_new)
            p = jnp.exp(s - m_new)
            l_new = alpha * l_ref[h][:, :1] + p.sum(axis=-1, keepdims=True)
            acc_ref[:, sl] = alpha * acc_ref[:, sl] + _dot(p.astype(BF16), v_ref[:, sl])
            m_ref[h] = jnp.broadcast_to(m_new, m_ref.shape[1:])
            l_ref[h] = jnp.broadcast_to(l_new, l_ref.shape[1:])

    @pl.when(j == i)
    def _():
        for h in range(N_HEADS):
            sl = slice(h * HEAD_DIM, (h + 1) * HEAD_DIM)
            o_ref[:, sl] = (acc_ref[:, sl] / l_ref[h][:, :1]).astype(o_ref.dtype)


def attn(q, k, v, mask, tz, n_batch, t):
    m = q.shape[0]
    nq = m // n_batch // t
    kv_idx = lambda b, i, j: (b * nq + jnp.minimum(j, i), 0)
    return pl.pallas_call(
        _attn_kernel,
        out_shape=jax.ShapeDtypeStruct((m, D_MODEL), BF16),
        grid=(n_batch, nq, nq),
        in_specs=[pl.BlockSpec((t, D_MODEL), lambda b, i, j: (b * nq + i, 0)),
                  pl.BlockSpec((t, D_MODEL), kv_idx),
                  pl.BlockSpec((t, D_MODEL), kv_idx),
                  pl.BlockSpec((None, t, t), lambda b, i, j: (b, i, jnp.minimum(j, i))),
                  pl.BlockSpec((None, N_HEADS, t, t),
                               lambda b, i, j: (jnp.clip(i - j, 0, 2), 0, 0, 0))],
        out_specs=pl.BlockSpec((t, D_MODEL), lambda b, i, j: (b * nq + i, 0)),
        scratch_shapes=[pltpu.VMEM((N_HEADS, t, LANES), F32),
                        pltpu.VMEM((N_HEADS, t, LANES), F32),
                        pltpu.VMEM((t, D_MODEL), F32)],
        compiler_params=_cp(3),
        name="attn",
    )(q, k, v, mask, tz)


def _ret_kernel(lg_ref, q_ref, k_ref, v_ref, g_ref, gn_ref, s0_ref, z_ref, sout_ref, state_ref,
                *, c_pad, c_true, nc):
    h = pl.program_id(1)
    c = pl.program_id(2)
    lg = lg_ref[h]

    @pl.when(c == 0)
    def _():
        state_ref[...] = s0_ref[...]

    q = q_ref[...]
    k = k_ref[...]
    v = v_ref[...]
    ii = lax.broadcasted_iota(jnp.int32, (c_pad, c_pad), 0)
    jj = lax.broadcasted_iota(jnp.int32, (c_pad, c_pad), 1)
    rel = (ii - jj).astype(F32)
    decay = jnp.where(rel >= 0, jnp.exp(lg * jnp.maximum(rel, 0.0)), 0.0)
    scores = _dot_nt(q, k) * decay
    inner = _dot(scores.astype(BF16), v)
    s_prev = state_ref[...]
    idx = lax.broadcasted_iota(jnp.int32, (c_pad, 1), 0).astype(F32)
    cross = _dot(q, s_prev.astype(BF16)) * jnp.exp(lg * (idx + 1.0))
    o = inner + cross
    w_state = jnp.exp(lg * (c_true - 1.0 - idx))
    kw = (k.astype(F32) * w_state).astype(BF16)
    s_new = s_prev * jnp.exp(lg * c_true) + _dot_tn(kw, v)
    state_ref[...] = s_new

    mu = o.mean(axis=-1, keepdims=True)
    var = jnp.square(o - mu).mean(axis=-1, keepdims=True)
    on = (o - mu) * lax.rsqrt(var + LN_EPS) * gn_ref[...]
    z_ref[...] = (_silu(g_ref[...]) * on).astype(BF16)

    @pl.when(c == nc - 1)
    def _():
        sout_ref[...] = s_new


def retention(lg, qk, v, g, gn, s0, n_batch, c_pad, c_true):
    m = qk.shape[0]
    nc = m // n_batch // c_pad
    row = lambda b, h, c: b * nc + c
    return pl.pallas_call(
        functools.partial(_ret_kernel, c_pad=c_pad, c_true=float(c_true), nc=nc),
        out_shape=[jax.ShapeDtypeStruct((m, RET_VW), BF16),
                   jax.ShapeDtypeStruct((n_batch, RET_HEADS, RET_DK, RET_DV), F32)],
        grid=(n_batch, RET_HEADS, nc),
        in_specs=[pl.BlockSpec(memory_space=pltpu.SMEM),
                  pl.BlockSpec((c_pad, RET_DK), lambda b, h, c: (row(b, h, c), h)),
                  pl.BlockSpec((c_pad, RET_DK), lambda b, h, c: (row(b, h, c), RET_HEADS + h)),
                  pl.BlockSpec((c_pad, RET_DV), lambda b, h, c: (row(b, h, c), h)),
                  pl.BlockSpec((c_pad, RET_DV), lambda b, h, c: (row(b, h, c), h)),
                  pl.BlockSpec((1, RET_DV), lambda b, h, c: (0, h)),
                  pl.BlockSpec((None, None, RET_DK, RET_DV), lambda b, h, c: (b, h, 0, 0))],
        out_specs=[pl.BlockSpec((c_pad, RET_DV), lambda b, h, c: (row(b, h, c), h)),
                   pl.BlockSpec((None, None, RET_DK, RET_DV), lambda b, h, c: (b, h, 0, 0))],
        scratch_shapes=[pltpu.VMEM((RET_DK, RET_DV), F32)],
        compiler_params=_cp(3),
        name="retention",
    )(lg, qk, qk, v, g, gn, s0)


def _dec_scores_kernel(pt_ref, ik_ref, iq_ref, iw_ref, o_ref, *, t_len):
    del pt_ref
    d = _dot_nt(iq_ref[...], ik_ref[...].astype(BF16))
    r = jnp.maximum(d, 0.0) * (iw_ref[...] * (IDX_DIM ** -0.5))
    o_ref[...] = r.reshape(t_len, IDX_HEADS, r.shape[-1]).sum(axis=1)


def dec_scores(page_table, cache_idx_k, layer, iq, iw, t_len):
    n_batch, n_pages = page_table.shape
    rows = t_len * IDX_HEADS
    return pl.pallas_call(
        functools.partial(_dec_scores_kernel, t_len=t_len),
        out_shape=jax.ShapeDtypeStruct((n_batch, t_len, n_pages * PAGE_SIZE), F32),
        grid_spec=pltpu.PrefetchScalarGridSpec(
            num_scalar_prefetch=1,
            grid=(n_batch, n_pages),
            in_specs=[pl.BlockSpec((None, None, PAGE_SIZE, IDX_DIM),
                                   lambda b, p, pt: (layer, pt[b, p], 0, 0)),
                      pl.BlockSpec((None, rows, IDX_DIM), lambda b, p, pt: (b, 0, 0)),
                      pl.BlockSpec((None, rows, 1), lambda b, p, pt: (b, 0, 0))],
            out_specs=pl.BlockSpec((None, t_len, PAGE_SIZE), lambda b, p, pt: (b, 0, p))),
        compiler_params=_cp(2),
        name="dec_scores",
    )(page_table, cache_idx_k, iq, iw)


def _dec_mask_kernel(sc_ref, iq_ref, iw_ref, ikn_ref, op_ref, on_ref, key_ref, *, t_len, n_sel, n_pad):
    d = _dot_nt(iq_ref[...], ikn_ref[...])
    r = jnp.maximum(d, 0.0) * (iw_ref[...] * (IDX_DIM ** -0.5))
    sn = r.reshape(t_len, IDX_HEADS, n_pad).sum(axis=1)
    qi = lax.broadcasted_iota(jnp.int32, (t_len, n_pad), 0)
    si = lax.broadcasted_iota(jnp.int32, (t_len, n_pad), 1)
    valid_n = si <= qi
    key_n = jnp.where(si < t_len, _sort_key(jnp.where(valid_n, sn, NEG)), INT_MIN)
    key_ref[...] = _sort_key(sc_ref[...])

    def count_ge(cand):
        cp = jnp.where(key_ref[...] >= cand, 1.0, 0.0).sum(axis=-1, keepdims=True)
        cn = jnp.where(key_n >= cand, 1.0, 0.0).sum(axis=-1, keepdims=True)
        return cp + cn

    thr = _kth_largest_key(count_ge, float(n_sel), (t_len, 1))
    mp = jnp.where(key_ref[...] >= thr, 0.0, NEG)
    mn = jnp.where((key_n >= thr) & valid_n, 0.0, NEG)
    for t in range(t_len):
        rows = slice(t * N_HEADS, (t + 1) * N_HEADS)
        op_ref[rows, :] = jnp.broadcast_to(mp[t:t + 1, :], (N_HEADS, mp.shape[1]))
        on_ref[rows, :] = jnp.broadcast_to(mn[t:t + 1, :], (N_HEADS, n_pad))


def dec_mask(sc, iq, iw, ik_new, t_len, n_pad):
    n_batch, _, past = sc.shape
    rows = t_len * IDX_HEADS
    n_sel = min(TOPK_MAX, (past + t_len) // 4)
    return pl.pallas_call(
        functools.partial(_dec_mask_kernel, t_len=t_len, n_sel=n_sel, n_pad=n_pad),
        out_shape=[jax.ShapeDtypeStruct((n_batch, t_len * N_HEADS, past), F32),
                   jax.ShapeDtypeStruct((n_batch, t_len * N_HEADS, n_pad), F32)],
        grid=(n_batch,),
        in_specs=[pl.BlockSpec((None, t_len, past), lambda b: (b, 0, 0)),
                  pl.BlockSpec((None, rows, IDX_DIM), lambda b: (b, 0, 0)),
                  pl.BlockSpec((None, rows, 1), lambda b: (b, 0, 0)),
                  pl.BlockSpec((None, n_pad, IDX_DIM), lambda b: (b, 0, 0))],
        out_specs=[pl.BlockSpec((None, t_len * N_HEADS, past), lambda b: (b, 0, 0)),
                   pl.BlockSpec((None, t_len * N_HEADS, n_pad), lambda b: (b, 0, 0))],
        scratch_shapes=[pltpu.VMEM((t_len, past), jnp.int32)],
        compiler_params=_cp(1),
        name="dec_mask",
    )(sc, iq, iw, ik_new)


def _dec_attn_kernel(pt_ref, q_ref, k_ref, v_ref, kn_ref, vn_ref, mp_ref, mn_ref, rbt_ref, o_ref,
                     qbd_ref, m_ref, l_ref, acc_ref, *, t_len, n_pages, n_pad):
    del pt_ref
    p = pl.program_id(1)
    rows = t_len * N_HEADS
    row = lax.broadcasted_iota(jnp.int32, (rows, D_MODEL), 0)
    col = lax.broadcasted_iota(jnp.int32, (rows, D_MODEL), 1)
    head_mask = (col // HEAD_DIM) == (row % N_HEADS)

    @pl.when(p == 0)
    def _():
        qrows = jnp.concatenate([jnp.broadcast_to(q_ref[t:t + 1, :], (N_HEADS, D_MODEL))
                                 for t in range(t_len)], axis=0)
        qbd_ref[...] = jnp.where(head_mask, qrows, 0.0).astype(BF16)
        m_ref[...] = jnp.full(m_ref.shape, -jnp.inf, F32)
        l_ref[...] = jnp.zeros(l_ref.shape, F32)
        acc_ref[...] = jnp.zeros(acc_ref.shape, F32)

    past = n_pages * PAGE_SIZE

    def step(kb, vb, maskadd, kpos0, width):
        qpos = past + lax.broadcasted_iota(jnp.int32, (rows, width), 0) // N_HEADS
        kpos = kpos0 + lax.broadcasted_iota(jnp.int32, (rows, width), 1)
        bucket = _rel_bucket(jnp.maximum(qpos - kpos, 0))
        bias = jnp.zeros((rows, width), F32)
        for bkt in range(REL_BUCKETS):
            bias = jnp.where(bucket == bkt, rbt_ref[:, bkt:bkt + 1], bias)
        s = _dot_nt(qbd_ref[...], kb) + bias + maskadd
        m_prev = m_ref[:, :1]
        m_new = jnp.maximum(m_prev, s.max(axis=-1, keepdims=True))
        alpha = jnp.exp(m_prev - m_new)
        pr = jnp.exp(s - m_new)
        l_new = alpha * l_ref[:, :1] + pr.sum(axis=-1, keepdims=True)
        acc_ref[...] = alpha * acc_ref[...] + _dot(pr.astype(BF16), vb)
        m_ref[...] = jnp.broadcast_to(m_new, m_ref.shape)
        l_ref[...] = jnp.broadcast_to(l_new, l_ref.shape)

    @pl.when(p < n_pages)
    def _():
        step(k_ref[...].astype(BF16), v_ref[...].astype(BF16), mp_ref[...], p * PAGE_SIZE, PAGE_SIZE)

    @pl.when(p == n_pages)
    def _():
        step(kn_ref[...].astype(BF16), vn_ref[...].astype(BF16), mn_ref[...], past, n_pad)
        out = jnp.where(head_mask, acc_ref[...] / l_ref[:, :1], 0.0)
        o_ref[...] = out.reshape(t_len, N_HEADS, D_MODEL).sum(axis=1).astype(o_ref.dtype)


def dec_attn(page_table, q, cache_k, cache_v, layer, k_new, v_new, mask_past, mask_new, rbt, t_len, n_pad):
    n_batch, n_pages = page_table.shape
    rows = t_len * N_HEADS
    page_idx = lambda b, p, pt: (layer, pt[b, jnp.minimum(p, n_pages - 1)], 0, 0)
    return pl.pallas_call(
        functools.partial(_dec_attn_kernel, t_len=t_len, n_pages=n_pages, n_pad=n_pad),
        out_shape=jax.ShapeDtypeStruct((n_batch, t_len, D_MODEL), F32),
        grid_spec=pltpu.PrefetchScalarGridSpec(
            num_scalar_prefetch=1,
            grid=(n_batch, n_pages + 1),
            in_specs=[pl.BlockSpec((None, t_len, D_MODEL), lambda b, p, pt: (b, 0, 0)),
                      pl.BlockSpec((None, None, PAGE_SIZE, D_MODEL), page_idx),
                      pl.BlockSpec((None, None, PAGE_SIZE, D_MODEL), page_idx),
                      pl.BlockSpec((None, n_pad, D_MODEL), lambda b, p, pt: (b, 0, 0)),
                      pl.BlockSpec((None, n_pad, D_MODEL), lambda b, p, pt: (b, 0, 0)),
                      pl.BlockSpec((None, rows, PAGE_SIZE),
                                   lambda b, p, pt: (b, 0, jnp.minimum(p, n_pages - 1))),
                      pl.BlockSpec((None, rows, n_pad), lambda b, p, pt: (b, 0, 0)),
                      pl.BlockSpec((rows, REL_BUCKETS), lambda b, p, pt: (0, 0))],
            out_specs=pl.BlockSpec((None, t_len, D_MODEL), lambda b, p, pt: (b, 0, 0)),
            scratch_shapes=[pltpu.VMEM((rows, D_MODEL), BF16),
                            pltpu.VMEM((rows, LANES), F32),
                            pltpu.VMEM((rows, LANES), F32),
                            pltpu.VMEM((rows, D_MODEL), F32)]),
        compiler_params=_cp(2),
        name="dec_attn",
    )(page_table, q, cache_k, cache_v, k_new, v_new, mask_past, mask_new, rbt)


ATT_T = 256
IDX_TQ = 128
DEC_PAD = 16


def _attn_in_proj(xb, w_in, layer, tm, q_dtype=BF16):
    q, = proj(xb, w_in, layer, 0, D_MODEL, 1024, tm, [q_dtype], scale=HEAD_DIM ** -0.5, name="proj_q")
    kv, kvb = proj(xb, w_in, layer, D_MODEL, 2 * D_MODEL, 1024, tm, [F32, BF16], name="proj_kv")
    iq, = proj(xb, w_in, layer, 3 * D_MODEL, IDX_HEADS * IDX_DIM, 1024, tm, [BF16], name="proj_iq")
    o4 = 3 * D_MODEL + IDX_HEADS * IDX_DIM
    w_tail = jnp.pad(w_in[layer, :, o4:], ((0, 0), (0, 2 * LANES - (IDX_DIM + IDX_HEADS))))
    tail, = proj(xb, w_tail, 0, 0, 2 * LANES, 2 * LANES, tm, [F32], name="proj_tail")
    ik = tail[:, :IDX_DIM]
    iw = tail[:, IDX_DIM:IDX_DIM + IDX_HEADS] * IDX_HEADS ** -0.5
    return q, kv, kvb, iq, ik, iw


def _attn_prompt(xb, w_in, w_out_args, layer, rel_tiles, n_batch):
    m = xb.shape[0]
    t_len = m // n_batch
    q, kv, kvb, iq, ik, iw = _attn_in_proj(xb, w_in, layer, 1024)
    ikt = jnp.swapaxes(ik.astype(BF16).reshape(n_batch, t_len, IDX_DIM), 1, 2)
    mask = idx_mask(iq, iw, ikt, n_batch, IDX_TQ)
    o = attn(q, kvb[:, :D_MODEL], kvb[:, D_MODEL:], mask, rel_tiles, n_batch, ATT_T)
    k = kv[:, :D_MODEL].reshape(n_batch, t_len, N_HEADS, HEAD_DIM)
    v = kv[:, D_MODEL:].reshape(n_batch, t_len, N_HEADS, HEAD_DIM)
    return o, k, v, ik.reshape(n_batch, t_len, IDX_DIM)


def _attn_sample(xb, w_in, layer, cache_k, cache_v, cache_idx_k, page_table, rbt, n_batch):
    m = xb.shape[0]
    t_len = m // n_batch
    q, kv, _, iq, ik, iw = _attn_in_proj(xb, w_in, layer, m, q_dtype=F32)
    iq_r = iq.reshape(n_batch, t_len * IDX_HEADS, IDX_DIM)
    iw_r = iw.reshape(n_batch, t_len * IDX_HEADS, 1)
    sc = dec_scores(page_table, cache_idx_k, layer, iq_r, iw_r, t_len)
    pad3 = lambda a: jnp.pad(a.reshape(n_batch, t_len, -1), ((0, 0), (0, DEC_PAD - t_len), (0, 0)))
    mask_past, mask_new = dec_mask(sc, iq_r, iw_r, pad3(ik).astype(BF16), t_len, DEC_PAD)
    n_phys = cache_k.shape[1]
    ck = cache_k.reshape(cache_k.shape[0], n_phys, PAGE_SIZE, D_MODEL)
    cv = cache_v.reshape(cache_v.shape[0], n_phys, PAGE_SIZE, D_MODEL)
    k_new, v_new = kv[:, :D_MODEL], kv[:, D_MODEL:]
    o = dec_attn(page_table, q.reshape(n_batch, t_len, D_MODEL), ck, cv, layer,
                 pad3(k_new), pad3(v_new), mask_past, mask_new, rbt, t_len, DEC_PAD)
    return (o.reshape(m, D_MODEL).astype(BF16), k_new.reshape(n_batch, t_len, N_HEADS, HEAD_DIM),
            v_new.reshape(n_batch, t_len, N_HEADS, HEAD_DIM), ik.reshape(n_batch, t_len, IDX_DIM))


def _rot_tables(pos):
    half = RET_DK // 2
    inv = jnp.exp(-math.log(ROPE_BASE) * jnp.arange(half, dtype=F32) / half)
    ang = pos.astype(F32)[:, None] * inv[None, :]
    return jnp.cos(ang), jnp.sin(ang)


def _retention_mixer(xb, w_in, layer, gn, s0, lg, cos, sin, n_batch, chunk, pad_to):
    m = xb.shape[0]
    t_len = m // n_batch
    tm = min(1024, m)
    qk = proj_rot(xb, w_in, layer, cos, sin, 1024, tm)
    v, = proj(xb, w_in, layer, 2 * D_MODEL, RET_VW, 1024, tm, [BF16], name="proj_rv")
    g, = proj(xb, w_in, layer, 2 * D_MODEL + RET_VW, RET_VW, 1024, tm, [F32], name="proj_rg")
    if pad_to != t_len:
        padr = lambda a: jnp.pad(a.reshape(n_batch, t_len, -1),
                                 ((0, 0), (0, pad_to - t_len), (0, 0))).reshape(n_batch * pad_to, -1)
        z, s_fin = retention(lg, padr(qk), padr(v), padr(g), gn, s0, n_batch, pad_to, chunk)
        z = z.reshape(n_batch, pad_to, RET_VW)[:, :t_len].reshape(m, RET_VW)
    else:
        z, s_fin = retention(lg, qk, v, g, gn, s0, n_batch, chunk, chunk)
    return z, s_fin


def _moe_gates(xf, w_router_pad):
    dense = router(xf, w_router_pad, 512)
    return jnp.swapaxes(dense[:, :N_EXPERTS], 0, 1)[:, :, None]


def kernel(x_prompt, x_sample, cache_k, cache_v, cache_idx_k, state_ret, page_table, rel_bias,
           w_in_attn, w_out_attn, w_in_ret, ret_gn_g, w_out_ret, w_ffn_gu, w_ffn_down,
           w_router, w_exp_gu, w_exp_down, ln_g, ln_b):
    bp, tp, _ = x_prompt.shape
    bs, ts, _ = x_sample.shape
    past = page_table.shape[1] * PAGE_SIZE
    mp, ms = bp * tp, bs * ts
    xp = x_prompt.reshape(mp, D_MODEL)
    xs = x_sample.reshape(ms, D_MODEL)
    xpb, xsb = xp.astype(BF16), xs.astype(BF16)

    rel_tiles = bias_tiles(rel_bias, ATT_T)
    rbt = jnp.tile(rel_bias.T, (ts, 1))
    lg = jnp.log1p(-jnp.exp2(-5.0 - jnp.arange(RET_HEADS, dtype=F32)))
    cos_p, sin_p = _rot_tables(jnp.tile(jnp.arange(tp, dtype=jnp.int32), bp))
    cos_s, sin_s = _rot_tables(jnp.tile(past + jnp.arange(ts, dtype=jnp.int32), bs))
    w_exp_down_flat = w_exp_down.reshape(w_exp_down.shape[0], N_EXPERTS * D_FF_EXPERT, D_MODEL)
    zero_state = jnp.zeros((bp, RET_HEADS, RET_DK, RET_DV), F32)

    kp_l, vp_l, ikp_l, sp_l = [], [], [], []
    ks_l, vs_l, iks_l, ss_l = [], [], [], []
    for i in range(DEPTH):
        j = i // 2
        g0, b0 = ln_g[i, 0][None, :], ln_b[i, 0][None, :]
        g1, b1 = ln_g[i, 1][None, :], ln_b[i, 1][None, :]
        if i % 2 == 0:
            op, kp, vp, ikp = _attn_prompt(xpb, w_in_attn, None, j, rel_tiles, bp)
            os_, k_s, v_s, iks = _attn_sample(xsb, w_in_attn, j, cache_k, cache_v, cache_idx_k,
                                              page_table, rbt, bs)
            kp_l.append(kp); vp_l.append(vp); ikp_l.append(ikp)
            ks_l.append(k_s); vs_l.append(v_s); iks_l.append(iks)
            xp, xpb = down_ln(op, w_out_attn, j, xp, g0, b0, 512, 512, D_MODEL)
            xs, xsb = down_ln(os_, w_out_attn, j, xs, g0, b0, 512, 512, D_MODEL)
        else:
            gn = ret_gn_g[j][None, :]
            zp, sp = _retention_mixer(xpb, w_in_ret, j, gn, zero_state, lg, cos_p, sin_p,
                                      bp, min(RET_CHUNK, tp), tp)
            zs, ss = _retention_mixer(xsb, w_in_ret, j, gn, state_ret[j], lg, cos_s, sin_s,
                                      bs, ts, DEC_PAD)
            sp_l.append(sp); ss_l.append(ss)
            xp, xpb = down_ln(zp, w_out_ret, j, xp, g0, b0, 512, 512, RET_VW // 2)
            xs, xsb = down_ln(zs, w_out_ret, j, xs, g0, b0, 512, 512, RET_VW // 2)
        if i % 2 == 0:
            hp = ffn_gu(xpb, w_ffn_gu, j, 512, 1024)
            hs = ffn_gu(xsb, w_ffn_gu, j, 512, 1024)
            xp, xpb = down_ln(hp, w_ffn_down, j, xp, g1, b1, 512, 512, D_FF // 2)
            xs, xsb = down_ln(hs, w_ffn_down, j, xs, g1, b1, 512, 512, D_FF // 2)
        else:
            w_r = jnp.pad(w_router[j], ((0, 0), (0, LANES - N_EXPERTS)))
            hp = moe_gu(xpb, w_exp_gu, j, _moe_gates(xp, w_r), 256, 1024)
            hs = moe_gu(xsb, w_exp_gu, j, _moe_gates(xs, w_r), 256, 1024)
            xp, xpb = down_ln(hp, w_exp_down_flat, j, xp, g1, b1, 512, 512, D_FF_EXPERT)
            xs, xsb = down_ln(hs, w_exp_down_flat, j, xs, g1, b1, 512, 512, D_FF_EXPERT)

    return (xp.reshape(bp, tp, D_MODEL), xs.reshape(bs, ts, D_MODEL),
            jnp.stack(kp_l), jnp.stack(vp_l), jnp.stack(ikp_l), jnp.stack(sp_l),
            jnp.stack(ks_l), jnp.stack(vs_l), jnp.stack(iks_l), jnp.stack(ss_l))
```

```python
import functools
import math

import jax
import jax.numpy as jnp
from jax import lax
from jax.experimental import pallas as pl
from jax.experimental.pallas import tpu as pltpu

F32 = jnp.float32
BF16 = jnp.bfloat16

D_MODEL = 2048
PAGE_SIZE = 128
N_HEADS = 16
HEAD_DIM = 128
IDX_HEADS = 16
IDX_DIM = 128
TOPK_MAX = 256
REL_BUCKETS = 32
REL_MAX_DIST = 128
RET_HEADS = 8
RET_DK = 256
RET_DV = 512
RET_VW = RET_HEADS * RET_DV
RET_CHUNK = 128
ROPE_BASE = 10000.0
D_FF = 5632
N_EXPERTS = 8
D_FF_EXPERT = 2816
DEPTH = 4
DN_ALPHA = (2 * DEPTH) ** 0.25
LN_EPS = 1e-5
NEG = -1e30

LANES = 128
VMEM_LIMIT = 56 * 1024 * 1024
INT_MIN = -2 ** 31


def _cp(n_axes, vmem=VMEM_LIMIT):
    return pltpu.CompilerParams(dimension_semantics=("arbitrary",) * n_axes,
                                vmem_limit_bytes=vmem)


def _silu(x):
    return x * (1.0 / (1.0 + jnp.exp(-x)))


def _dot(a, b):
    return jnp.dot(a, b, preferred_element_type=F32)


def _dot_nt(a, b):
    return lax.dot_general(a, b, (((1,), (1,)), ((), ())), preferred_element_type=F32)


def _dot_tn(a, b):
    return lax.dot_general(a, b, (((0,), (0,)), ((), ())), preferred_element_type=F32)


def _proj_kernel(a_ref, w_ref, *rest, n_out, scale):
    out_refs = rest[:n_out]
    wb_ref = rest[n_out]

    @pl.when(pl.program_id(1) == 0)
    def _():
        wb_ref[...] = w_ref[...].astype(BF16)

    acc = _dot(a_ref[...], wb_ref[...])
    if scale != 1.0:
        acc = acc * scale
    for o in out_refs:
        o[...] = acc.astype(o.dtype)


def proj(a, w, layer, col0, ncols, tn, tm, out_dtypes, scale=1.0, name="proj"):
    m, k = a.shape
    tm = min(tm, m)
    assert col0 % tn == 0 and ncols % tn == 0 and m % tm == 0
    off = col0 // tn
    if w.ndim == 3:
        w_spec = pl.BlockSpec((None, k, tn), lambda j, i: (layer, 0, j + off))
    else:
        w_spec = pl.BlockSpec((k, tn), lambda j, i: (0, j + off))
    outs = pl.pallas_call(
        functools.partial(_proj_kernel, n_out=len(out_dtypes), scale=scale),
        out_shape=[jax.ShapeDtypeStruct((m, ncols), dt) for dt in out_dtypes],
        grid=(ncols // tn, m // tm),
        in_specs=[pl.BlockSpec((tm, k), lambda j, i: (i, 0)), w_spec],
        out_specs=[pl.BlockSpec((tm, tn), lambda j, i: (i, j)) for _ in out_dtypes],
        scratch_shapes=[pltpu.VMEM((k, tn), BF16)],
        compiler_params=_cp(2),
        name=name,
    )(a, w)
    return outs


def _proj_rot_kernel(a_ref, w_ref, cos_ref, sin_ref, o_ref, wb_ref, *, heads_per_tile, q_tiles):
    j = pl.program_id(0)

    @pl.when(pl.program_id(1) == 0)
    def _():
        wb_ref[...] = w_ref[...].astype(BF16)

    acc = _dot(a_ref[...], wb_ref[...])
    scale = jnp.where(j >= q_tiles, RET_DK ** -0.5, 1.0).astype(F32)
    c = cos_ref[...]
    s = sin_ref[...]
    half = RET_DK // 2
    for hh in range(heads_per_tile):
        x1 = acc[:, hh * RET_DK: hh * RET_DK + half]
        x2 = acc[:, hh * RET_DK + half: (hh + 1) * RET_DK]
        o_ref[:, hh * RET_DK: hh * RET_DK + half] = ((x1 * c - x2 * s) * scale).astype(BF16)
        o_ref[:, hh * RET_DK + half: (hh + 1) * RET_DK] = ((x1 * s + x2 * c) * scale).astype(BF16)


def proj_rot(a, w, layer, cos, sin, tn, tm):
    m, k = a.shape
    tm = min(tm, m)
    ncols = 2 * D_MODEL
    return pl.pallas_call(
        functools.partial(_proj_rot_kernel, heads_per_tile=tn // RET_DK, q_tiles=D_MODEL // tn),
        out_shape=jax.ShapeDtypeStruct((m, ncols), BF16),
        grid=(ncols // tn, m // tm),
        in_specs=[pl.BlockSpec((tm, k), lambda j, i: (i, 0)),
                  pl.BlockSpec((None, k, tn), lambda j, i: (layer, 0, j)),
                  pl.BlockSpec((tm, RET_DK // 2), lambda j, i: (i, 0)),
                  pl.BlockSpec((tm, RET_DK // 2), lambda j, i: (i, 0))],
        out_specs=pl.BlockSpec((tm, tn), lambda j, i: (i, j)),
        scratch_shapes=[pltpu.VMEM((k, tn), BF16)],
        compiler_params=_cp(2),
        name="proj_rot",
    )(a, w, cos, sin)


def _gu_kernel(a_ref, wg_ref, wu_ref, *rest, gated, inner_axis):
    if gated:
        gate_ref, o_ref, wgb_ref, wub_ref = rest
    else:
        o_ref, wgb_ref, wub_ref = rest

    @pl.when(pl.program_id(inner_axis) == 0)
    def _():
        wgb_ref[...] = wg_ref[...].astype(BF16)
        wub_ref[...] = wu_ref[...].astype(BF16)

    a = a_ref[...]
    h = _silu(_dot(a, wgb_ref[...])) * _dot(a, wub_ref[...])
    if gated:
        h = h * gate_ref[...]
    o_ref[...] = h.astype(BF16)


def ffn_gu(a, w_gu, layer, tn, tm):
    m, k = a.shape
    tm = min(tm, m)
    nj = D_FF // tn
    return pl.pallas_call(
        functools.partial(_gu_kernel, gated=False, inner_axis=1),
        out_shape=jax.ShapeDtypeStruct((m, D_FF), BF16),
        grid=(nj, m // tm),
        in_specs=[pl.BlockSpec((tm, k), lambda j, i: (i, 0)),
                  pl.BlockSpec((None, k, tn), lambda j, i: (layer, 0, j)),
                  pl.BlockSpec((None, k, tn), lambda j, i: (layer, 0, j + nj))],
        out_specs=pl.BlockSpec((tm, tn), lambda j, i: (i, j)),
        scratch_shapes=[pltpu.VMEM((k, tn), BF16), pltpu.VMEM((k, tn), BF16)],
        compiler_params=_cp(2),
        name="ffn_gu",
    )(a, w_gu, w_gu)


def moe_gu(a, w_gu, layer, gates_t, tn, tm):
    m, k = a.shape
    tm = min(tm, m)
    nj = D_FF_EXPERT // tn
    return pl.pallas_call(
        functools.partial(_gu_kernel, gated=True, inner_axis=2),
        out_shape=jax.ShapeDtypeStruct((m, N_EXPERTS * D_FF_EXPERT), BF16),
        grid=(N_EXPERTS, nj, m // tm),
        in_specs=[pl.BlockSpec((tm, k), lambda e, j, i: (i, 0)),
                  pl.BlockSpec((None, None, k, tn), lambda e, j, i: (layer, e, 0, j)),
                  pl.BlockSpec((None, None, k, tn), lambda e, j, i: (layer, e, 0, j + nj)),
                  pl.BlockSpec((None, tm, 1), lambda e, j, i: (e, i, 0))],
        out_specs=pl.BlockSpec((tm, tn), lambda e, j, i: (i, e * nj + j)),
        scratch_shapes=[pltpu.VMEM((k, tn), BF16), pltpu.VMEM((k, tn), BF16)],
        compiler_params=_cp(3),
        name="moe_gu",
    )(a, w_gu, w_gu, gates_t)


def _down_ln_kernel(a_ref, w_ref, x_ref, g_ref, b_ref, of_ref, ob_ref, row_ref, *, nj, nk, tn):
    k = pl.program_id(1)
    j = pl.program_id(2)
    y = _dot(a_ref[...], w_ref[...].astype(BF16))

    @pl.when(k == 0)
    def _():
        row_ref[j] = y

    @pl.when(k > 0)
    def _():
        row_ref[j] = row_ref[j] + y

    @pl.when((j == nj - 1) & (k == nk - 1))
    def _():
        zs = [DN_ALPHA * x_ref[:, jj * tn:(jj + 1) * tn] + row_ref[jj] for jj in range(nj)]
        tot = zs[0].sum(axis=-1, keepdims=True)
        for z in zs[1:]:
            tot = tot + z.sum(axis=-1, keepdims=True)
        mu = tot * (1.0 / D_MODEL)
        sq = None
        for z in zs:
            t = jnp.square(z - mu).sum(axis=-1, keepdims=True)
            sq = t if sq is None else sq + t
        rstd = lax.rsqrt(sq * (1.0 / D_MODEL) + LN_EPS)
        for jj, z in enumerate(zs):
            sl = slice(jj * tn, (jj + 1) * tn)
            o = (z - mu) * rstd * g_ref[:, sl] + b_ref[:, sl]
            of_ref[:, sl] = o
            ob_ref[:, sl] = o.astype(BF16)


def down_ln(a, w, layer, x, g, b, tm, tn, tk):
    m, kk = a.shape
    tm = min(tm, m)
    n = D_MODEL
    assert kk % tk == 0 and n % tn == 0 and m % tm == 0
    nj, nk = n // tn, kk // tk
    return pl.pallas_call(
        functools.partial(_down_ln_kernel, nj=nj, nk=nk, tn=tn),
        out_shape=[jax.ShapeDtypeStruct((m, n), F32), jax.ShapeDtypeStruct((m, n), BF16)],
        grid=(m // tm, nk, nj),
        in_specs=[pl.BlockSpec((tm, tk), lambda i, k, j: (i, k)),
                  pl.BlockSpec((None, tk, tn), lambda i, k, j: (layer, k, j)),
                  pl.BlockSpec((tm, n), lambda i, k, j: (i, 0)),
                  pl.BlockSpec((1, n), lambda i, k, j: (0, 0)),
                  pl.BlockSpec((1, n), lambda i, k, j: (0, 0))],
        out_specs=[pl.BlockSpec((tm, n), lambda i, k, j: (i, 0)),
                   pl.BlockSpec((tm, n), lambda i, k, j: (i, 0))],
        scratch_shapes=[pltpu.VMEM((nj, tm, tn), F32)],
        compiler_params=_cp(3),
        name="down_ln",
    )(a, w, x, g, b)


def _router_kernel(x_ref, w_ref, o_ref, cnt_ref, carry_ref):
    x = x_ref[...]
    w = w_ref[...]
    xh = x.astype(BF16)
    xl = (x - xh.astype(F32)).astype(BF16)
    wh = w.astype(BF16)
    wl = (w - wh.astype(F32)).astype(BF16)
    lg = _dot(xh, wh) + (_dot(xh, wl) + _dot(xl, wh))
    lane = lax.broadcasted_iota(jnp.int32, lg.shape, 1)
    lg = jnp.where(lane < N_EXPERTS, lg, -jnp.inf)
    m1 = lg.max(axis=-1, keepdims=True)
    i1 = jnp.where(lg == m1, lane, LANES).min(axis=-1, keepdims=True)
    lg2 = jnp.where(lane == i1, -jnp.inf, lg)
    m2 = lg2.max(axis=-1, keepdims=True)
    i2 = jnp.where(lg2 == m2, lane, LANES).min(axis=-1, keepdims=True)
    e2 = jnp.exp(m2 - m1)
    den = 1.0 + e2
    g1 = 1.0 / den
    g2 = e2 / den
    @pl.when(pl.program_id(0) == 0)
    def _():
        carry_ref[...] = jnp.zeros(carry_ref.shape, F32)

    tm = lg.shape[0]
    hit = jnp.where(lane == i1, 1.0, jnp.where(lane == i2, 1.0, 0.0))
    rr = lax.broadcasted_iota(jnp.int32, (tm, tm), 0)
    cc = lax.broadcasted_iota(jnp.int32, (tm, tm), 1)
    tri = jnp.where(cc < rr, 1.0, 0.0).astype(BF16)
    before = _dot(tri, hit.astype(BF16)) + carry_ref[...]
    r1 = jnp.where(lane == i1, before, 0.0).sum(axis=-1, keepdims=True)
    r2 = jnp.where(lane == i2, before, 0.0).sum(axis=-1, keepdims=True)
    carry_ref[...] = carry_ref[...] + hit.sum(axis=0, keepdims=True)
    cnt_ref[...] = carry_ref[...]
    meta = jnp.where(lane == i1, g1, 0.0) + jnp.where(lane == i2, g2, 0.0)
    for col, val in ((META_E1, i1.astype(F32)), (META_E2, i2.astype(F32)), (META_G1, g1),
                     (META_G2, g2), (META_R1, r1), (META_R2, r2)):
        meta = jnp.where(lane == col, val, meta)
    o_ref[...] = meta


META_E1, META_E2, META_G1, META_G2, META_R1, META_R2 = 8, 9, 10, 11, 12, 13


def router(x, w_pad, tm):
    m, k = x.shape
    tm = min(tm, m)
    return pl.pallas_call(
        _router_kernel,
        out_shape=[jax.ShapeDtypeStruct((m, LANES), F32), jax.ShapeDtypeStruct((1, LANES), F32)],
        grid=(m // tm,),
        in_specs=[pl.BlockSpec((tm, k), lambda i: (i, 0)),
                  pl.BlockSpec((k, LANES), lambda i: (0, 0))],
        out_specs=[pl.BlockSpec((tm, LANES), lambda i: (i, 0)),
                   pl.BlockSpec((1, LANES), lambda i: (0, 0))],
        scratch_shapes=[pltpu.VMEM((1, LANES), F32)],
        compiler_params=_cp(1),
        name="router",
    )(x, w_pad)


def _dispatch_kernel(p1_ref, p2_ref, x_hbm, o_ref, src_ref, buf_ref, sem_ref, *, m, tile, n_tiles):
    t = pl.program_id(0)

    @pl.when(t == 0)
    def _():
        def clear(i, c):
            src_ref[i] = 0
            return c
        lax.fori_loop(0, n_tiles * tile, clear, 0)

        def invert(n, c):
            src_ref[p1_ref[n]] = n
            src_ref[p2_ref[n]] = n
            return c
        lax.fori_loop(0, m, invert, 0)

    def issue(tt, slot):
        def body(i, c):
            tok = src_ref[tt * tile + i]
            pltpu.make_async_copy(x_hbm.at[pl.ds(tok, 1)], buf_ref.at[slot, pl.ds(i, 1)],
                                  sem_ref.at[slot]).start()
            return c
        lax.fori_loop(0, tile, body, 0)

    @pl.when(t == 0)
    def _():
        issue(0, 0)

    @pl.when(t + 1 < n_tiles)
    def _():
        issue(t + 1, (t + 1) % 2)

    slot = t % 2
    pltpu.make_async_copy(buf_ref.at[slot], buf_ref.at[slot], sem_ref.at[slot]).wait()
    o_ref[...] = buf_ref[slot].astype(BF16)


def moe_dispatch(pos1, pos2, x, tile, n_tiles):
    m, d = x.shape
    return pl.pallas_call(
        functools.partial(_dispatch_kernel, m=m, tile=tile, n_tiles=n_tiles),
        out_shape=jax.ShapeDtypeStruct((n_tiles * tile, d), BF16),
        grid_spec=pltpu.PrefetchScalarGridSpec(
            num_scalar_prefetch=2,
            grid=(n_tiles,),
            in_specs=[pl.BlockSpec(memory_space=pl.ANY)],
            out_specs=pl.BlockSpec((tile, d), lambda t, p1, p2: (t, 0)),
            scratch_shapes=[pltpu.SMEM((n_tiles * tile,), jnp.int32),
                            pltpu.VMEM((2, tile, d), F32),
                            pltpu.SemaphoreType.DMA((2,))]),
        compiler_params=_cp(1),
        name="moe_dispatch",
    )(pos1, pos2, x)


def _tile_ids(t, te_ref, nv_ref):
    nv = nv_ref[0]
    cur = te_ref[jnp.minimum(t, nv - 1)]
    prev = te_ref[jnp.minimum(jnp.maximum(t - 1, 0), nv - 1)]
    return nv, (t == 0) | (cur != prev)


def _ggu_kernel(te_ref, nv_ref, a_ref, wg_ref, wu_ref, o_ref, wgb_ref, wub_ref):
    t = pl.program_id(1)
    nv, new_expert = _tile_ids(t, te_ref, nv_ref)

    @pl.when(new_expert)
    def _():
        wgb_ref[...] = wg_ref[...].astype(BF16)
        wub_ref[...] = wu_ref[...].astype(BF16)

    @pl.when(t < nv)
    def _():
        a = a_ref[...]
        o_ref[...] = (_silu(_dot(a, wgb_ref[...])) * _dot(a, wub_ref[...])).astype(BF16)

    @pl.when(t >= nv)
    def _():
        o_ref[...] = jnp.zeros(o_ref.shape, o_ref.dtype)


def moe_gu_grouped(tile_expert, n_valid, a, w_gu, layer, tile, tn):
    r, k = a.shape
    nj = D_FF_EXPERT // tn
    row = lambda t, nv: jnp.minimum(t, nv[0] - 1)
    return pl.pallas_call(
        _ggu_kernel,
        out_shape=jax.ShapeDtypeStruct((r, D_FF_EXPERT), BF16),
        grid_spec=pltpu.PrefetchScalarGridSpec(
            num_scalar_prefetch=2,
            grid=(nj, r // tile),
            in_specs=[pl.BlockSpec((tile, k), lambda j, t, te, nv: (row(t, nv), 0)),
                      pl.BlockSpec((None, None, k, tn),
                                   lambda j, t, te, nv: (layer, te[row(t, nv)], 0, j)),
                      pl.BlockSpec((None, None, k, tn),
                                   lambda j, t, te, nv: (layer, te[row(t, nv)], 0, j + nj))],
            out_specs=pl.BlockSpec((tile, tn), lambda j, t, te, nv: (t, j)),
            scratch_shapes=[pltpu.VMEM((k, tn), BF16), pltpu.VMEM((k, tn), BF16)]),
        compiler_params=_cp(2),
        name="moe_gu_grouped",
    )(tile_expert, n_valid, a, w_gu, w_gu)


def _gdown_kernel(te_ref, nv_ref, a_ref, w_ref, o_ref, wb_ref):
    t = pl.program_id(1)
    nv, new_expert = _tile_ids(t, te_ref, nv_ref)

    @pl.when(new_expert)
    def _():
        wb_ref[...] = w_ref[...].astype(BF16)

    @pl.when(t < nv)
    def _():
        o_ref[...] = _dot(a_ref[...], wb_ref[...])

    @pl.when(t >= nv)
    def _():
        o_ref[...] = jnp.zeros(o_ref.shape, o_ref.dtype)


def moe_down_grouped(tile_expert, n_valid, a, w_down, layer, tile, tn):
    r, k = a.shape
    row = lambda t, nv: jnp.minimum(t, nv[0] - 1)
    return pl.pallas_call(
        _gdown_kernel,
        out_shape=jax.ShapeDtypeStruct((r, D_MODEL), F32),
        grid_spec=pltpu.PrefetchScalarGridSpec(
            num_scalar_prefetch=2,
            grid=(D_MODEL // tn, r // tile),
            in_specs=[pl.BlockSpec((tile, k), lambda j, t, te, nv: (row(t, nv), 0)),
                      pl.BlockSpec((None, None, k, tn),
                                   lambda j, t, te, nv: (layer, te[row(t, nv)], 0, j))],
            out_specs=pl.BlockSpec((tile, tn), lambda j, t, te, nv: (t, j)),
            scratch_shapes=[pltpu.VMEM((k, tn), BF16)]),
        compiler_params=_cp(2),
        name="moe_down_grouped",
    )(tile_expert, n_valid, a, w_down)


def _combine_kernel(p1_ref, p2_ref, y_hbm, meta_ref, x_ref, g_ref, b_ref, of_ref, ob_ref,
                    ybuf_ref, sem_ref, *, tm, n_tiles):
    i = pl.program_id(0)

    def issue(ii, slot):
        def body(r, c):
            n = ii * tm + r
            pltpu.make_async_copy(y_hbm.at[pl.ds(p1_ref[n], 1)], ybuf_ref.at[slot, 0, pl.ds(r, 1)],
                                  sem_ref.at[slot]).start()
            pltpu.make_async_copy(y_hbm.at[pl.ds(p2_ref[n], 1)], ybuf_ref.at[slot, 1, pl.ds(r, 1)],
                                  sem_ref.at[slot]).start()
            return c
        lax.fori_loop(0, tm, body, 0)

    @pl.when(i == 0)
    def _():
        issue(0, 0)

    @pl.when(i + 1 < n_tiles)
    def _():
        issue(i + 1, (i + 1) % 2)

    slot = i % 2
    pltpu.make_async_copy(ybuf_ref.at[slot], ybuf_ref.at[slot], sem_ref.at[slot]).wait()
    meta = meta_ref[...]
    lane = lax.broadcasted_iota(jnp.int32, meta.shape, 1)
    g1 = jnp.where(lane == META_G1, meta, 0.0).sum(axis=-1, keepdims=True)
    g2 = jnp.where(lane == META_G2, meta, 0.0).sum(axis=-1, keepdims=True)
    z = DN_ALPHA * x_ref[...] + (g1 * ybuf_ref[slot, 0] + g2 * ybuf_ref[slot, 1])
    mu = z.mean(axis=-1, keepdims=True)
    var = jnp.square(z - mu).mean(axis=-1, keepdims=True)
    o = (z - mu) * lax.rsqrt(var + LN_EPS) * g_ref[...] + b_ref[...]
    of_ref[...] = o
    ob_ref[...] = o.astype(BF16)


def moe_combine_ln(pos1, pos2, y, meta, x, g, b, tm):
    m, d = x.shape
    n_tiles = m // tm
    tok = lambda i, p1, p2: (i, 0)
    vec = lambda i, p1, p2: (0, 0)
    return pl.pallas_call(
        functools.partial(_combine_kernel, tm=tm, n_tiles=n_tiles),
        out_shape=[jax.ShapeDtypeStruct((m, d), F32), jax.ShapeDtypeStruct((m, d), BF16)],
        grid_spec=pltpu.PrefetchScalarGridSpec(
            num_scalar_prefetch=2,
            grid=(n_tiles,),
            in_specs=[pl.BlockSpec(memory_space=pl.ANY),
                      pl.BlockSpec((tm, LANES), tok),
                      pl.BlockSpec((tm, d), tok),
                      pl.BlockSpec((1, d), vec),
                      pl.BlockSpec((1, d), vec)],
            out_specs=[pl.BlockSpec((tm, d), tok), pl.BlockSpec((tm, d), tok)],
            scratch_shapes=[pltpu.VMEM((2, 2, tm, d), F32),
                            pltpu.SemaphoreType.DMA((2,))]),
        compiler_params=_cp(1),
        name="moe_combine_ln",
    )(pos1, pos2, y, meta, x, g, b)


def _sort_key(x):
    k = pltpu.bitcast(x, jnp.int32)
    return jnp.where(k < 0, k ^ jnp.int32(0x7FFFFFFF), k)


def _kth_largest_key(count_ge, n_sel, shape):
    zero = jnp.zeros(shape, jnp.int32)
    t0 = jnp.where(count_ge(zero) >= n_sel, zero, jnp.full(shape, INT_MIN, jnp.int32))

    def body(i, t):
        cand = t | jnp.left_shift(jnp.int32(1), 30 - i)
        return jnp.where(count_ge(cand) >= n_sel, cand, t)

    return lax.fori_loop(0, 31, body, t0)


def _idx_mask_kernel(iq_ref, iw_ref, ikt_ref, o_ref, key_ref, *, tq, n_sel):
    i = pl.program_id(1)
    s_len = ikt_ref.shape[1]
    wf = iw_ref[...] * (IDX_DIM ** -0.5)
    ikt = ikt_ref[...]
    sc = None
    for h in range(IDX_HEADS):
        d = _dot(iq_ref[:, h * IDX_DIM:(h + 1) * IDX_DIM], ikt)
        t = jnp.maximum(d, 0.0) * wf[:, h:h + 1]
        sc = t if sc is None else sc + t
    qpos = i * tq + lax.broadcasted_iota(jnp.int32, (tq, s_len), 0)
    kpos = lax.broadcasted_iota(jnp.int32, (tq, s_len), 1)
    valid = kpos <= qpos
    sc = jnp.where(valid, sc, NEG)
    key_ref[...] = _sort_key(sc)

    def count_ge(cand):
        return jnp.where(key_ref[...] >= cand, 1.0, 0.0).sum(axis=-1, keepdims=True)

    thr = _kth_largest_key(count_ge, float(n_sel), (tq, 1))
    sel = (key_ref[...] >= thr) & valid
    o_ref[...] = jnp.where(sel, 0.0, NEG).astype(o_ref.dtype)


def idx_mask(iq, iw, ikt, n_batch, tq):
    m = iq.shape[0]
    s_len = ikt.shape[2]
    nq = s_len // tq
    n_sel = min(TOPK_MAX, s_len // 4)
    return pl.pallas_call(
        functools.partial(_idx_mask_kernel, tq=tq, n_sel=n_sel),
        out_shape=jax.ShapeDtypeStruct((n_batch, s_len, s_len), BF16),
        grid=(n_batch, nq),
        in_specs=[pl.BlockSpec((tq, IDX_HEADS * IDX_DIM), lambda b, i: (b * nq + i, 0)),
                  pl.BlockSpec((tq, IDX_HEADS), lambda b, i: (b * nq + i, 0)),
                  pl.BlockSpec((None, IDX_DIM, s_len), lambda b, i: (b, 0, 0))],
        out_specs=pl.BlockSpec((None, tq, s_len), lambda b, i: (b, i, 0)),
        scratch_shapes=[pltpu.VMEM((tq, s_len), jnp.int32)],
        compiler_params=_cp(2),
        name="idx_mask",
    )(iq, iw, ikt)


def _rel_bucket(dist):
    max_exact = REL_BUCKETS // 2
    large = max_exact + (jnp.log(jnp.maximum(dist, 1).astype(F32) / max_exact)
                         / math.log(REL_MAX_DIST / max_exact) * (REL_BUCKETS - max_exact)).astype(jnp.int32)
    large = jnp.minimum(large, REL_BUCKETS - 1)
    return jnp.where(dist < max_exact, dist, large)


def _bias_tiles_kernel(rb_ref, o_ref, *, t):
    d = pl.program_id(0)
    ii = lax.broadcasted_iota(jnp.int32, (t, t), 0)
    jj = lax.broadcasted_iota(jnp.int32, (t, t), 1)
    bucket = _rel_bucket(jnp.maximum(d * t + ii - jj, 0))
    for h in range(N_HEADS):
        acc = jnp.zeros((t, t), F32)
        for bkt in range(REL_BUCKETS):
            acc = jnp.where(bucket == bkt, rb_ref[bkt, h], acc)
        o_ref[h] = acc


def bias_tiles(rel_bias, t):
    assert 2 * t - (t - 1) >= REL_MAX_DIST
    return pl.pallas_call(
        functools.partial(_bias_tiles_kernel, t=t),
        out_shape=jax.ShapeDtypeStruct((3, N_HEADS, t, t), F32),
        grid=(3,),
        in_specs=[pl.BlockSpec(memory_space=pltpu.SMEM)],
        out_specs=pl.BlockSpec((None, N_HEADS, t, t), lambda d: (d, 0, 0, 0)),
        compiler_params=_cp(1),
        name="bias_tiles",
    )(rel_bias)


def _attn_kernel(q_ref, k_ref, v_ref, mask_ref, tz_ref, o_ref, m_ref, l_ref, acc_ref):
    i = pl.program_id(1)
    j = pl.program_id(2)

    @pl.when(j == 0)
    def _():
        m_ref[...] = jnp.full(m_ref.shape, -jnp.inf, F32)
        l_ref[...] = jnp.zeros(l_ref.shape, F32)
        acc_ref[...] = jnp.zeros(acc_ref.shape, F32)

    @pl.when(j <= i)
    def _():
        maskf = mask_ref[...].astype(F32)
        for h in range(N_HEADS):
            sl = slice(h * HEAD_DIM, (h + 1) * HEAD_DIM)
            s = _dot_nt(q_ref[:, sl], k_ref[:, sl]) + tz_ref[h] + maskf
            m_prev = m_ref[h][:, :1]
            m_new = jnp.maximum(m_prev, s.max(axis=-1, keepdims=True))
            alpha = jnp.exp(m_prev - m_new)
            p = jnp.exp(s - m_new)
            l_new = alpha * l_ref[h][:, :1] + p.sum(axis=-1, keepdims=True)
            acc_ref[:, sl] = alpha * acc_ref[:, sl] + _dot(p.astype(BF16), v_ref[:, sl])
            m_ref[h] = jnp.broadcast_to(m_new, m_ref.shape[1:])
            l_ref[h] = jnp.broadcast_to(l_new, l_ref.shape[1:])

    @pl.when(j == i)
    def _():
        for h in range(N_HEADS):
            sl = slice(h * HEAD_DIM, (h + 1) * HEAD_DIM)
            o_ref[:, sl] = (acc_ref[:, sl] / l_ref[h][:, :1]).astype(o_ref.dtype)


def attn(q, k, v, mask, tz, n_batch, t):
    m = q.shape[0]
    nq = m // n_batch // t
    kv_idx = lambda b, i, j: (b * nq + jnp.minimum(j, i), 0)
    return pl.pallas_call(
        _attn_kernel,
        out_shape=jax.ShapeDtypeStruct((m, D_MODEL), BF16),
        grid=(n_batch, nq, nq),
        in_specs=[pl.BlockSpec((t, D_MODEL), lambda b, i, j: (b * nq + i, 0)),
                  pl.BlockSpec((t, D_MODEL), kv_idx),
                  pl.BlockSpec((t, D_MODEL), kv_idx),
                  pl.BlockSpec((None, t, t), lambda b, i, j: (b, i, jnp.minimum(j, i))),
                  pl.BlockSpec((None, N_HEADS, t, t),
                               lambda b, i, j: (jnp.clip(i - j, 0, 2), 0, 0, 0))],
        out_specs=pl.BlockSpec((t, D_MODEL), lambda b, i, j: (b * nq + i, 0)),
        scratch_shapes=[pltpu.VMEM((N_HEADS, t, LANES), F32),
                        pltpu.VMEM((N_HEADS, t, LANES), F32),
                        pltpu.VMEM((t, D_MODEL), F32)],
        compiler_params=_cp(3),
        name="attn",
    )(q, k, v, mask, tz)


def _ret_kernel(lg_ref, q_ref, k_ref, v_ref, g_ref, gn_ref, s0_ref, z_ref, sout_ref, state_ref,
                *, c_pad, c_true, nc):
    h = pl.program_id(1)
    c = pl.program_id(2)
    lg = lg_ref[h]

    @pl.when(c == 0)
    def _():
        state_ref[...] = s0_ref[...]

    q = q_ref[...]
    k = k_ref[...]
    v = v_ref[...]
    ii = lax.broadcasted_iota(jnp.int32, (c_pad, c_pad), 0)
    jj = lax.broadcasted_iota(jnp.int32, (c_pad, c_pad), 1)
    rel = (ii - jj).astype(F32)
    decay = jnp.where(rel >= 0, jnp.exp(lg * jnp.maximum(rel, 0.0)), 0.0)
    scores = _dot_nt(q, k) * decay
    inner = _dot(scores.astype(BF16), v)
    s_prev = state_ref[...]
    idx = lax.broadcasted_iota(jnp.int32, (c_pad, 1), 0).astype(F32)
    cross = _dot(q, s_prev.astype(BF16)) * jnp.exp(lg * (idx + 1.0))
    o = inner + cross
    w_state = jnp.exp(lg * (c_true - 1.0 - idx))
    kw = (k.astype(F32) * w_state).astype(BF16)
    s_new = s_prev * jnp.exp(lg * c_true) + _dot_tn(kw, v)
    state_ref[...] = s_new

    mu = o.mean(axis=-1, keepdims=True)
    var = jnp.square(o - mu).mean(axis=-1, keepdims=True)
    on = (o - mu) * lax.rsqrt(var + LN_EPS) * gn_ref[...]
    z_ref[...] = (_silu(g_ref[...]) * on).astype(BF16)

    @pl.when(c == nc - 1)
    def _():
        sout_ref[...] = s_new


def retention(lg, qk, v, g, gn, s0, n_batch, c_pad, c_true):
    m = qk.shape[0]
    nc = m // n_batch // c_pad
    row = lambda b, h, c: b * nc + c
    return pl.pallas_call(
        functools.partial(_ret_kernel, c_pad=c_pad, c_true=float(c_true), nc=nc),
        out_shape=[jax.ShapeDtypeStruct((m, RET_VW), BF16),
                   jax.ShapeDtypeStruct((n_batch, RET_HEADS, RET_DK, RET_DV), F32)],
        grid=(n_batch, RET_HEADS, nc),
        in_specs=[pl.BlockSpec(memory_space=pltpu.SMEM),
                  pl.BlockSpec((c_pad, RET_DK), lambda b, h, c: (row(b, h, c), h)),
                  pl.BlockSpec((c_pad, RET_DK), lambda b, h, c: (row(b, h, c), RET_HEADS + h)),
                  pl.BlockSpec((c_pad, RET_DV), lambda b, h, c: (row(b, h, c), h)),
                  pl.BlockSpec((c_pad, RET_DV), lambda b, h, c: (row(b, h, c), h)),
                  pl.BlockSpec((1, RET_DV), lambda b, h, c: (0, h)),
                  pl.BlockSpec((None, None, RET_DK, RET_DV), lambda b, h, c: (b, h, 0, 0))],
        out_specs=[pl.BlockSpec((c_pad, RET_DV), lambda b, h, c: (row(b, h, c), h)),
                   pl.BlockSpec((None, None, RET_DK, RET_DV), lambda b, h, c: (b, h, 0, 0))],
        scratch_shapes=[pltpu.VMEM((RET_DK, RET_DV), F32)],
        compiler_params=_cp(3),
        name="retention",
    )(lg, qk, qk, v, g, gn, s0)


DEC_CHUNK = 2048


def _dec_scores_kernel(pt_ref, ik_hbm, iq_ref, iw_ref, ikn_ref, o_ref, buf_ref, sem_ref,
                       *, layer, t_len, n_pages, n_batch):
    b = pl.program_id(0)
    past = n_pages * PAGE_SIZE

    def issue(bb, slot):
        def body(p, c):
            pltpu.make_async_copy(ik_hbm.at[layer, pt_ref[bb, p]],
                                  buf_ref.at[slot, pl.ds(p * PAGE_SIZE, PAGE_SIZE)],
                                  sem_ref.at[slot]).start()
            return c
        lax.fori_loop(0, n_pages, body, 0)

    @pl.when(b == 0)
    def _():
        issue(0, 0)

    @pl.when(b + 1 < n_batch)
    def _():
        issue(b + 1, (b + 1) % 2)

    slot = b % 2
    pltpu.make_async_copy(buf_ref.at[slot], buf_ref.at[slot], sem_ref.at[slot]).wait()
    iq = iq_ref[...]
    wf = iw_ref[...] * (IDX_DIM ** -0.5)

    def head_sum(keys_bf16):
        r = jnp.maximum(_dot_nt(iq, keys_bf16), 0.0) * wf
        return r.reshape(t_len, IDX_HEADS, r.shape[-1]).sum(axis=1)

    chunk = min(DEC_CHUNK, past)
    for c in range(past // chunk):
        sl = slice(c * chunk, (c + 1) * chunk)
        o_ref[:, sl] = head_sum(buf_ref[slot, sl, :].astype(BF16))
    sn = head_sum(ikn_ref[...])
    qi = lax.broadcasted_iota(jnp.int32, sn.shape, 0)
    si = lax.broadcasted_iota(jnp.int32, sn.shape, 1)
    o_ref[:, past:] = jnp.where(si < t_len, jnp.where(si <= qi, sn, NEG), -jnp.inf)


def dec_scores(page_table, cache_idx_k, layer, iq, iw, ik_new, t_len):
    n_batch, n_pages = page_table.shape
    past = n_pages * PAGE_SIZE
    assert past % min(DEC_CHUNK, past) == 0
    rows = t_len * IDX_HEADS
    return pl.pallas_call(
        functools.partial(_dec_scores_kernel, layer=layer, t_len=t_len, n_pages=n_pages,
                          n_batch=n_batch),
        out_shape=jax.ShapeDtypeStruct((n_batch, t_len, past + LANES), F32),
        grid_spec=pltpu.PrefetchScalarGridSpec(
            num_scalar_prefetch=1,
            grid=(n_batch,),
            in_specs=[pl.BlockSpec(memory_space=pl.ANY),
                      pl.BlockSpec((None, rows, IDX_DIM), lambda b, pt: (b, 0, 0)),
                      pl.BlockSpec((None, rows, 1), lambda b, pt: (b, 0, 0)),
                      pl.BlockSpec((None, LANES, IDX_DIM), lambda b, pt: (b, 0, 0))],
            out_specs=pl.BlockSpec((None, t_len, past + LANES), lambda b, pt: (b, 0, 0)),
            scratch_shapes=[pltpu.VMEM((2, past, IDX_DIM), F32),
                            pltpu.SemaphoreType.DMA((2,))]),
        compiler_params=_cp(1),
        name="dec_scores",
    )(page_table, cache_idx_k, iq, iw, ik_new)


def _dec_topk_kernel(s_ref, o_ref, x_ref, *, n_sel):
    x_ref[...] = s_ref[...]
    blk = lax.broadcasted_iota(jnp.int32, x_ref.shape, 0)
    lane = lax.broadcasted_iota(jnp.int32, x_ref.shape, 2)
    key_id = blk * LANES + lane
    big = jnp.int32(2 ** 30)

    def body(j, c):
        x = x_ref[...]
        m = x.max(axis=0).max(axis=-1, keepdims=True)
        pick = jnp.where(x == m[None], key_id, big).min(axis=0).min(axis=-1, keepdims=True)
        x_ref[...] = jnp.where(key_id == pick[None], -jnp.inf, x)
        o_ref[j] = pick
        return c

    lax.fori_loop(0, n_sel, body, 0)


def dec_topk(s_blocks, n_sel):
    nb, rows, _ = s_blocks.shape
    return pl.pallas_call(
        functools.partial(_dec_topk_kernel, n_sel=n_sel),
        out_shape=jax.ShapeDtypeStruct((n_sel, rows, 1), jnp.int32),
        grid=(1,),
        in_specs=[pl.BlockSpec((nb, rows, LANES), lambda i: (0, 0, 0))],
        out_specs=pl.BlockSpec((n_sel, rows, 1), lambda i: (0, 0, 0)),
        scratch_shapes=[pltpu.VMEM((nb, rows, LANES), F32)],
        compiler_params=_cp(1),
        name="dec_topk",
    )(s_blocks)


def _dec_attn_kernel(idx_ref, pt_ref, ck_hbm, cv_hbm, kn_hbm, vn_hbm, q_ref, idxv_ref, rbt_ref, o_ref,
                     kbuf_ref, vbuf_ref, sem_ref, *, layer, t_len, n_pages, n_q, n_sel):
    r = pl.program_id(0)
    past = n_pages * PAGE_SIZE

    def issue(rr, slot):
        bb = rr // t_len

        def body(j, c):
            key = idx_ref[rr * n_sel + j]

            @pl.when(key < past)
            def _():
                phys = pt_ref[bb, key // PAGE_SIZE]
                row = key % PAGE_SIZE
                pltpu.make_async_copy(ck_hbm.at[layer, phys, row], kbuf_ref.at[slot, j],
                                      sem_ref.at[0, slot]).start()
                pltpu.make_async_copy(cv_hbm.at[layer, phys, row], vbuf_ref.at[slot, j],
                                      sem_ref.at[1, slot]).start()

            @pl.when(key >= past)
            def _():
                row = bb * t_len + jnp.minimum(key - past, t_len - 1)
                pltpu.make_async_copy(kn_hbm.at[row], kbuf_ref.at[slot, j], sem_ref.at[0, slot]).start()
                pltpu.make_async_copy(vn_hbm.at[row], vbuf_ref.at[slot, j], sem_ref.at[1, slot]).start()

            return c
        lax.fori_loop(0, n_sel, body, 0)

    @pl.when(r == 0)
    def _():
        issue(0, 0)

    @pl.when(r + 1 < n_q)
    def _():
        issue(r + 1, (r + 1) % 2)

    slot = r % 2
    pltpu.make_async_copy(kbuf_ref.at[slot], kbuf_ref.at[slot], sem_ref.at[0, slot]).wait()
    pltpu.make_async_copy(vbuf_ref.at[slot], vbuf_ref.at[slot], sem_ref.at[1, slot]).wait()

    qb = q_ref[...].astype(BF16)
    dist = (past + r % t_len) - idxv_ref[...]
    bucket = _rel_bucket(jnp.maximum(dist, 0))
    s_rows = []
    for h in range(N_HEADS):
        kh = kbuf_ref[slot, :, h, :].astype(BF16)
        s_rows.append(_dot_nt(qb, kh)[h:h + 1, :])
    s = jnp.concatenate(s_rows, axis=0)
    bias = jnp.zeros(s.shape, F32)
    for bkt in range(REL_BUCKETS):
        bias = jnp.where(bucket == bkt, rbt_ref[:, bkt:bkt + 1], bias)
    s = jnp.where(dist >= 0, s + bias, NEG)
    p = jnp.exp(s - s.max(axis=-1, keepdims=True))
    p = (p / p.sum(axis=-1, keepdims=True)).astype(BF16)
    for h in range(N_HEADS):
        vh = vbuf_ref[slot, :, h, :].astype(BF16)
        o_ref[h:h + 1, :] = _dot(p, vh)[h:h + 1, :]


def dec_attn(idx_flat, page_table, idx_rows, q, cache_k, cache_v, layer, k_new, v_new, rbt, t_len, n_sel):
    n_batch, n_pages = page_table.shape
    n_q = n_batch * t_len
    slab = (N_HEADS, HEAD_DIM)
    return pl.pallas_call(
        functools.partial(_dec_attn_kernel, layer=layer, t_len=t_len, n_pages=n_pages, n_q=n_q,
                          n_sel=n_sel),
        out_shape=jax.ShapeDtypeStruct((n_q,) + slab, F32),
        grid_spec=pltpu.PrefetchScalarGridSpec(
            num_scalar_prefetch=2,
            grid=(n_q,),
            in_specs=[pl.BlockSpec(memory_space=pl.ANY),
                      pl.BlockSpec(memory_space=pl.ANY),
                      pl.BlockSpec(memory_space=pl.ANY),
                      pl.BlockSpec(memory_space=pl.ANY),
                      pl.BlockSpec((None,) + slab, lambda r, ix, pt: (r, 0, 0)),
                      pl.BlockSpec((None, 1, n_sel), lambda r, ix, pt: (r, 0, 0)),
                      pl.BlockSpec((N_HEADS, REL_BUCKETS), lambda r, ix, pt: (0, 0))],
            out_specs=pl.BlockSpec((None,) + slab, lambda r, ix, pt: (r, 0, 0)),
            scratch_shapes=[pltpu.VMEM((2, n_sel) + slab, F32),
                            pltpu.VMEM((2, n_sel) + slab, F32),
                            pltpu.SemaphoreType.DMA((2, 2))]),
        compiler_params=_cp(1),
        name="dec_attn",
    )(idx_flat, page_table, cache_k, cache_v, k_new, v_new, q, idx_rows, rbt)


ATT_T = 256
IDX_TQ = 128
DEC_PAD = 16


def _attn_in_proj(xb, w_in, layer, tm, q_dtype=BF16):
    q, = proj(xb, w_in, layer, 0, D_MODEL, 1024, tm, [q_dtype], scale=HEAD_DIM ** -0.5, name="proj_q")
    kv, kvb = proj(xb, w_in, layer, D_MODEL, 2 * D_MODEL, 1024, tm, [F32, BF16], name="proj_kv")
    iq, = proj(xb, w_in, layer, 3 * D_MODEL, IDX_HEADS * IDX_DIM, 1024, tm, [BF16], name="proj_iq")
    o4 = 3 * D_MODEL + IDX_HEADS * IDX_DIM
    w_tail = jnp.pad(w_in[layer, :, o4:], ((0, 0), (0, 2 * LANES - (IDX_DIM + IDX_HEADS))))
    tail, = proj(xb, w_tail, 0, 0, 2 * LANES, 2 * LANES, tm, [F32], name="proj_tail")
    ik = tail[:, :IDX_DIM]
    iw = tail[:, IDX_DIM:IDX_DIM + IDX_HEADS] * IDX_HEADS ** -0.5
    return q, kv, kvb, iq, ik, iw


def _attn_prompt(xb, w_in, w_out_args, layer, rel_tiles, n_batch):
    m = xb.shape[0]
    t_len = m // n_batch
    q, kv, kvb, iq, ik, iw = _attn_in_proj(xb, w_in, layer, 1024)
    ikt = jnp.swapaxes(ik.astype(BF16).reshape(n_batch, t_len, IDX_DIM), 1, 2)
    mask = idx_mask(iq, iw, ikt, n_batch, IDX_TQ)
    o = attn(q, kvb[:, :D_MODEL], kvb[:, D_MODEL:], mask, rel_tiles, n_batch, ATT_T)
    k = kv[:, :D_MODEL].reshape(n_batch, t_len, N_HEADS, HEAD_DIM)
    v = kv[:, D_MODEL:].reshape(n_batch, t_len, N_HEADS, HEAD_DIM)
    return o, k, v, ik.reshape(n_batch, t_len, IDX_DIM)


def _attn_sample(xb, w_in, layer, cache_k, cache_v, cache_idx_k, page_table, rbt, n_batch):
    m = xb.shape[0]
    t_len = m // n_batch
    q, kv, _, iq, ik, iw = _attn_in_proj(xb, w_in, layer, m, q_dtype=F32)
    iq_r = iq.reshape(n_batch, t_len * IDX_HEADS, IDX_DIM)
    iw_r = iw.reshape(n_batch, t_len * IDX_HEADS, 1)
    ik_pad = jnp.pad(ik.reshape(n_batch, t_len, IDX_DIM), ((0, 0), (0, LANES - t_len), (0, 0)))
    sc = dec_scores(page_table, cache_idx_k, layer, iq_r, iw_r, ik_pad.astype(BF16), t_len)
    n_keys = sc.shape[-1]
    n_sel = min(TOPK_MAX, (n_keys - LANES + t_len) // 4)
    s_blocks = jnp.transpose(sc.reshape(m, n_keys // LANES, LANES), (1, 0, 2))
    sel = dec_topk(s_blocks, n_sel)
    sel = jnp.swapaxes(sel[:, :, 0], 0, 1)
    slab = (m, N_HEADS, HEAD_DIM)
    k_new, v_new = kv[:, :D_MODEL], kv[:, D_MODEL:]
    o = dec_attn(sel.reshape(-1), page_table, sel[:, None, :], q.reshape(slab), cache_k, cache_v, layer,
                 k_new.reshape(slab), v_new.reshape(slab), rbt, t_len, n_sel)
    return (o.reshape(m, D_MODEL).astype(BF16), k_new.reshape(n_batch, t_len, N_HEADS, HEAD_DIM),
            v_new.reshape(n_batch, t_len, N_HEADS, HEAD_DIM), ik.reshape(n_batch, t_len, IDX_DIM))


def _rot_tables(pos):
    half = RET_DK // 2
    inv = jnp.exp(-math.log(ROPE_BASE) * jnp.arange(half, dtype=F32) / half)
    ang = pos.astype(F32)[:, None] * inv[None, :]
    return jnp.cos(ang), jnp.sin(ang)


def _retention_mixer(xb, w_in, layer, gn, s0, lg, cos, sin, n_batch, chunk, pad_to):
    m = xb.shape[0]
    t_len = m // n_batch
    tm = min(1024, m)
    qk = proj_rot(xb, w_in, layer, cos, sin, 1024, tm)
    v, = proj(xb, w_in, layer, 2 * D_MODEL, RET_VW, 1024, tm, [BF16], name="proj_rv")
    g, = proj(xb, w_in, layer, 2 * D_MODEL + RET_VW, RET_VW, 1024, tm, [F32], name="proj_rg")
    if pad_to != t_len:
        padr = lambda a: jnp.pad(a.reshape(n_batch, t_len, -1),
                                 ((0, 0), (0, pad_to - t_len), (0, 0))).reshape(n_batch * pad_to, -1)
        z, s_fin = retention(lg, padr(qk), padr(v), padr(g), gn, s0, n_batch, pad_to, chunk)
        z = z.reshape(n_batch, pad_to, RET_VW)[:, :t_len].reshape(m, RET_VW)
    else:
        z, s_fin = retention(lg, qk, v, g, gn, s0, n_batch, chunk, chunk)
    return z, s_fin


MOE_TILE = 512


def _moe_dense(xf, xb, w_router_pad, w_gu, w_down_flat, layer, g, b):
    meta, _ = router(xf, w_router_pad, 512)
    gates_t = jnp.swapaxes(meta[:, :N_EXPERTS], 0, 1)[:, :, None]
    h = moe_gu(xb, w_gu, layer, gates_t, 256, 1024)
    return down_ln(h, w_down_flat, layer, xf, g, b, 512, 512, D_FF_EXPERT)


def _moe_sparse(xf, w_router_pad, w_gu, w_down, layer, g, b):
    m = xf.shape[0]
    meta, cnt = router(xf, w_router_pad, 512)
    col = lambda c: meta[:, c].astype(jnp.int32)
    counts = cnt[0, :N_EXPERTS].astype(jnp.int32)
    n_tile = (counts + MOE_TILE - 1) // MOE_TILE
    end_tile = jnp.cumsum(n_tile)
    start_row = (end_tile - n_tile) * MOE_TILE
    pos1 = start_row[col(META_E1)] + col(META_R1)
    pos2 = start_row[col(META_E2)] + col(META_R2)
    max_tiles = (2 * m + N_EXPERTS * (MOE_TILE - 1)) // MOE_TILE
    tiles = jnp.arange(max_tiles, dtype=jnp.int32)
    tile_expert = jnp.minimum(jnp.sum(tiles[:, None] >= end_tile[None, :], axis=1), N_EXPERTS - 1)
    n_valid = end_tile[-1:].astype(jnp.int32)
    xg = moe_dispatch(pos1, pos2, xf, MOE_TILE, max_tiles)
    te = tile_expert.astype(jnp.int32)
    hg = moe_gu_grouped(te, n_valid, xg, w_gu, layer, MOE_TILE, 256)
    yg = moe_down_grouped(te, n_valid, hg, w_down, layer, MOE_TILE, 512)
    return moe_combine_ln(pos1, pos2, yg, meta, xf, g, b, 256)


def kernel(x_prompt, x_sample, cache_k, cache_v, cache_idx_k, state_ret, page_table, rel_bias,
           w_in_attn, w_out_attn, w_in_ret, ret_gn_g, w_out_ret, w_ffn_gu, w_ffn_down,
           w_router, w_exp_gu, w_exp_down, ln_g, ln_b):
    bp, tp, _ = x_prompt.shape
    bs, ts, _ = x_sample.shape
    past = page_table.shape[1] * PAGE_SIZE
    mp, ms = bp * tp, bs * ts
    xp = x_prompt.reshape(mp, D_MODEL)
    xs = x_sample.reshape(ms, D_MODEL)
    xpb, xsb = xp.astype(BF16), xs.astype(BF16)

    rel_tiles = bias_tiles(rel_bias, ATT_T)
    rbt = rel_bias.T
    w_out_attn, w_out_ret, w_ffn_down = (w.astype(BF16) for w in (w_out_attn, w_out_ret, w_ffn_down))
    lg = jnp.log1p(-jnp.exp2(-5.0 - jnp.arange(RET_HEADS, dtype=F32)))
    cos_p, sin_p = _rot_tables(jnp.tile(jnp.arange(tp, dtype=jnp.int32), bp))
    cos_s, sin_s = _rot_tables(jnp.tile(past + jnp.arange(ts, dtype=jnp.int32), bs))
    w_exp_down_flat = w_exp_down.reshape(w_exp_down.shape[0], N_EXPERTS * D_FF_EXPERT, D_MODEL)
    zero_state = jnp.zeros((bp, RET_HEADS, RET_DK, RET_DV), F32)

    kp_l, vp_l, ikp_l, sp_l = [], [], [], []
    ks_l, vs_l, iks_l, ss_l = [], [], [], []
    for i in range(DEPTH):
        j = i // 2
        g0, b0 = ln_g[i, 0][None, :], ln_b[i, 0][None, :]
        g1, b1 = ln_g[i, 1][None, :], ln_b[i, 1][None, :]
        if i % 2 == 0:
            op, kp, vp, ikp = _attn_prompt(xpb, w_in_attn, None, j, rel_tiles, bp)
            os_, k_s, v_s, iks = _attn_sample(xsb, w_in_attn, j, cache_k, cache_v, cache_idx_k,
                                              page_table, rbt, bs)
            kp_l.append(kp); vp_l.append(vp); ikp_l.append(ikp)
            ks_l.append(k_s); vs_l.append(v_s); iks_l.append(iks)
            xp, xpb = down_ln(op, w_out_attn, j, xp, g0, b0, 512, 512, D_MODEL)
            xs, xsb = down_ln(os_, w_out_attn, j, xs, g0, b0, 512, 512, D_MODEL)
        else:
            gn = ret_gn_g[j][None, :]
            zp, sp = _retention_mixer(xpb, w_in_ret, j, gn, zero_state, lg, cos_p, sin_p,
                                      bp, min(RET_CHUNK, tp), tp)
            zs, ss = _retention_mixer(xsb, w_in_ret, j, gn, state_ret[j], lg, cos_s, sin_s,
                                      bs, ts, DEC_PAD)
            sp_l.append(sp); ss_l.append(ss)
            xp, xpb = down_ln(zp, w_out_ret, j, xp, g0, b0, 512, 512, RET_VW // 2)
            xs, xsb = down_ln(zs, w_out_ret, j, xs, g0, b0, 512, 512, RET_VW // 2)
        if i % 2 == 0:
            hp = ffn_gu(xpb, w_ffn_gu, j, 512, 1024)
            hs = ffn_gu(xsb, w_ffn_gu, j, 512, 1024)
            xp, xpb = down_ln(hp, w_ffn_down, j, xp, g1, b1, 512, 512, D_FF // 2)
            xs, xsb = down_ln(hs, w_ffn_down, j, xs, g1, b1, 512, 512, D_FF // 2)
        else:
            w_r = jnp.pad(w_router[j], ((0, 0), (0, LANES - N_EXPERTS)))
            xp, xpb = _moe_sparse(xp, w_r, w_exp_gu, w_exp_down, j, g1, b1)
            xs, xsb = _moe_dense(xs, xsb, w_r, w_exp_gu, w_exp_down_flat, j, g1, b1)

    return (xp.reshape(bp, tp, D_MODEL), xs.reshape(bs, ts, D_MODEL),
            jnp.stack(kp_l), jnp.stack(vp_l), jnp.stack(ikp_l), jnp.stack(sp_l),
            jnp.stack(ks_l), jnp.stack(vs_l), jnp.stack(iks_l), jnp.stack(ss_l))
```

```python
import functools
import math

import jax
import jax.numpy as jnp
import numpy as np
from jax import lax
from jax.experimental import pallas as pl
from jax.experimental.pallas import tpu as pltpu

F32 = jnp.float32
BF16 = jnp.bfloat16

D_MODEL = 2048
PAGE_SIZE = 128
N_HEADS = 16
HEAD_DIM = 128
IDX_HEADS = 16
IDX_DIM = 128
TOPK_MAX = 256
REL_BUCKETS = 32
REL_MAX_DIST = 128
RET_HEADS = 8
RET_DK = 256
RET_DV = 512
RET_VW = RET_HEADS * RET_DV
RET_CHUNK = 128
ROPE_BASE = 10000.0
D_FF = 5632
N_EXPERTS = 8
D_FF_EXPERT = 2816
DEPTH = 4
DN_ALPHA = (2 * DEPTH) ** 0.25
LN_EPS = 1e-5
NEG = -1e30

LANES = 128
VMEM_LIMIT = 56 * 1024 * 1024
INT_MIN = -2 ** 31
NEG_KEY = int(np.float32(NEG).view(np.int32)) ^ 0x7FFFFFFF


def _cp(n_axes, vmem=VMEM_LIMIT):
    return pltpu.CompilerParams(dimension_semantics=("arbitrary",) * n_axes,
                                vmem_limit_bytes=vmem)


def _silu(x):
    return x * (1.0 / (1.0 + jnp.exp(-x)))


def _dot(a, b):
    return jnp.dot(a, b, preferred_element_type=F32)


def _dot_nt(a, b):
    return lax.dot_general(a, b, (((1,), (1,)), ((), ())), preferred_element_type=F32)


def _dot_tn(a, b):
    return lax.dot_general(a, b, (((0,), (0,)), ((), ())), preferred_element_type=F32)


def _proj_kernel(a_ref, w_ref, *rest, n_out, scale):
    out_refs = rest[:n_out]
    wb_ref = rest[n_out]

    @pl.when(pl.program_id(1) == 0)
    def _():
        wb_ref[...] = w_ref[...].astype(BF16)

    acc = _dot(a_ref[...], wb_ref[...])
    if scale != 1.0:
        acc = acc * scale
    for o in out_refs:
        o[...] = acc.astype(o.dtype)


def proj(a, w, layer, col0, ncols, tn, tm, out_dtypes, scale=1.0, name="proj"):
    m, k = a.shape
    tm = min(tm, m)
    assert col0 % tn == 0 and ncols % tn == 0 and m % tm == 0
    off = col0 // tn
    if w.ndim == 3:
        w_spec = pl.BlockSpec((None, k, tn), lambda j, i: (layer, 0, j + off))
    else:
        w_spec = pl.BlockSpec((k, tn), lambda j, i: (0, j + off))
    outs = pl.pallas_call(
        functools.partial(_proj_kernel, n_out=len(out_dtypes), scale=scale),
        out_shape=[jax.ShapeDtypeStruct((m, ncols), dt) for dt in out_dtypes],
        grid=(ncols // tn, m // tm),
        in_specs=[pl.BlockSpec((tm, k), lambda j, i: (i, 0)), w_spec],
        out_specs=[pl.BlockSpec((tm, tn), lambda j, i: (i, j)) for _ in out_dtypes],
        scratch_shapes=[pltpu.VMEM((k, tn), BF16)],
        compiler_params=_cp(2),
        name=name,
    )(a, w)
    return outs


def _proj_rot_kernel(a_ref, w_ref, cos_ref, sin_ref, o_ref, wb_ref, *, heads_per_tile, q_tiles):
    j = pl.program_id(0)

    @pl.when(pl.program_id(1) == 0)
    def _():
        wb_ref[...] = w_ref[...].astype(BF16)

    acc = _dot(a_ref[...], wb_ref[...])
    scale = jnp.where(j >= q_tiles, RET_DK ** -0.5, 1.0).astype(F32)
    c = cos_ref[...]
    s = sin_ref[...]
    half = RET_DK // 2
    for hh in range(heads_per_tile):
        x1 = acc[:, hh * RET_DK: hh * RET_DK + half]
        x2 = acc[:, hh * RET_DK + half: (hh + 1) * RET_DK]
        o_ref[:, hh * RET_DK: hh * RET_DK + half] = ((x1 * c - x2 * s) * scale).astype(BF16)
        o_ref[:, hh * RET_DK + half: (hh + 1) * RET_DK] = ((x1 * s + x2 * c) * scale).astype(BF16)


def proj_rot(a, w, layer, cos, sin, tn, tm):
    m, k = a.shape
    tm = min(tm, m)
    ncols = 2 * D_MODEL
    return pl.pallas_call(
        functools.partial(_proj_rot_kernel, heads_per_tile=tn // RET_DK, q_tiles=D_MODEL // tn),
        out_shape=jax.ShapeDtypeStruct((m, ncols), BF16),
        grid=(ncols // tn, m // tm),
        in_specs=[pl.BlockSpec((tm, k), lambda j, i: (i, 0)),
                  pl.BlockSpec((None, k, tn), lambda j, i: (layer, 0, j)),
                  pl.BlockSpec((tm, RET_DK // 2), lambda j, i: (i, 0)),
                  pl.BlockSpec((tm, RET_DK // 2), lambda j, i: (i, 0))],
        out_specs=pl.BlockSpec((tm, tn), lambda j, i: (i, j)),
        scratch_shapes=[pltpu.VMEM((k, tn), BF16)],
        compiler_params=_cp(2),
        name="proj_rot",
    )(a, w, cos, sin)


def _gu_kernel(a_ref, wg_ref, wu_ref, *rest, gated, inner_axis):
    if gated:
        gate_ref, o_ref, wgb_ref, wub_ref = rest
    else:
        o_ref, wgb_ref, wub_ref = rest

    @pl.when(pl.program_id(inner_axis) == 0)
    def _():
        wgb_ref[...] = wg_ref[...].astype(BF16)
        wub_ref[...] = wu_ref[...].astype(BF16)

    a = a_ref[...]
    h = _silu(_dot(a, wgb_ref[...])) * _dot(a, wub_ref[...])
    if gated:
        h = h * gate_ref[...]
    o_ref[...] = h.astype(BF16)


def ffn_gu(a, w_gu, layer, tn, tm):
    m, k = a.shape
    tm = min(tm, m)
    nj = D_FF // tn
    return pl.pallas_call(
        functools.partial(_gu_kernel, gated=False, inner_axis=1),
        out_shape=jax.ShapeDtypeStruct((m, D_FF), BF16),
        grid=(nj, m // tm),
        in_specs=[pl.BlockSpec((tm, k), lambda j, i: (i, 0)),
                  pl.BlockSpec((None, k, tn), lambda j, i: (layer, 0, j)),
                  pl.BlockSpec((None, k, tn), lambda j, i: (layer, 0, j + nj))],
        out_specs=pl.BlockSpec((tm, tn), lambda j, i: (i, j)),
        scratch_shapes=[pltpu.VMEM((k, tn), BF16), pltpu.VMEM((k, tn), BF16)],
        compiler_params=_cp(2),
        name="ffn_gu",
    )(a, w_gu, w_gu)


def moe_gu(a, w_gu, layer, gates_t, tn, tm):
    m, k = a.shape
    tm = min(tm, m)
    nj = D_FF_EXPERT // tn
    return pl.pallas_call(
        functools.partial(_gu_kernel, gated=True, inner_axis=2),
        out_shape=jax.ShapeDtypeStruct((m, N_EXPERTS * D_FF_EXPERT), BF16),
        grid=(N_EXPERTS, nj, m // tm),
        in_specs=[pl.BlockSpec((tm, k), lambda e, j, i: (i, 0)),
                  pl.BlockSpec((None, None, k, tn), lambda e, j, i: (layer, e, 0, j)),
                  pl.BlockSpec((None, None, k, tn), lambda e, j, i: (layer, e, 0, j + nj)),
                  pl.BlockSpec((None, tm, 1), lambda e, j, i: (e, i, 0))],
        out_specs=pl.BlockSpec((tm, tn), lambda e, j, i: (i, e * nj + j)),
        scratch_shapes=[pltpu.VMEM((k, tn), BF16), pltpu.VMEM((k, tn), BF16)],
        compiler_params=_cp(3),
        name="moe_gu",
    )(a, w_gu, w_gu, gates_t)


def _down_ln_kernel(a_ref, w_ref, x_ref, g_ref, b_ref, of_ref, ob_ref, row_ref, *, nj, nk, tn):
    k = pl.program_id(1)
    j = pl.program_id(2)
    y = _dot(a_ref[...], w_ref[...].astype(BF16))

    @pl.when(k == 0)
    def _():
        row_ref[j] = y

    @pl.when(k > 0)
    def _():
        row_ref[j] = row_ref[j] + y

    @pl.when((j == nj - 1) & (k == nk - 1))
    def _():
        zs = [DN_ALPHA * x_ref[:, jj * tn:(jj + 1) * tn] + row_ref[jj] for jj in range(nj)]
        tot = zs[0].sum(axis=-1, keepdims=True)
        for z in zs[1:]:
            tot = tot + z.sum(axis=-1, keepdims=True)
        mu = tot * (1.0 / D_MODEL)
        sq = None
        for z in zs:
            t = jnp.square(z - mu).sum(axis=-1, keepdims=True)
            sq = t if sq is None else sq + t
        rstd = lax.rsqrt(sq * (1.0 / D_MODEL) + LN_EPS)
        for jj, z in enumerate(zs):
            sl = slice(jj * tn, (jj + 1) * tn)
            o = (z - mu) * rstd * g_ref[:, sl] + b_ref[:, sl]
            of_ref[:, sl] = o
            ob_ref[:, sl] = o.astype(BF16)


def down_ln(a, w, layer, x, g, b, tm, tn, tk):
    m, kk = a.shape
    tm = min(tm, m)
    n = D_MODEL
    assert kk % tk == 0 and n % tn == 0 and m % tm == 0
    nj, nk = n // tn, kk // tk
    return pl.pallas_call(
        functools.partial(_down_ln_kernel, nj=nj, nk=nk, tn=tn),
        out_shape=[jax.ShapeDtypeStruct((m, n), F32), jax.ShapeDtypeStruct((m, n), BF16)],
        grid=(m // tm, nk, nj),
        in_specs=[pl.BlockSpec((tm, tk), lambda i, k, j: (i, k)),
                  pl.BlockSpec((None, tk, tn), lambda i, k, j: (layer, k, j)),
                  pl.BlockSpec((tm, n), lambda i, k, j: (i, 0)),
                  pl.BlockSpec((1, n), lambda i, k, j: (0, 0)),
                  pl.BlockSpec((1, n), lambda i, k, j: (0, 0))],
        out_specs=[pl.BlockSpec((tm, n), lambda i, k, j: (i, 0)),
                   pl.BlockSpec((tm, n), lambda i, k, j: (i, 0))],
        scratch_shapes=[pltpu.VMEM((nj, tm, tn), F32)],
        compiler_params=_cp(3),
        name="down_ln",
    )(a, w, x, g, b)


def _router_kernel(x_ref, w_ref, o_ref, cnt_ref, carry_ref):
    x = x_ref[...]
    w = w_ref[...]
    xh = x.astype(BF16)
    xl = (x - xh.astype(F32)).astype(BF16)
    wh = w.astype(BF16)
    wl = (w - wh.astype(F32)).astype(BF16)
    lg = _dot(xh, wh) + (_dot(xh, wl) + _dot(xl, wh))
    lane = lax.broadcasted_iota(jnp.int32, lg.shape, 1)
    lg = jnp.where(lane < N_EXPERTS, lg, -jnp.inf)
    m1 = lg.max(axis=-1, keepdims=True)
    i1 = jnp.where(lg == m1, lane, LANES).min(axis=-1, keepdims=True)
    lg2 = jnp.where(lane == i1, -jnp.inf, lg)
    m2 = lg2.max(axis=-1, keepdims=True)
    i2 = jnp.where(lg2 == m2, lane, LANES).min(axis=-1, keepdims=True)
    e2 = jnp.exp(m2 - m1)
    den = 1.0 + e2
    g1 = 1.0 / den
    g2 = e2 / den
    @pl.when(pl.program_id(0) == 0)
    def _():
        carry_ref[...] = jnp.zeros(carry_ref.shape, F32)

    tm = lg.shape[0]
    hit = jnp.where(lane == i1, 1.0, jnp.where(lane == i2, 1.0, 0.0))
    rr = lax.broadcasted_iota(jnp.int32, (tm, tm), 0)
    cc = lax.broadcasted_iota(jnp.int32, (tm, tm), 1)
    tri = jnp.where(cc < rr, 1.0, 0.0).astype(BF16)
    before = _dot(tri, hit.astype(BF16)) + carry_ref[...]
    r1 = jnp.where(lane == i1, before, 0.0).sum(axis=-1, keepdims=True)
    r2 = jnp.where(lane == i2, before, 0.0).sum(axis=-1, keepdims=True)
    carry_ref[...] = carry_ref[...] + hit.sum(axis=0, keepdims=True)
    cnt_ref[...] = carry_ref[...]
    meta = jnp.where(lane == i1, g1, 0.0) + jnp.where(lane == i2, g2, 0.0)
    for col, val in ((META_E1, i1.astype(F32)), (META_E2, i2.astype(F32)), (META_G1, g1),
                     (META_G2, g2), (META_R1, r1), (META_R2, r2)):
        meta = jnp.where(lane == col, val, meta)
    o_ref[...] = meta


META_E1, META_E2, META_G1, META_G2, META_R1, META_R2 = 8, 9, 10, 11, 12, 13


def router(x, w_pad, tm):
    m, k = x.shape
    tm = min(tm, m)
    return pl.pallas_call(
        _router_kernel,
        out_shape=[jax.ShapeDtypeStruct((m, LANES), F32), jax.ShapeDtypeStruct((1, LANES), F32)],
        grid=(m // tm,),
        in_specs=[pl.BlockSpec((tm, k), lambda i: (i, 0)),
                  pl.BlockSpec((k, LANES), lambda i: (0, 0))],
        out_specs=[pl.BlockSpec((tm, LANES), lambda i: (i, 0)),
                   pl.BlockSpec((1, LANES), lambda i: (0, 0))],
        scratch_shapes=[pltpu.VMEM((1, LANES), F32)],
        compiler_params=_cp(1),
        name="router",
    )(x, w_pad)


def _dispatch_kernel(p1_ref, p2_ref, lo_ref, hi_ref, nv_ref, x_hbm, o_ref, src_ref, buf_ref, sem_ref,
                     *, m, tile, n_tiles):
    t = pl.program_id(0)
    nv = nv_ref[0]

    @pl.when(t == 0)
    def _():
        for e in range(N_EXPERTS):
            def clear(i, c):
                src_ref[i] = 0
                return c
            lax.fori_loop(lo_ref[e], hi_ref[e], clear, 0)

        def invert(n, c):
            src_ref[p1_ref[n]] = n
            src_ref[p2_ref[n]] = n
            return c
        lax.fori_loop(0, m, invert, 0, unroll=8)

    def issue(tt, slot):
        def body(i, c):
            tok = src_ref[tt * tile + i]
            pltpu.make_async_copy(x_hbm.at[pl.ds(tok, 1)], buf_ref.at[slot, pl.ds(i, 1)],
                                  sem_ref.at[slot]).start()
            return c
        lax.fori_loop(0, tile, body, 0, unroll=8)

    @pl.when(t == 0)
    def _():
        issue(0, 0)

    @pl.when(t + 1 < nv)
    def _():
        issue(t + 1, (t + 1) % 2)

    slot = t % 2

    @pl.when(t < nv)
    def _():
        pltpu.make_async_copy(buf_ref.at[slot], buf_ref.at[slot], sem_ref.at[slot]).wait()
        o_ref[...] = buf_ref[slot].astype(BF16)

    @pl.when(t >= nv)
    def _():
        o_ref[...] = jnp.zeros(o_ref.shape, o_ref.dtype)


def moe_dispatch(pos1, pos2, pad_lo, pad_hi, n_valid, x, tile, n_tiles):
    m, d = x.shape
    return pl.pallas_call(
        functools.partial(_dispatch_kernel, m=m, tile=tile, n_tiles=n_tiles),
        out_shape=jax.ShapeDtypeStruct((n_tiles * tile, d), BF16),
        grid_spec=pltpu.PrefetchScalarGridSpec(
            num_scalar_prefetch=5,
            grid=(n_tiles,),
            in_specs=[pl.BlockSpec(memory_space=pl.ANY)],
            out_specs=pl.BlockSpec((tile, d), lambda t, *_: (t, 0)),
            scratch_shapes=[pltpu.SMEM((n_tiles * tile,), jnp.int32),
                            pltpu.VMEM((2, tile, d), F32),
                            pltpu.SemaphoreType.DMA((2,))]),
        compiler_params=_cp(1),
        name="moe_dispatch",
    )(pos1, pos2, pad_lo, pad_hi, n_valid, x)


def _tile_ids(t, te_ref, nv_ref):
    nv = nv_ref[0]
    cur = te_ref[jnp.minimum(t, nv - 1)]
    prev = te_ref[jnp.minimum(jnp.maximum(t - 1, 0), nv - 1)]
    return nv, (t == 0) | (cur != prev)


def _ggu_kernel(te_ref, nv_ref, a_ref, wg_ref, wu_ref, o_ref, wgb_ref, wub_ref):
    t = pl.program_id(1)
    nv, new_expert = _tile_ids(t, te_ref, nv_ref)

    @pl.when(new_expert)
    def _():
        wgb_ref[...] = wg_ref[...].astype(BF16)
        wub_ref[...] = wu_ref[...].astype(BF16)

    @pl.when(t < nv)
    def _():
        a = a_ref[...]
        o_ref[...] = (_silu(_dot(a, wgb_ref[...])) * _dot(a, wub_ref[...])).astype(BF16)

    @pl.when(t >= nv)
    def _():
        o_ref[...] = jnp.zeros(o_ref.shape, o_ref.dtype)


def moe_gu_grouped(tile_expert, n_valid, a, w_gu, layer, tile, tn):
    r, k = a.shape
    nj = D_FF_EXPERT // tn
    row = lambda t, nv: jnp.minimum(t, nv[0] - 1)
    return pl.pallas_call(
        _ggu_kernel,
        out_shape=jax.ShapeDtypeStruct((r, D_FF_EXPERT), BF16),
        grid_spec=pltpu.PrefetchScalarGridSpec(
            num_scalar_prefetch=2,
            grid=(nj, r // tile),
            in_specs=[pl.BlockSpec((tile, k), lambda j, t, te, nv: (row(t, nv), 0)),
                      pl.BlockSpec((None, None, k, tn),
                                   lambda j, t, te, nv: (layer, te[row(t, nv)], 0, j),
                                   pipeline_mode=pl.Buffered(1)),
                      pl.BlockSpec((None, None, k, tn),
                                   lambda j, t, te, nv: (layer, te[row(t, nv)], 0, j + nj),
                                   pipeline_mode=pl.Buffered(1))],
            out_specs=pl.BlockSpec((tile, tn), lambda j, t, te, nv: (t, j)),
            scratch_shapes=[pltpu.VMEM((k, tn), BF16), pltpu.VMEM((k, tn), BF16)]),
        compiler_params=_cp(2),
        name="moe_gu_grouped",
    )(tile_expert, n_valid, a, w_gu, w_gu)


def _gdown_kernel(te_ref, nv_ref, a_ref, w_ref, o_ref, wb_ref):
    t = pl.program_id(1)
    nv, new_expert = _tile_ids(t, te_ref, nv_ref)

    @pl.when(new_expert)
    def _():
        wb_ref[...] = w_ref[...].astype(BF16)

    @pl.when(t < nv)
    def _():
        o_ref[...] = _dot(a_ref[...], wb_ref[...])

    @pl.when(t >= nv)
    def _():
        o_ref[...] = jnp.zeros(o_ref.shape, o_ref.dtype)


def moe_down_grouped(tile_expert, n_valid, a, w_down, layer, tile, tn):
    r, k = a.shape
    row = lambda t, nv: jnp.minimum(t, nv[0] - 1)
    return pl.pallas_call(
        _gdown_kernel,
        out_shape=jax.ShapeDtypeStruct((r, D_MODEL), F32),
        grid_spec=pltpu.PrefetchScalarGridSpec(
            num_scalar_prefetch=2,
            grid=(D_MODEL // tn, r // tile),
            in_specs=[pl.BlockSpec((tile, k), lambda j, t, te, nv: (row(t, nv), 0)),
                      pl.BlockSpec((None, None, k, tn),
                                   lambda j, t, te, nv: (layer, te[row(t, nv)], 0, j))],
            out_specs=pl.BlockSpec((tile, tn), lambda j, t, te, nv: (t, j)),
            scratch_shapes=[pltpu.VMEM((k, tn), BF16)]),
        compiler_params=_cp(2),
        name="moe_down_grouped",
    )(tile_expert, n_valid, a, w_down)


def _combine_kernel(p1_ref, p2_ref, y_hbm, meta_ref, x_ref, g_ref, b_ref, of_ref, ob_ref,
                    ybuf_ref, sem_ref, *, tm, n_tiles):
    i = pl.program_id(0)

    def issue(ii, slot):
        def body(r, c):
            n = ii * tm + r
            pltpu.make_async_copy(y_hbm.at[pl.ds(p1_ref[n], 1)], ybuf_ref.at[slot, 0, pl.ds(r, 1)],
                                  sem_ref.at[slot]).start()
            pltpu.make_async_copy(y_hbm.at[pl.ds(p2_ref[n], 1)], ybuf_ref.at[slot, 1, pl.ds(r, 1)],
                                  sem_ref.at[slot]).start()
            return c
        lax.fori_loop(0, tm, body, 0)

    @pl.when(i == 0)
    def _():
        issue(0, 0)

    @pl.when(i + 1 < n_tiles)
    def _():
        issue(i + 1, (i + 1) % 2)

    slot = i % 2
    pltpu.make_async_copy(ybuf_ref.at[slot], ybuf_ref.at[slot], sem_ref.at[slot]).wait()
    meta = meta_ref[...]
    lane = lax.broadcasted_iota(jnp.int32, meta.shape, 1)
    g1 = jnp.where(lane == META_G1, meta, 0.0).sum(axis=-1, keepdims=True)
    g2 = jnp.where(lane == META_G2, meta, 0.0).sum(axis=-1, keepdims=True)
    z = DN_ALPHA * x_ref[...] + (g1 * ybuf_ref[slot, 0] + g2 * ybuf_ref[slot, 1])
    mu = z.mean(axis=-1, keepdims=True)
    var = jnp.square(z - mu).mean(axis=-1, keepdims=True)
    o = (z - mu) * lax.rsqrt(var + LN_EPS) * g_ref[...] + b_ref[...]
    of_ref[...] = o
    ob_ref[...] = o.astype(BF16)


def moe_combine_ln(pos1, pos2, y, meta, x, g, b, tm):
    m, d = x.shape
    n_tiles = m // tm
    tok = lambda i, p1, p2: (i, 0)
    vec = lambda i, p1, p2: (0, 0)
    return pl.pallas_call(
        functools.partial(_combine_kernel, tm=tm, n_tiles=n_tiles),
        out_shape=[jax.ShapeDtypeStruct((m, d), F32), jax.ShapeDtypeStruct((m, d), BF16)],
        grid_spec=pltpu.PrefetchScalarGridSpec(
            num_scalar_prefetch=2,
            grid=(n_tiles,),
            in_specs=[pl.BlockSpec(memory_space=pl.ANY),
                      pl.BlockSpec((tm, LANES), tok),
                      pl.BlockSpec((tm, d), tok),
                      pl.BlockSpec((1, d), vec),
                      pl.BlockSpec((1, d), vec)],
            out_specs=[pl.BlockSpec((tm, d), tok), pl.BlockSpec((tm, d), tok)],
            scratch_shapes=[pltpu.VMEM((2, 2, tm, d), F32),
                            pltpu.SemaphoreType.DMA((2,))]),
        compiler_params=_cp(1),
        name="moe_combine_ln",
    )(pos1, pos2, y, meta, x, g, b)


def _sort_key(x):
    k = pltpu.bitcast(x, jnp.int32)
    return jnp.where(k < 0, k ^ jnp.int32(0x7FFFFFFF), k)


def _kth_largest_key(count_ge, n_sel, shape):
    zero = jnp.zeros(shape, jnp.int32)
    t0 = jnp.where(count_ge(zero) >= n_sel, zero, jnp.full(shape, INT_MIN, jnp.int32))

    def body(i, t):
        cand = t | jnp.left_shift(jnp.int32(1), 30 - i)
        return jnp.where(count_ge(cand) >= n_sel, cand, t)

    return lax.fori_loop(0, 31, body, t0)


IDX_GROUPS = 4


def _idx_mask_kernel(iq_ref, iw_ref, ikt_ref, o_ref, key_ref, *, tq, n_sel):
    i = pl.program_id(1)
    s_len = ikt_ref.shape[1]
    per = s_len // tq // IDX_GROUPS
    wf = iw_ref[...] * (IDX_DIM ** -0.5)

    def run(s_eff):
        ikt = ikt_ref[:, :s_eff]
        sc = None
        for h in range(IDX_HEADS):
            d = _dot(iq_ref[:, h * IDX_DIM:(h + 1) * IDX_DIM], ikt)
            t = jnp.maximum(d, 0.0) * wf[:, h:h + 1]
            sc = t if sc is None else sc + t
        qpos = i * tq + lax.broadcasted_iota(jnp.int32, (tq, s_eff), 0)
        kpos = lax.broadcasted_iota(jnp.int32, (tq, s_eff), 1)
        valid = kpos <= qpos
        key_ref[:, :s_eff] = _sort_key(jnp.where(valid, sc, NEG))
        n_tail = float(s_len - s_eff)

        def count_ge(cand):
            c = jnp.where(key_ref[:, :s_eff] >= cand, 1.0, 0.0).sum(axis=-1, keepdims=True)
            return c + jnp.where(cand <= NEG_KEY, n_tail, 0.0)

        thr = _kth_largest_key(count_ge, float(n_sel), (tq, 1))
        sel = (key_ref[:, :s_eff] >= thr) & valid
        o_ref[:, :s_eff] = jnp.where(sel, 0.0, NEG).astype(o_ref.dtype)
        if s_eff < s_len:
            o_ref[:, s_eff:] = jnp.full((tq, s_len - s_eff), NEG, o_ref.dtype)

    for g in range(IDX_GROUPS):
        pl.when(i // per == g)(functools.partial(run, (g + 1) * per * tq))


def idx_mask(iq, iw, ikt, n_batch, tq):
    m = iq.shape[0]
    s_len = ikt.shape[2]
    nq = s_len // tq
    assert nq % IDX_GROUPS == 0
    n_sel = min(TOPK_MAX, s_len // 4)
    return pl.pallas_call(
        functools.partial(_idx_mask_kernel, tq=tq, n_sel=n_sel),
        out_shape=jax.ShapeDtypeStruct((n_batch, s_len, s_len), BF16),
        grid=(n_batch, nq),
        in_specs=[pl.BlockSpec((tq, IDX_HEADS * IDX_DIM), lambda b, i: (b * nq + i, 0)),
                  pl.BlockSpec((tq, IDX_HEADS), lambda b, i: (b * nq + i, 0)),
                  pl.BlockSpec((None, IDX_DIM, s_len), lambda b, i: (b, 0, 0))],
        out_specs=pl.BlockSpec((None, tq, s_len), lambda b, i: (b, i, 0)),
        scratch_shapes=[pltpu.VMEM((tq, s_len), jnp.int32)],
        compiler_params=_cp(2),
        name="idx_mask",
    )(iq, iw, ikt)


def _rel_bucket(dist):
    max_exact = REL_BUCKETS // 2
    large = max_exact + (jnp.log(jnp.maximum(dist, 1).astype(F32) / max_exact)
                         / math.log(REL_MAX_DIST / max_exact) * (REL_BUCKETS - max_exact)).astype(jnp.int32)
    large = jnp.minimum(large, REL_BUCKETS - 1)
    return jnp.where(dist < max_exact, dist, large)


def _bias_tiles_kernel(rb_ref, o_ref, *, t):
    d = pl.program_id(0)
    ii = lax.broadcasted_iota(jnp.int32, (t, t), 0)
    jj = lax.broadcasted_iota(jnp.int32, (t, t), 1)
    bucket = _rel_bucket(jnp.maximum(d * t + ii - jj, 0))
    for h in range(N_HEADS):
        acc = jnp.zeros((t, t), F32)
        for bkt in range(REL_BUCKETS):
            acc = jnp.where(bucket == bkt, rb_ref[bkt, h], acc)
        o_ref[h] = acc


def bias_tiles(rel_bias, t):
    assert 2 * t - (t - 1) >= REL_MAX_DIST
    return pl.pallas_call(
        functools.partial(_bias_tiles_kernel, t=t),
        out_shape=jax.ShapeDtypeStruct((3, N_HEADS, t, t), F32),
        grid=(3,),
        in_specs=[pl.BlockSpec(memory_space=pltpu.SMEM)],
        out_specs=pl.BlockSpec((None, N_HEADS, t, t), lambda d: (d, 0, 0, 0)),
        compiler_params=_cp(1),
        name="bias_tiles",
    )(rel_bias)


def _attn_kernel(q_ref, k_ref, v_ref, mask_ref, tz_ref, o_ref, m_ref, l_ref, acc_ref):
    i = pl.program_id(1)
    j = pl.program_id(2)

    @pl.when(j == 0)
    def _():
        m_ref[...] = jnp.full(m_ref.shape, -jnp.inf, F32)
        l_ref[...] = jnp.zeros(l_ref.shape, F32)
        acc_ref[...] = jnp.zeros(acc_ref.shape, F32)

    @pl.when(j <= i)
    def _():
        maskf = mask_ref[...].astype(F32)
        for h in range(N_HEADS):
            sl = slice(h * HEAD_DIM, (h + 1) * HEAD_DIM)
            s = _dot_nt(q_ref[:, sl], k_ref[:, sl]) + tz_ref[h] + maskf
            m_prev = m_ref[h][:, :1]
            m_new = jnp.maximum(m_prev, s.max(axis=-1, keepdims=True))
            alpha = jnp.exp(m_prev - m_new)
            p = jnp.exp(s - m_new)
            l_new = alpha * l_ref[h][:, :1] + p.sum(axis=-1, keepdims=True)
            acc_ref[:, sl] = alpha * acc_ref[:, sl] + _dot(p.astype(BF16), v_ref[:, sl])
            m_ref[h] = jnp.broadcast_to(m_new, m_ref.shape[1:])
            l_ref[h] = jnp.broadcast_to(l_new, l_ref.shape[1:])

    @pl.when(j == i)
    def _():
        for h in range(N_HEADS):
            sl = slice(h * HEAD_DIM, (h + 1) * HEAD_DIM)
            o_ref[:, sl] = (acc_ref[:, sl] / l_ref[h][:, :1]).astype(o_ref.dtype)


def attn(q, k, v, mask, tz, n_batch, t):
    m = q.shape[0]
    nq = m // n_batch // t
    kv_idx = lambda b, i, j: (b * nq + jnp.minimum(j, i), 0)
    return pl.pallas_call(
        _attn_kernel,
        out_shape=jax.ShapeDtypeStruct((m, D_MODEL), BF16),
        grid=(n_batch, nq, nq),
        in_specs=[pl.BlockSpec((t, D_MODEL), lambda b, i, j: (b * nq + i, 0)),
                  pl.BlockSpec((t, D_MODEL), kv_idx),
                  pl.BlockSpec((t, D_MODEL), kv_idx),
                  pl.BlockSpec((None, t, t), lambda b, i, j: (b, i, jnp.minimum(j, i))),
                  pl.BlockSpec((None, N_HEADS, t, t),
                               lambda b, i, j: (jnp.clip(i - j, 0, 2), 0, 0, 0))],
        out_specs=pl.BlockSpec((t, D_MODEL), lambda b, i, j: (b * nq + i, 0)),
        scratch_shapes=[pltpu.VMEM((N_HEADS, t, LANES), F32),
                        pltpu.VMEM((N_HEADS, t, LANES), F32),
                        pltpu.VMEM((t, D_MODEL), F32)],
        compiler_params=_cp(3),
        name="attn",
    )(q, k, v, mask, tz)


def _ret_kernel(lg_ref, q_ref, k_ref, v_ref, g_ref, gn_ref, s0_ref, z_ref, sout_ref, state_ref,
                *, c_pad, c_true, nc):
    c = pl.program_id(1)

    @pl.when(c == 0)
    def _():
        state_ref[...] = s0_ref[...]

    ii = lax.broadcasted_iota(jnp.int32, (c_pad, c_pad), 0)
    jj = lax.broadcasted_iota(jnp.int32, (c_pad, c_pad), 1)
    rel = (ii - jj).astype(F32)
    idx = lax.broadcasted_iota(jnp.int32, (c_pad, 1), 0).astype(F32)
    for h in range(RET_HEADS):
        lg = lg_ref[h]
        qs = slice(h * RET_DK, (h + 1) * RET_DK)
        vs = slice(h * RET_DV, (h + 1) * RET_DV)
        q = q_ref[:, qs]
        k = k_ref[:, qs]
        v = v_ref[:, vs]
        decay = jnp.where(rel >= 0, jnp.exp(lg * jnp.maximum(rel, 0.0)), 0.0)
        scores = _dot_nt(q, k) * decay
        inner = _dot(scores.astype(BF16), v)
        s_prev = state_ref[h]
        cross = _dot(q, s_prev.astype(BF16)) * jnp.exp(lg * (idx + 1.0))
        o = inner + cross
        w_state = jnp.exp(lg * (c_true - 1.0 - idx))
        kw = (k.astype(F32) * w_state).astype(BF16)
        state_ref[h] = s_prev * jnp.exp(lg * c_true) + _dot_tn(kw, v)

        mu = o.mean(axis=-1, keepdims=True)
        var = jnp.square(o - mu).mean(axis=-1, keepdims=True)
        on = (o - mu) * lax.rsqrt(var + LN_EPS) * gn_ref[:, vs]
        z_ref[:, vs] = (_silu(g_ref[:, vs]) * on).astype(BF16)

    @pl.when(c == nc - 1)
    def _():
        sout_ref[...] = state_ref[...]


def retention(lg, qk, v, g, gn, s0, n_batch, c_pad, c_true):
    m = qk.shape[0]
    nc = m // n_batch // c_pad
    row = lambda b, c: b * nc + c
    state = (None, RET_HEADS, RET_DK, RET_DV)
    return pl.pallas_call(
        functools.partial(_ret_kernel, c_pad=c_pad, c_true=float(c_true), nc=nc),
        out_shape=[jax.ShapeDtypeStruct((m, RET_VW), BF16),
                   jax.ShapeDtypeStruct((n_batch, RET_HEADS, RET_DK, RET_DV), F32)],
        grid=(n_batch, nc),
        in_specs=[pl.BlockSpec(memory_space=pltpu.SMEM),
                  pl.BlockSpec((c_pad, D_MODEL), lambda b, c: (row(b, c), 0)),
                  pl.BlockSpec((c_pad, D_MODEL), lambda b, c: (row(b, c), 1)),
                  pl.BlockSpec((c_pad, RET_VW), lambda b, c: (row(b, c), 0)),
                  pl.BlockSpec((c_pad, RET_VW), lambda b, c: (row(b, c), 0)),
                  pl.BlockSpec((1, RET_VW), lambda b, c: (0, 0)),
                  pl.BlockSpec(state, lambda b, c: (b, 0, 0, 0))],
        out_specs=[pl.BlockSpec((c_pad, RET_VW), lambda b, c: (row(b, c), 0)),
                   pl.BlockSpec(state, lambda b, c: (b, 0, 0, 0))],
        scratch_shapes=[pltpu.VMEM((RET_HEADS, RET_DK, RET_DV), F32)],
        compiler_params=_cp(2),
        name="retention",
    )(lg, qk, qk, v, g, gn, s0)


DEC_CHUNK = 2048


def _dec_scores_kernel(pt_ref, ik_hbm, iq_ref, iw_ref, ikn_ref, o_ref, buf_ref, sem_ref,
                       *, layer, t_len, n_pages, n_batch):
    b = pl.program_id(0)
    past = n_pages * PAGE_SIZE

    def issue(bb, slot):
        def body(p, c):
            pltpu.make_async_copy(ik_hbm.at[layer, pt_ref[bb, p]],
                                  buf_ref.at[slot, pl.ds(p * PAGE_SIZE, PAGE_SIZE)],
                                  sem_ref.at[slot]).start()
            return c
        lax.fori_loop(0, n_pages, body, 0)

    @pl.when(b == 0)
    def _():
        issue(0, 0)

    @pl.when(b + 1 < n_batch)
    def _():
        issue(b + 1, (b + 1) % 2)

    slot = b % 2
    pltpu.make_async_copy(buf_ref.at[slot], buf_ref.at[slot], sem_ref.at[slot]).wait()
    iq = iq_ref[...]
    wf = iw_ref[...] * (IDX_DIM ** -0.5)

    def head_sum(keys_bf16):
        r = jnp.maximum(_dot_nt(iq, keys_bf16), 0.0) * wf
        return r.reshape(t_len, IDX_HEADS, r.shape[-1]).sum(axis=1)

    chunk = min(DEC_CHUNK, past)
    for c in range(past // chunk):
        sl = slice(c * chunk, (c + 1) * chunk)
        o_ref[:, sl] = head_sum(buf_ref[slot, sl, :].astype(BF16))
    sn = head_sum(ikn_ref[...])
    qi = lax.broadcasted_iota(jnp.int32, sn.shape, 0)
    si = lax.broadcasted_iota(jnp.int32, sn.shape, 1)
    o_ref[:, past:] = jnp.where(si < t_len, jnp.where(si <= qi, sn, NEG), -jnp.inf)


def dec_scores(page_table, cache_idx_k, layer, iq, iw, ik_new, t_len):
    n_batch, n_pages = page_table.shape
    past = n_pages * PAGE_SIZE
    assert past % min(DEC_CHUNK, past) == 0
    rows = t_len * IDX_HEADS
    return pl.pallas_call(
        functools.partial(_dec_scores_kernel, layer=layer, t_len=t_len, n_pages=n_pages,
                          n_batch=n_batch),
        out_shape=jax.ShapeDtypeStruct((n_batch, t_len, past + LANES), F32),
        grid_spec=pltpu.PrefetchScalarGridSpec(
            num_scalar_prefetch=1,
            grid=(n_batch,),
            in_specs=[pl.BlockSpec(memory_space=pl.ANY),
                      pl.BlockSpec((None, rows, IDX_DIM), lambda b, pt: (b, 0, 0)),
                      pl.BlockSpec((None, rows, 1), lambda b, pt: (b, 0, 0)),
                      pl.BlockSpec((None, LANES, IDX_DIM), lambda b, pt: (b, 0, 0))],
            out_specs=pl.BlockSpec((None, t_len, past + LANES), lambda b, pt: (b, 0, 0)),
            scratch_shapes=[pltpu.VMEM((2, past, IDX_DIM), F32),
                            pltpu.SemaphoreType.DMA((2,))]),
        compiler_params=_cp(1),
        name="dec_scores",
    )(page_table, cache_idx_k, iq, iw, ik_new)


def _dec_topk_kernel(s_ref, o_ref, x_ref, *, n_sel):
    x_ref[...] = s_ref[...]
    blk = lax.broadcasted_iota(jnp.int32, x_ref.shape, 0)
    lane = lax.broadcasted_iota(jnp.int32, x_ref.shape, 2)
    key_id = blk * LANES + lane
    big = jnp.int32(2 ** 30)

    def body(j, c):
        x = x_ref[...]
        m = x.max(axis=0).max(axis=-1, keepdims=True)
        pick = jnp.where(x == m[None], key_id, big).min(axis=0).min(axis=-1, keepdims=True)
        x_ref[...] = jnp.where(key_id == pick[None], -jnp.inf, x)
        o_ref[j] = pick
        return c

    lax.fori_loop(0, n_sel, body, 0)


def dec_topk(s_blocks, n_sel):
    nb, rows, _ = s_blocks.shape
    return pl.pallas_call(
        functools.partial(_dec_topk_kernel, n_sel=n_sel),
        out_shape=jax.ShapeDtypeStruct((n_sel, rows, 1), jnp.int32),
        grid=(1,),
        in_specs=[pl.BlockSpec((nb, rows, LANES), lambda i: (0, 0, 0))],
        out_specs=pl.BlockSpec((n_sel, rows, 1), lambda i: (0, 0, 0)),
        scratch_shapes=[pltpu.VMEM((nb, rows, LANES), F32)],
        compiler_params=_cp(1),
        name="dec_topk",
    )(s_blocks)


def _dec_attn_kernel(idx_ref, pt_ref, ck_hbm, cv_hbm, kn_hbm, vn_hbm, q_ref, idxv_ref, rbt_ref, o_ref,
                     kbuf_ref, vbuf_ref, sem_ref, *, layer, t_len, n_pages, n_q, n_sel):
    r = pl.program_id(0)
    past = n_pages * PAGE_SIZE

    def issue(rr, slot):
        bb = rr // t_len

        def body(j, c):
            key = idx_ref[rr * n_sel + j]

            @pl.when(key < past)
            def _():
                phys = pt_ref[bb, key // PAGE_SIZE]
                row = key % PAGE_SIZE
                pltpu.make_async_copy(ck_hbm.at[layer, phys, row], kbuf_ref.at[slot, j],
                                      sem_ref.at[0, slot]).start()
                pltpu.make_async_copy(cv_hbm.at[layer, phys, row], vbuf_ref.at[slot, j],
                                      sem_ref.at[1, slot]).start()

            @pl.when(key >= past)
            def _():
                row = bb * t_len + jnp.minimum(key - past, t_len - 1)
                pltpu.make_async_copy(kn_hbm.at[row], kbuf_ref.at[slot, j], sem_ref.at[0, slot]).start()
                pltpu.make_async_copy(vn_hbm.at[row], vbuf_ref.at[slot, j], sem_ref.at[1, slot]).start()

            return c
        lax.fori_loop(0, n_sel, body, 0)

    @pl.when(r == 0)
    def _():
        issue(0, 0)

    @pl.when(r + 1 < n_q)
    def _():
        issue(r + 1, (r + 1) % 2)

    slot = r % 2
    pltpu.make_async_copy(kbuf_ref.at[slot], kbuf_ref.at[slot], sem_ref.at[0, slot]).wait()
    pltpu.make_async_copy(vbuf_ref.at[slot], vbuf_ref.at[slot], sem_ref.at[1, slot]).wait()

    qb = q_ref[...].astype(BF16)
    dist = (past + r % t_len) - idxv_ref[...]
    bucket = _rel_bucket(jnp.maximum(dist, 0))
    s_rows = []
    for h in range(N_HEADS):
        kh = kbuf_ref[slot, :, h, :].astype(BF16)
        s_rows.append(_dot_nt(qb, kh)[h:h + 1, :])
    s = jnp.concatenate(s_rows, axis=0)
    bias = jnp.zeros(s.shape, F32)
    for bkt in range(REL_BUCKETS):
        bias = jnp.where(bucket == bkt, rbt_ref[:, bkt:bkt + 1], bias)
    s = jnp.where(dist >= 0, s + bias, NEG)
    p = jnp.exp(s - s.max(axis=-1, keepdims=True))
    p = (p / p.sum(axis=-1, keepdims=True)).astype(BF16)
    for h in range(N_HEADS):
        vh = vbuf_ref[slot, :, h, :].astype(BF16)
        o_ref[h:h + 1, :] = _dot(p, vh)[h:h + 1, :]


def dec_attn(idx_flat, page_table, idx_rows, q, cache_k, cache_v, layer, k_new, v_new, rbt, t_len, n_sel):
    n_batch, n_pages = page_table.shape
    n_q = n_batch * t_len
    slab = (N_HEADS, HEAD_DIM)
    return pl.pallas_call(
        functools.partial(_dec_attn_kernel, layer=layer, t_len=t_len, n_pages=n_pages, n_q=n_q,
                          n_sel=n_sel),
        out_shape=jax.ShapeDtypeStruct((n_q,) + slab, F32),
        grid_spec=pltpu.PrefetchScalarGridSpec(
            num_scalar_prefetch=2,
            grid=(n_q,),
            in_specs=[pl.BlockSpec(memory_space=pl.ANY),
                      pl.BlockSpec(memory_space=pl.ANY),
                      pl.BlockSpec(memory_space=pl.ANY),
                      pl.BlockSpec(memory_space=pl.ANY),
                      pl.BlockSpec((None,) + slab, lambda r, ix, pt: (r, 0, 0)),
                      pl.BlockSpec((None, 1, n_sel), lambda r, ix, pt: (r, 0, 0)),
                      pl.BlockSpec((N_HEADS, REL_BUCKETS), lambda r, ix, pt: (0, 0))],
            out_specs=pl.BlockSpec((None,) + slab, lambda r, ix, pt: (r, 0, 0)),
            scratch_shapes=[pltpu.VMEM((2, n_sel) + slab, F32),
                            pltpu.VMEM((2, n_sel) + slab, F32),
                            pltpu.SemaphoreType.DMA((2, 2))]),
        compiler_params=_cp(1),
        name="dec_attn",
    )(idx_flat, page_table, cache_k, cache_v, k_new, v_new, q, idx_rows, rbt)


ATT_T = 256
IDX_TQ = 128
DEC_PAD = 16


def _attn_in_proj(xb, w_in, layer, tm, q_dtype=BF16):
    q, = proj(xb, w_in, layer, 0, D_MODEL, 1024, tm, [q_dtype], scale=HEAD_DIM ** -0.5, name="proj_q")
    k, kb = proj(xb, w_in, layer, D_MODEL, D_MODEL, 1024, tm, [F32, BF16], name="proj_k")
    v, vb = proj(xb, w_in, layer, 2 * D_MODEL, D_MODEL, 1024, tm, [F32, BF16], name="proj_v")
    iq, = proj(xb, w_in, layer, 3 * D_MODEL, IDX_HEADS * IDX_DIM, 1024, tm, [BF16], name="proj_iq")
    o4 = 3 * D_MODEL + IDX_HEADS * IDX_DIM
    w_tail = jnp.pad(w_in[layer, :, o4:], ((0, 0), (0, 2 * LANES - (IDX_DIM + IDX_HEADS))))
    tail, = proj(xb, w_tail, 0, 0, 2 * LANES, 2 * LANES, tm, [F32], name="proj_tail")
    ik = tail[:, :IDX_DIM]
    iw = tail[:, IDX_DIM:IDX_DIM + IDX_HEADS] * IDX_HEADS ** -0.5
    return q, k, v, kb, vb, iq, ik, iw


def _attn_prompt(xb, w_in, w_out_args, layer, rel_tiles, n_batch):
    m = xb.shape[0]
    t_len = m // n_batch
    q, k, v, kb, vb, iq, ik, iw = _attn_in_proj(xb, w_in, layer, 1024)
    ikt = jnp.swapaxes(ik.astype(BF16).reshape(n_batch, t_len, IDX_DIM), 1, 2)
    mask = idx_mask(iq, iw, ikt, n_batch, IDX_TQ)
    o = attn(q, kb, vb, mask, rel_tiles, n_batch, ATT_T)
    k = k.reshape(n_batch, t_len, N_HEADS, HEAD_DIM)
    v = v.reshape(n_batch, t_len, N_HEADS, HEAD_DIM)
    return o, k, v, ik.reshape(n_batch, t_len, IDX_DIM)


def _attn_sample(xb, w_in, layer, cache_k, cache_v, cache_idx_k, page_table, rbt, n_batch):
    m = xb.shape[0]
    t_len = m // n_batch
    q, k_new, v_new, _, _, iq, ik, iw = _attn_in_proj(xb, w_in, layer, m, q_dtype=F32)
    iq_r = iq.reshape(n_batch, t_len * IDX_HEADS, IDX_DIM)
    iw_r = iw.reshape(n_batch, t_len * IDX_HEADS, 1)
    ik_pad = jnp.pad(ik.reshape(n_batch, t_len, IDX_DIM), ((0, 0), (0, LANES - t_len), (0, 0)))
    sc = dec_scores(page_table, cache_idx_k, layer, iq_r, iw_r, ik_pad.astype(BF16), t_len)
    n_keys = sc.shape[-1]
    n_sel = min(TOPK_MAX, (n_keys - LANES + t_len) // 4)
    s_blocks = jnp.transpose(sc.reshape(m, n_keys // LANES, LANES), (1, 0, 2))
    sel = dec_topk(s_blocks, n_sel)
    sel = jnp.swapaxes(sel[:, :, 0], 0, 1)
    slab = (m, N_HEADS, HEAD_DIM)
    o = dec_attn(sel.reshape(-1), page_table, sel[:, None, :], q.reshape(slab), cache_k, cache_v, layer,
                 k_new.reshape(slab), v_new.reshape(slab), rbt, t_len, n_sel)
    return (o.reshape(m, D_MODEL).astype(BF16), k_new.reshape(n_batch, t_len, N_HEADS, HEAD_DIM),
            v_new.reshape(n_batch, t_len, N_HEADS, HEAD_DIM), ik.reshape(n_batch, t_len, IDX_DIM))


def _rot_tables(pos):
    half = RET_DK // 2
    inv = jnp.exp(-math.log(ROPE_BASE) * jnp.arange(half, dtype=F32) / half)
    ang = pos.astype(F32)[:, None] * inv[None, :]
    return jnp.cos(ang), jnp.sin(ang)


def _retention_mixer(xb, w_in, layer, gn, s0, lg, cos, sin, n_batch, chunk, pad_to):
    m = xb.shape[0]
    t_len = m // n_batch
    tm = min(1024, m)
    qk = proj_rot(xb, w_in, layer, cos, sin, 1024, tm)
    v, = proj(xb, w_in, layer, 2 * D_MODEL, RET_VW, 1024, tm, [BF16], name="proj_rv")
    g, = proj(xb, w_in, layer, 2 * D_MODEL + RET_VW, RET_VW, 1024, tm, [F32], name="proj_rg")
    if pad_to != t_len:
        padr = lambda a: jnp.pad(a.reshape(n_batch, t_len, -1),
                                 ((0, 0), (0, pad_to - t_len), (0, 0))).reshape(n_batch * pad_to, -1)
        z, s_fin = retention(lg, padr(qk), padr(v), padr(g), gn, s0, n_batch, pad_to, chunk)
        z = z.reshape(n_batch, pad_to, RET_VW)[:, :t_len].reshape(m, RET_VW)
    else:
        z, s_fin = retention(lg, qk, v, g, gn, s0, n_batch, chunk, chunk)
    return z, s_fin


MOE_TILE = 512
MOE_GU_TN = D_FF_EXPERT // 2


def _moe_dense(xf, xb, w_router_pad, w_gu, w_down_flat, layer, g, b):
    meta, _ = router(xf, w_router_pad, 512)
    gates_t = jnp.swapaxes(meta[:, :N_EXPERTS], 0, 1)[:, :, None]
    h = moe_gu(xb, w_gu, layer, gates_t, 256, 1024)
    return down_ln(h, w_down_flat, layer, xf, g, b, 512, 512, D_FF_EXPERT)


def _moe_sparse(xf, w_router_pad, w_gu, w_down, layer, g, b):
    m = xf.shape[0]
    meta, cnt = router(xf, w_router_pad, 512)
    col = lambda c: meta[:, c].astype(jnp.int32)
    counts = cnt[0, :N_EXPERTS].astype(jnp.int32)
    n_tile = (counts + MOE_TILE - 1) // MOE_TILE
    end_tile = jnp.cumsum(n_tile)
    start_row = (end_tile - n_tile) * MOE_TILE
    pos1 = start_row[col(META_E1)] + col(META_R1)
    pos2 = start_row[col(META_E2)] + col(META_R2)
    max_tiles = (2 * m + N_EXPERTS * (MOE_TILE - 1)) // MOE_TILE
    tiles = jnp.arange(max_tiles, dtype=jnp.int32)
    tile_expert = jnp.minimum(jnp.sum(tiles[:, None] >= end_tile[None, :], axis=1), N_EXPERTS - 1)
    n_valid = end_tile[-1:].astype(jnp.int32)
    xg = moe_dispatch(pos1, pos2, start_row + counts, end_tile * MOE_TILE, n_valid, xf, MOE_TILE, max_tiles)
    te = tile_expert.astype(jnp.int32)
    hg = moe_gu_grouped(te, n_valid, xg, w_gu, layer, MOE_TILE, MOE_GU_TN)
    yg = moe_down_grouped(te, n_valid, hg, w_down, layer, MOE_TILE, 512)
    return moe_combine_ln(pos1, pos2, yg, meta, xf, g, b, 256)


def kernel(x_prompt, x_sample, cache_k, cache_v, cache_idx_k, state_ret, page_table, rel_bias,
           w_in_attn, w_out_attn, w_in_ret, ret_gn_g, w_out_ret, w_ffn_gu, w_ffn_down,
           w_router, w_exp_gu, w_exp_down, ln_g, ln_b):
    bp, tp, _ = x_prompt.shape
    bs, ts, _ = x_sample.shape
    past = page_table.shape[1] * PAGE_SIZE
    mp, ms = bp * tp, bs * ts
    xp = x_prompt.reshape(mp, D_MODEL)
    xs = x_sample.reshape(ms, D_MODEL)
    xpb, xsb = xp.astype(BF16), xs.astype(BF16)

    rel_tiles = bias_tiles(rel_bias, ATT_T)
    rbt = rel_bias.T
    w_out_attn, w_out_ret, w_ffn_down = (w.astype(BF16) for w in (w_out_attn, w_out_ret, w_ffn_down))
    lg = jnp.log1p(-jnp.exp2(-5.0 - jnp.arange(RET_HEADS, dtype=F32)))
    cos_p, sin_p = _rot_tables(jnp.tile(jnp.arange(tp, dtype=jnp.int32), bp))
    cos_s, sin_s = _rot_tables(jnp.tile(past + jnp.arange(ts, dtype=jnp.int32), bs))
    w_exp_down_flat = w_exp_down.reshape(w_exp_down.shape[0], N_EXPERTS * D_FF_EXPERT, D_MODEL)
    zero_state = jnp.zeros((bp, RET_HEADS, RET_DK, RET_DV), F32)

    kp_l, vp_l, ikp_l, sp_l = [], [], [], []
    ks_l, vs_l, iks_l, ss_l = [], [], [], []
    for i in range(DEPTH):
        j = i // 2
        g0, b0 = ln_g[i, 0][None, :], ln_b[i, 0][None, :]
        g1, b1 = ln_g[i, 1][None, :], ln_b[i, 1][None, :]
        if i % 2 == 0:
            op, kp, vp, ikp = _attn_prompt(xpb, w_in_attn, None, j, rel_tiles, bp)
            os_, k_s, v_s, iks = _attn_sample(xsb, w_in_attn, j, cache_k, cache_v, cache_idx_k,
                                              page_table, rbt, bs)
            kp_l.append(kp); vp_l.append(vp); ikp_l.append(ikp)
            ks_l.append(k_s); vs_l.append(v_s); iks_l.append(iks)
            xp, xpb = down_ln(op, w_out_attn, j, xp, g0, b0, 512, 512, D_MODEL)
            xs, xsb = down_ln(os_, w_out_attn, j, xs, g0, b0, 512, 512, D_MODEL)
        else:
            gn = ret_gn_g[j][None, :]
            zp, sp = _retention_mixer(xpb, w_in_ret, j, gn, zero_state, lg, cos_p, sin_p,
                                      bp, min(RET_CHUNK, tp), tp)
            zs, ss = _retention_mixer(xsb, w_in_ret, j, gn, state_ret[j], lg, cos_s, sin_s,
                                      bs, ts, DEC_PAD)
            sp_l.append(sp); ss_l.append(ss)
            xp, xpb = down_ln(zp, w_out_ret, j, xp, g0, b0, 512, 512, RET_VW // 2)
            xs, xsb = down_ln(zs, w_out_ret, j, xs, g0, b0, 512, 512, RET_VW // 2)
        if i % 2 == 0:
            hp = ffn_gu(xpb, w_ffn_gu, j, 512, 1024)
            hs = ffn_gu(xsb, w_ffn_gu, j, 512, 1024)
            xp, xpb = down_ln(hp, w_ffn_down, j, xp, g1, b1, 512, 512, D_FF // 2)
            xs, xsb = down_ln(hs, w_ffn_down, j, xs, g1, b1, 512, 512, D_FF // 2)
        else:
            w_r = jnp.pad(w_router[j], ((0, 0), (0, LANES - N_EXPERTS)))
            xp, xpb = _moe_sparse(xp, w_r, w_exp_gu, w_exp_down, j, g1, b1)
            xs, xsb = _moe_dense(xs, xsb, w_r, w_exp_gu, w_exp_down_flat, j, g1, b1)

    return (xp.reshape(bp, tp, D_MODEL), xs.reshape(bs, ts, D_MODEL),
            jnp.stack(kp_l), jnp.stack(vp_l), jnp.stack(ikp_l), jnp.stack(sp_l),
            jnp.stack(ks_l), jnp.stack(vs_l), jnp.stack(iks_l), jnp.stack(ss_l))
```

```python
import functools
import math

import jax
import jax.numpy as jnp
import numpy as np
from jax import lax
from jax.experimental import pallas as pl
from jax.experimental.pallas import tpu as pltpu

F32 = jnp.float32
BF16 = jnp.bfloat16

D_MODEL = 2048
PAGE_SIZE = 128
N_HEADS = 16
HEAD_DIM = 128
IDX_HEADS = 16
IDX_DIM = 128
TOPK_MAX = 256
REL_BUCKETS = 32
REL_MAX_DIST = 128
RET_HEADS = 8
RET_DK = 256
RET_DV = 512
RET_VW = RET_HEADS * RET_DV
RET_CHUNK = 128
ROPE_BASE = 10000.0
D_FF = 5632
N_EXPERTS = 8
D_FF_EXPERT = 2816
DEPTH = 4
DN_ALPHA = (2 * DEPTH) ** 0.25
LN_EPS = 1e-5
NEG = -1e30

LANES = 128
VMEM_LIMIT = 56 * 1024 * 1024
INT_MIN = -2 ** 31
NEG_KEY = int(np.float32(NEG).view(np.int32)) ^ 0x7FFFFFFF


def _cp(n_axes, vmem=VMEM_LIMIT):
    return pltpu.CompilerParams(dimension_semantics=("arbitrary",) * n_axes,
                                vmem_limit_bytes=vmem)


def _silu(x):
    return x * (1.0 / (1.0 + jnp.exp(-x)))


def _dot(a, b):
    return jnp.dot(a, b, preferred_element_type=F32)


def _dot_nt(a, b):
    return lax.dot_general(a, b, (((1,), (1,)), ((), ())), preferred_element_type=F32)


def _dot_tn(a, b):
    return lax.dot_general(a, b, (((0,), (0,)), ((), ())), preferred_element_type=F32)


def _proj_kernel(a_ref, w_ref, *rest, n_out, scale):
    out_refs = rest[:n_out]
    wb_ref = rest[n_out]

    @pl.when(pl.program_id(1) == 0)
    def _():
        wb_ref[...] = w_ref[...].astype(BF16)

    acc = _dot(a_ref[...], wb_ref[...])
    if scale != 1.0:
        acc = acc * scale
    for o in out_refs:
        o[...] = acc.astype(o.dtype)


def proj(a, w, layer, col0, ncols, tn, tm, out_dtypes, scale=1.0, name="proj"):
    m, k = a.shape
    tm = min(tm, m)
    assert col0 % tn == 0 and ncols % tn == 0 and m % tm == 0
    off = col0 // tn
    if w.ndim == 3:
        w_spec = pl.BlockSpec((None, k, tn), lambda j, i: (layer, 0, j + off))
    else:
        w_spec = pl.BlockSpec((k, tn), lambda j, i: (0, j + off))
    outs = pl.pallas_call(
        functools.partial(_proj_kernel, n_out=len(out_dtypes), scale=scale),
        out_shape=[jax.ShapeDtypeStruct((m, ncols), dt) for dt in out_dtypes],
        grid=(ncols // tn, m // tm),
        in_specs=[pl.BlockSpec((tm, k), lambda j, i: (i, 0)), w_spec],
        out_specs=[pl.BlockSpec((tm, tn), lambda j, i: (i, j)) for _ in out_dtypes],
        scratch_shapes=[pltpu.VMEM((k, tn), BF16)],
        compiler_params=_cp(2),
        name=name,
    )(a, w)
    return outs


def _proj_rot_kernel(a_ref, w_ref, cos_ref, sin_ref, o_ref, wb_ref, *, heads_per_tile, q_tiles):
    j = pl.program_id(0)

    @pl.when(pl.program_id(1) == 0)
    def _():
        wb_ref[...] = w_ref[...].astype(BF16)

    acc = _dot(a_ref[...], wb_ref[...])
    scale = jnp.where(j >= q_tiles, RET_DK ** -0.5, 1.0).astype(F32)
    c = cos_ref[...]
    s = sin_ref[...]
    half = RET_DK // 2
    for hh in range(heads_per_tile):
        x1 = acc[:, hh * RET_DK: hh * RET_DK + half]
        x2 = acc[:, hh * RET_DK + half: (hh + 1) * RET_DK]
        o_ref[:, hh * RET_DK: hh * RET_DK + half] = ((x1 * c - x2 * s) * scale).astype(BF16)
        o_ref[:, hh * RET_DK + half: (hh + 1) * RET_DK] = ((x1 * s + x2 * c) * scale).astype(BF16)


def proj_rot(a, w, layer, cos, sin, tn, tm):
    m, k = a.shape
    tm = min(tm, m)
    ncols = 2 * D_MODEL
    return pl.pallas_call(
        functools.partial(_proj_rot_kernel, heads_per_tile=tn // RET_DK, q_tiles=D_MODEL // tn),
        out_shape=jax.ShapeDtypeStruct((m, ncols), BF16),
        grid=(ncols // tn, m // tm),
        in_specs=[pl.BlockSpec((tm, k), lambda j, i: (i, 0)),
                  pl.BlockSpec((None, k, tn), lambda j, i: (layer, 0, j)),
                  pl.BlockSpec((tm, RET_DK // 2), lambda j, i: (i, 0)),
                  pl.BlockSpec((tm, RET_DK // 2), lambda j, i: (i, 0))],
        out_specs=pl.BlockSpec((tm, tn), lambda j, i: (i, j)),
        scratch_shapes=[pltpu.VMEM((k, tn), BF16)],
        compiler_params=_cp(2),
        name="proj_rot",
    )(a, w, cos, sin)


def _gu_kernel(a_ref, wg_ref, wu_ref, *rest, gated, inner_axis):
    if gated:
        gate_ref, o_ref, wgb_ref, wub_ref = rest
    else:
        o_ref, wgb_ref, wub_ref = rest

    @pl.when(pl.program_id(inner_axis) == 0)
    def _():
        wgb_ref[...] = wg_ref[...].astype(BF16)
        wub_ref[...] = wu_ref[...].astype(BF16)

    a = a_ref[...]
    h = _silu(_dot(a, wgb_ref[...])) * _dot(a, wub_ref[...])
    if gated:
        h = h * gate_ref[...]
    o_ref[...] = h.astype(BF16)


def ffn_gu(a, w_gu, layer, tn, tm):
    m, k = a.shape
    tm = min(tm, m)
    nj = D_FF // tn
    return pl.pallas_call(
        functools.partial(_gu_kernel, gated=False, inner_axis=1),
        out_shape=jax.ShapeDtypeStruct((m, D_FF), BF16),
        grid=(nj, m // tm),
        in_specs=[pl.BlockSpec((tm, k), lambda j, i: (i, 0)),
                  pl.BlockSpec((None, k, tn), lambda j, i: (layer, 0, j)),
                  pl.BlockSpec((None, k, tn), lambda j, i: (layer, 0, j + nj))],
        out_specs=pl.BlockSpec((tm, tn), lambda j, i: (i, j)),
        scratch_shapes=[pltpu.VMEM((k, tn), BF16), pltpu.VMEM((k, tn), BF16)],
        compiler_params=_cp(2),
        name="ffn_gu",
    )(a, w_gu, w_gu)


def moe_gu(a, w_gu, layer, gates_t, tn, tm):
    m, k = a.shape
    tm = min(tm, m)
    nj = D_FF_EXPERT // tn
    return pl.pallas_call(
        functools.partial(_gu_kernel, gated=True, inner_axis=2),
        out_shape=jax.ShapeDtypeStruct((m, N_EXPERTS * D_FF_EXPERT), BF16),
        grid=(N_EXPERTS, nj, m // tm),
        in_specs=[pl.BlockSpec((tm, k), lambda e, j, i: (i, 0)),
                  pl.BlockSpec((None, None, k, tn), lambda e, j, i: (layer, e, 0, j)),
                  pl.BlockSpec((None, None, k, tn), lambda e, j, i: (layer, e, 0, j + nj)),
                  pl.BlockSpec((None, tm, 1), lambda e, j, i: (e, i, 0))],
        out_specs=pl.BlockSpec((tm, tn), lambda e, j, i: (i, e * nj + j)),
        scratch_shapes=[pltpu.VMEM((k, tn), BF16), pltpu.VMEM((k, tn), BF16)],
        compiler_params=_cp(3),
        name="moe_gu",
    )(a, w_gu, w_gu, gates_t)


def _down_ln_kernel(a_ref, w_ref, x_ref, g_ref, b_ref, of_ref, ob_ref, row_ref, *, nj, nk, tn):
    k = pl.program_id(1)
    j = pl.program_id(2)
    y = _dot(a_ref[...], w_ref[...].astype(BF16))

    @pl.when(k == 0)
    def _():
        row_ref[j] = y

    @pl.when(k > 0)
    def _():
        row_ref[j] = row_ref[j] + y

    @pl.when((j == nj - 1) & (k == nk - 1))
    def _():
        zs = [DN_ALPHA * x_ref[:, jj * tn:(jj + 1) * tn] + row_ref[jj] for jj in range(nj)]
        tot = zs[0].sum(axis=-1, keepdims=True)
        for z in zs[1:]:
            tot = tot + z.sum(axis=-1, keepdims=True)
        mu = tot * (1.0 / D_MODEL)
        sq = None
        for z in zs:
            t = jnp.square(z - mu).sum(axis=-1, keepdims=True)
            sq = t if sq is None else sq + t
        rstd = lax.rsqrt(sq * (1.0 / D_MODEL) + LN_EPS)
        for jj, z in enumerate(zs):
            sl = slice(jj * tn, (jj + 1) * tn)
            o = (z - mu) * rstd * g_ref[:, sl] + b_ref[:, sl]
            of_ref[:, sl] = o
            ob_ref[:, sl] = o.astype(BF16)


def down_ln(a, w, layer, x, g, b, tm, tn, tk):
    m, kk = a.shape
    tm = min(tm, m)
    n = D_MODEL
    assert kk % tk == 0 and n % tn == 0 and m % tm == 0
    nj, nk = n // tn, kk // tk
    return pl.pallas_call(
        functools.partial(_down_ln_kernel, nj=nj, nk=nk, tn=tn),
        out_shape=[jax.ShapeDtypeStruct((m, n), F32), jax.ShapeDtypeStruct((m, n), BF16)],
        grid=(m // tm, nk, nj),
        in_specs=[pl.BlockSpec((tm, tk), lambda i, k, j: (i, k)),
                  pl.BlockSpec((None, tk, tn), lambda i, k, j: (layer, k, j)),
                  pl.BlockSpec((tm, n), lambda i, k, j: (i, 0)),
                  pl.BlockSpec((1, n), lambda i, k, j: (0, 0)),
                  pl.BlockSpec((1, n), lambda i, k, j: (0, 0))],
        out_specs=[pl.BlockSpec((tm, n), lambda i, k, j: (i, 0)),
                   pl.BlockSpec((tm, n), lambda i, k, j: (i, 0))],
        scratch_shapes=[pltpu.VMEM((nj, tm, tn), F32)],
        compiler_params=_cp(3),
        name="down_ln",
    )(a, w, x, g, b)


def _router_kernel(x_ref, w_ref, o_ref, cnt_ref, carry_ref):
    x = x_ref[...]
    w = w_ref[...]
    xh = x.astype(BF16)
    xl = (x - xh.astype(F32)).astype(BF16)
    wh = w.astype(BF16)
    wl = (w - wh.astype(F32)).astype(BF16)
    lg = _dot(xh, wh) + (_dot(xh, wl) + _dot(xl, wh))
    lane = lax.broadcasted_iota(jnp.int32, lg.shape, 1)
    lg = jnp.where(lane < N_EXPERTS, lg, -jnp.inf)
    m1 = lg.max(axis=-1, keepdims=True)
    i1 = jnp.where(lg == m1, lane, LANES).min(axis=-1, keepdims=True)
    lg2 = jnp.where(lane == i1, -jnp.inf, lg)
    m2 = lg2.max(axis=-1, keepdims=True)
    i2 = jnp.where(lg2 == m2, lane, LANES).min(axis=-1, keepdims=True)
    e2 = jnp.exp(m2 - m1)
    den = 1.0 + e2
    g1 = 1.0 / den
    g2 = e2 / den
    @pl.when(pl.program_id(0) == 0)
    def _():
        carry_ref[...] = jnp.zeros(carry_ref.shape, F32)

    tm = lg.shape[0]
    hit = jnp.where(lane == i1, 1.0, jnp.where(lane == i2, 1.0, 0.0))
    rr = lax.broadcasted_iota(jnp.int32, (tm, tm), 0)
    cc = lax.broadcasted_iota(jnp.int32, (tm, tm), 1)
    tri = jnp.where(cc < rr, 1.0, 0.0).astype(BF16)
    before = _dot(tri, hit.astype(BF16)) + carry_ref[...]
    r1 = jnp.where(lane == i1, before, 0.0).sum(axis=-1, keepdims=True)
    r2 = jnp.where(lane == i2, before, 0.0).sum(axis=-1, keepdims=True)
    carry_ref[...] = carry_ref[...] + hit.sum(axis=0, keepdims=True)
    cnt_ref[...] = carry_ref[...]
    meta = jnp.where(lane == i1, g1, 0.0) + jnp.where(lane == i2, g2, 0.0)
    for col, val in ((META_E1, i1.astype(F32)), (META_E2, i2.astype(F32)), (META_G1, g1),
                     (META_G2, g2), (META_R1, r1), (META_R2, r2)):
        meta = jnp.where(lane == col, val, meta)
    o_ref[...] = meta


META_E1, META_E2, META_G1, META_G2, META_R1, META_R2 = 8, 9, 10, 11, 12, 13


def router(x, w_pad, tm):
    m, k = x.shape
    tm = min(tm, m)
    return pl.pallas_call(
        _router_kernel,
        out_shape=[jax.ShapeDtypeStruct((m, LANES), F32), jax.ShapeDtypeStruct((1, LANES), F32)],
        grid=(m // tm,),
        in_specs=[pl.BlockSpec((tm, k), lambda i: (i, 0)),
                  pl.BlockSpec((k, LANES), lambda i: (0, 0))],
        out_specs=[pl.BlockSpec((tm, LANES), lambda i: (i, 0)),
                   pl.BlockSpec((1, LANES), lambda i: (0, 0))],
        scratch_shapes=[pltpu.VMEM((1, LANES), F32)],
        compiler_params=_cp(1),
        name="router",
    )(x, w_pad)


def _dispatch_kernel(p1_ref, p2_ref, lo_ref, hi_ref, nv_ref, x_hbm, o_ref, src_ref, buf_ref, sem_ref,
                     *, m, tile, n_tiles):
    t = pl.program_id(0)
    nv = nv_ref[0]

    @pl.when(t == 0)
    def _():
        for e in range(N_EXPERTS):
            def clear(i, c):
                src_ref[i] = 0
                return c
            lax.fori_loop(lo_ref[e], hi_ref[e], clear, 0)

        def invert(n, c):
            src_ref[p1_ref[n]] = n
            src_ref[p2_ref[n]] = n
            return c
        lax.fori_loop(0, m, invert, 0, unroll=8)

    def issue(tt, slot):
        def body(i, c):
            tok = src_ref[tt * tile + i]
            pltpu.make_async_copy(x_hbm.at[pl.ds(tok, 1)], buf_ref.at[slot, pl.ds(i, 1)],
                                  sem_ref.at[slot]).start()
            return c
        lax.fori_loop(0, tile, body, 0, unroll=8)

    @pl.when(t == 0)
    def _():
        issue(0, 0)

    @pl.when(t + 1 < nv)
    def _():
        issue(t + 1, (t + 1) % 2)

    slot = t % 2

    @pl.when(t < nv)
    def _():
        pltpu.make_async_copy(buf_ref.at[slot], buf_ref.at[slot], sem_ref.at[slot]).wait()
        o_ref[...] = buf_ref[slot].astype(BF16)

    @pl.when(t >= nv)
    def _():
        o_ref[...] = jnp.zeros(o_ref.shape, o_ref.dtype)


def moe_dispatch(pos1, pos2, pad_lo, pad_hi, n_valid, x, tile, n_tiles):
    m, d = x.shape
    return pl.pallas_call(
        functools.partial(_dispatch_kernel, m=m, tile=tile, n_tiles=n_tiles),
        out_shape=jax.ShapeDtypeStruct((n_tiles * tile, d), BF16),
        grid_spec=pltpu.PrefetchScalarGridSpec(
            num_scalar_prefetch=5,
            grid=(n_tiles,),
            in_specs=[pl.BlockSpec(memory_space=pl.ANY)],
            out_specs=pl.BlockSpec((tile, d), lambda t, *_: (t, 0)),
            scratch_shapes=[pltpu.SMEM((n_tiles * tile,), jnp.int32),
                            pltpu.VMEM((2, tile, d), F32),
                            pltpu.SemaphoreType.DMA((2,))]),
        compiler_params=_cp(1),
        name="moe_dispatch",
    )(pos1, pos2, pad_lo, pad_hi, n_valid, x)


def _tile_ids(t, te_ref, nv_ref):
    nv = nv_ref[0]
    cur = te_ref[jnp.minimum(t, nv - 1)]
    prev = te_ref[jnp.minimum(jnp.maximum(t - 1, 0), nv - 1)]
    return nv, (t == 0) | (cur != prev)


def _ggu_kernel(te_ref, nv_ref, a_ref, wg_ref, wu_ref, o_ref, wgb_ref, wub_ref):
    t = pl.program_id(1)
    nv, new_expert = _tile_ids(t, te_ref, nv_ref)

    @pl.when(new_expert)
    def _():
        wgb_ref[...] = wg_ref[...].astype(BF16)
        wub_ref[...] = wu_ref[...].astype(BF16)

    @pl.when(t < nv)
    def _():
        a = a_ref[...]
        o_ref[...] = (_silu(_dot(a, wgb_ref[...])) * _dot(a, wub_ref[...])).astype(BF16)

    @pl.when(t >= nv)
    def _():
        o_ref[...] = jnp.zeros(o_ref.shape, o_ref.dtype)


def moe_gu_grouped(tile_expert, n_valid, a, w_gu, layer, tile, tn):
    r, k = a.shape
    nj = D_FF_EXPERT // tn
    row = lambda t, nv: jnp.minimum(t, nv[0] - 1)
    return pl.pallas_call(
        _ggu_kernel,
        out_shape=jax.ShapeDtypeStruct((r, D_FF_EXPERT), BF16),
        grid_spec=pltpu.PrefetchScalarGridSpec(
            num_scalar_prefetch=2,
            grid=(nj, r // tile),
            in_specs=[pl.BlockSpec((tile, k), lambda j, t, te, nv: (row(t, nv), 0)),
                      pl.BlockSpec((None, None, k, tn),
                                   lambda j, t, te, nv: (layer, te[row(t, nv)], 0, j),
                                   pipeline_mode=pl.Buffered(1)),
                      pl.BlockSpec((None, None, k, tn),
                                   lambda j, t, te, nv: (layer, te[row(t, nv)], 0, j + nj),
                                   pipeline_mode=pl.Buffered(1))],
            out_specs=pl.BlockSpec((tile, tn), lambda j, t, te, nv: (t, j)),
            scratch_shapes=[pltpu.VMEM((k, tn), BF16), pltpu.VMEM((k, tn), BF16)]),
        compiler_params=_cp(2),
        name="moe_gu_grouped",
    )(tile_expert, n_valid, a, w_gu, w_gu)


def _gdown_kernel(te_ref, nv_ref, a_ref, w_ref, o_ref, wb_ref):
    t = pl.program_id(1)
    nv, new_expert = _tile_ids(t, te_ref, nv_ref)

    @pl.when(new_expert)
    def _():
        wb_ref[...] = w_ref[...].astype(BF16)

    @pl.when(t < nv)
    def _():
        o_ref[...] = _dot(a_ref[...], wb_ref[...])

    @pl.when(t >= nv)
    def _():
        o_ref[...] = jnp.zeros(o_ref.shape, o_ref.dtype)


def moe_down_grouped(tile_expert, n_valid, a, w_down, layer, tile, tn):
    r, k = a.shape
    row = lambda t, nv: jnp.minimum(t, nv[0] - 1)
    return pl.pallas_call(
        _gdown_kernel,
        out_shape=jax.ShapeDtypeStruct((r, D_MODEL), F32),
        grid_spec=pltpu.PrefetchScalarGridSpec(
            num_scalar_prefetch=2,
            grid=(D_MODEL // tn, r // tile),
            in_specs=[pl.BlockSpec((tile, k), lambda j, t, te, nv: (row(t, nv), 0)),
                      pl.BlockSpec((None, None, k, tn),
                                   lambda j, t, te, nv: (layer, te[row(t, nv)], 0, j))],
            out_specs=pl.BlockSpec((tile, tn), lambda j, t, te, nv: (t, j)),
            scratch_shapes=[pltpu.VMEM((k, tn), BF16)]),
        compiler_params=_cp(2),
        name="moe_down_grouped",
    )(tile_expert, n_valid, a, w_down)


def _combine_kernel(p1_ref, p2_ref, y_hbm, meta_ref, x_ref, g_ref, b_ref, of_ref, ob_ref,
                    ybuf_ref, sem_ref, *, tm, n_tiles):
    i = pl.program_id(0)

    def issue(ii, slot):
        def body(r, c):
            n = ii * tm + r
            pltpu.make_async_copy(y_hbm.at[pl.ds(p1_ref[n], 1)], ybuf_ref.at[slot, 0, pl.ds(r, 1)],
                                  sem_ref.at[slot]).start()
            pltpu.make_async_copy(y_hbm.at[pl.ds(p2_ref[n], 1)], ybuf_ref.at[slot, 1, pl.ds(r, 1)],
                                  sem_ref.at[slot]).start()
            return c
        lax.fori_loop(0, tm, body, 0)

    @pl.when(i == 0)
    def _():
        issue(0, 0)

    @pl.when(i + 1 < n_tiles)
    def _():
        issue(i + 1, (i + 1) % 2)

    slot = i % 2
    pltpu.make_async_copy(ybuf_ref.at[slot], ybuf_ref.at[slot], sem_ref.at[slot]).wait()
    meta = meta_ref[...]
    lane = lax.broadcasted_iota(jnp.int32, meta.shape, 1)
    g1 = jnp.where(lane == META_G1, meta, 0.0).sum(axis=-1, keepdims=True)
    g2 = jnp.where(lane == META_G2, meta, 0.0).sum(axis=-1, keepdims=True)
    z = DN_ALPHA * x_ref[...] + (g1 * ybuf_ref[slot, 0] + g2 * ybuf_ref[slot, 1])
    mu = z.mean(axis=-1, keepdims=True)
    var = jnp.square(z - mu).mean(axis=-1, keepdims=True)
    o = (z - mu) * lax.rsqrt(var + LN_EPS) * g_ref[...] + b_ref[...]
    of_ref[...] = o
    ob_ref[...] = o.astype(BF16)


def moe_combine_ln(pos1, pos2, y, meta, x, g, b, tm):
    m, d = x.shape
    n_tiles = m // tm
    tok = lambda i, p1, p2: (i, 0)
    vec = lambda i, p1, p2: (0, 0)
    return pl.pallas_call(
        functools.partial(_combine_kernel, tm=tm, n_tiles=n_tiles),
        out_shape=[jax.ShapeDtypeStruct((m, d), F32), jax.ShapeDtypeStruct((m, d), BF16)],
        grid_spec=pltpu.PrefetchScalarGridSpec(
            num_scalar_prefetch=2,
            grid=(n_tiles,),
            in_specs=[pl.BlockSpec(memory_space=pl.ANY),
                      pl.BlockSpec((tm, LANES), tok),
                      pl.BlockSpec((tm, d), tok),
                      pl.BlockSpec((1, d), vec),
                      pl.BlockSpec((1, d), vec)],
            out_specs=[pl.BlockSpec((tm, d), tok), pl.BlockSpec((tm, d), tok)],
            scratch_shapes=[pltpu.VMEM((2, 2, tm, d), F32),
                            pltpu.SemaphoreType.DMA((2,))]),
        compiler_params=_cp(1),
        name="moe_combine_ln",
    )(pos1, pos2, y, meta, x, g, b)


def _sort_key(x):
    k = pltpu.bitcast(x, jnp.int32)
    return jnp.where(k < 0, k ^ jnp.int32(0x7FFFFFFF), k)


def _kth_largest_key(count_ge, n_sel, shape):
    zero = jnp.zeros(shape, jnp.int32)
    t0 = jnp.where(count_ge(zero) >= n_sel, zero, jnp.full(shape, INT_MIN, jnp.int32))

    def body(i, t):
        cand = t | jnp.left_shift(jnp.int32(1), 30 - i)
        return jnp.where(count_ge(cand) >= n_sel, cand, t)

    return lax.fori_loop(0, 31, body, t0)


IDX_GROUPS = 4


def _idx_mask_kernel(iq_ref, iw_ref, ikt_ref, o_ref, key_ref, *, tq, n_sel):
    i = pl.program_id(1)
    s_len = ikt_ref.shape[1]
    per = s_len // tq // IDX_GROUPS
    wf = iw_ref[...] * (IDX_DIM ** -0.5)

    def run(s_eff):
        ikt = ikt_ref[:, :s_eff]
        sc = None
        for h in range(IDX_HEADS):
            d = _dot(iq_ref[:, h * IDX_DIM:(h + 1) * IDX_DIM], ikt)
            t = jnp.maximum(d, 0.0) * wf[:, h:h + 1]
            sc = t if sc is None else sc + t
        qpos = i * tq + lax.broadcasted_iota(jnp.int32, (tq, s_eff), 0)
        kpos = lax.broadcasted_iota(jnp.int32, (tq, s_eff), 1)
        valid = kpos <= qpos
        key_ref[:, :s_eff] = _sort_key(jnp.where(valid, sc, NEG))
        n_tail = float(s_len - s_eff)

        def count_ge(cand):
            c = jnp.where(key_ref[:, :s_eff] >= cand, 1.0, 0.0).sum(axis=-1, keepdims=True)
            return c + jnp.where(cand <= NEG_KEY, n_tail, 0.0)

        thr = _kth_largest_key(count_ge, float(n_sel), (tq, 1))
        sel = (key_ref[:, :s_eff] >= thr) & valid
        o_ref[:, :s_eff] = jnp.where(sel, 0.0, NEG).astype(o_ref.dtype)
        if s_eff < s_len:
            o_ref[:, s_eff:] = jnp.full((tq, s_len - s_eff), NEG, o_ref.dtype)

    for g in range(IDX_GROUPS):
        pl.when(i // per == g)(functools.partial(run, (g + 1) * per * tq))


def idx_mask(iq, iw, ikt, n_batch, tq):
    m = iq.shape[0]
    s_len = ikt.shape[2]
    nq = s_len // tq
    assert nq % IDX_GROUPS == 0
    n_sel = min(TOPK_MAX, s_len // 4)
    return pl.pallas_call(
        functools.partial(_idx_mask_kernel, tq=tq, n_sel=n_sel),
        out_shape=jax.ShapeDtypeStruct((n_batch, s_len, s_len), BF16),
        grid=(n_batch, nq),
        in_specs=[pl.BlockSpec((tq, IDX_HEADS * IDX_DIM), lambda b, i: (b * nq + i, 0)),
                  pl.BlockSpec((tq, IDX_HEADS), lambda b, i: (b * nq + i, 0)),
                  pl.BlockSpec((None, IDX_DIM, s_len), lambda b, i: (b, 0, 0))],
        out_specs=pl.BlockSpec((None, tq, s_len), lambda b, i: (b, i, 0)),
        scratch_shapes=[pltpu.VMEM((tq, s_len), jnp.int32)],
        compiler_params=_cp(2),
        name="idx_mask",
    )(iq, iw, ikt)


def _rel_bucket(dist):
    max_exact = REL_BUCKETS // 2
    large = max_exact + (jnp.log(jnp.maximum(dist, 1).astype(F32) / max_exact)
                         / math.log(REL_MAX_DIST / max_exact) * (REL_BUCKETS - max_exact)).astype(jnp.int32)
    large = jnp.minimum(large, REL_BUCKETS - 1)
    return jnp.where(dist < max_exact, dist, large)


def _bias_tiles_kernel(rb_ref, o_ref, *, t):
    d = pl.program_id(0)
    ii = lax.broadcasted_iota(jnp.int32, (t, t), 0)
    jj = lax.broadcasted_iota(jnp.int32, (t, t), 1)
    bucket = _rel_bucket(jnp.maximum(d * t + ii - jj, 0))
    for h in range(N_HEADS):
        acc = jnp.zeros((t, t), F32)
        for bkt in range(REL_BUCKETS):
            acc = jnp.where(bucket == bkt, rb_ref[bkt, h], acc)
        o_ref[h] = acc


def bias_tiles(rel_bias, t):
    assert 2 * t - (t - 1) >= REL_MAX_DIST
    return pl.pallas_call(
        functools.partial(_bias_tiles_kernel, t=t),
        out_shape=jax.ShapeDtypeStruct((3, N_HEADS, t, t), F32),
        grid=(3,),
        in_specs=[pl.BlockSpec(memory_space=pltpu.SMEM)],
        out_specs=pl.BlockSpec((None, N_HEADS, t, t), lambda d: (d, 0, 0, 0)),
        compiler_params=_cp(1),
        name="bias_tiles",
    )(rel_bias)


def _attn_kernel(it_ref, jt_ref, q_ref, kt_ref, v_ref, mask_ref, tz_ref, o_ref, m_ref, acc_ref):
    step = pl.program_id(1)
    i = it_ref[step]
    j = jt_ref[step]
    t = q_ref.shape[0]

    @pl.when(j == 0)
    def _():
        m_ref[...] = jnp.full(m_ref.shape, -jnp.inf, F32)
        acc_ref[...] = jnp.zeros(acc_ref.shape, F32)

    maskf = mask_ref[...].astype(F32)
    ones = jnp.ones((t, HEAD_DIM), BF16)
    twice = lambda a: jnp.concatenate([a] * (t // LANES), axis=1)
    for h in range(N_HEADS):
        sl = slice(h * HEAD_DIM, (h + 1) * HEAD_DIM)
        s = _dot(q_ref[:, sl], kt_ref[sl, :]) + tz_ref[h] + maskf
        m_prev = m_ref[h]
        m_new = jnp.maximum(m_prev, s.max(axis=-1, keepdims=True))
        alpha = jnp.exp(m_prev - m_new)
        p = jnp.exp(s - twice(m_new))
        pv = _dot(p.astype(BF16), jnp.concatenate([v_ref[:, sl], ones], axis=1))
        acc_ref[h] = jnp.concatenate([alpha, alpha], axis=1) * acc_ref[h] + pv
        m_ref[h] = m_new

    @pl.when(j == i)
    def _():
        for h in range(N_HEADS):
            a = acc_ref[h]
            o_ref[:, h * HEAD_DIM:(h + 1) * HEAD_DIM] = (a[:, :HEAD_DIM] / a[:, HEAD_DIM:]).astype(o_ref.dtype)


def attn(q, kt, v, mask, tz, n_batch, t):
    m = q.shape[0]
    nq = m // n_batch // t
    pairs = [(i, j) for i in range(nq) for j in range(i + 1)]
    it = jnp.asarray([p[0] for p in pairs], jnp.int32)
    jt = jnp.asarray([p[1] for p in pairs], jnp.int32)
    return pl.pallas_call(
        _attn_kernel,
        out_shape=jax.ShapeDtypeStruct((m, D_MODEL), BF16),
        grid_spec=pltpu.PrefetchScalarGridSpec(
            num_scalar_prefetch=2,
            grid=(n_batch, len(pairs)),
            in_specs=[pl.BlockSpec((t, D_MODEL), lambda b, s, it, jt: (b * nq + it[s], 0)),
                      pl.BlockSpec((None, D_MODEL, t), lambda b, s, it, jt: (b, 0, jt[s])),
                      pl.BlockSpec((t, D_MODEL), lambda b, s, it, jt: (b * nq + jt[s], 0)),
                      pl.BlockSpec((None, t, t), lambda b, s, it, jt: (b, it[s], jt[s])),
                      pl.BlockSpec((None, N_HEADS, t, t),
                                   lambda b, s, it, jt: (jnp.minimum(it[s] - jt[s], 2), 0, 0, 0))],
            out_specs=pl.BlockSpec((t, D_MODEL), lambda b, s, it, jt: (b * nq + it[s], 0)),
            scratch_shapes=[pltpu.VMEM((N_HEADS, t, LANES), F32),
                            pltpu.VMEM((N_HEADS, t, 2 * HEAD_DIM), F32)]),
        compiler_params=_cp(2),
        name="attn",
    )(it, jt, q, kt, v, mask, tz)


def _ret_kernel(lg_ref, q_ref, k_ref, v_ref, g_ref, gn_ref, s0_ref, z_ref, sout_ref, state_ref,
                *, c_pad, c_true, nc):
    c = pl.program_id(1)

    @pl.when(c == 0)
    def _():
        state_ref[...] = s0_ref[...]

    ii = lax.broadcasted_iota(jnp.int32, (c_pad, c_pad), 0)
    jj = lax.broadcasted_iota(jnp.int32, (c_pad, c_pad), 1)
    rel = (ii - jj).astype(F32)
    idx = lax.broadcasted_iota(jnp.int32, (c_pad, 1), 0).astype(F32)
    for h in range(RET_HEADS):
        lg = lg_ref[h]
        qs = slice(h * RET_DK, (h + 1) * RET_DK)
        vs = slice(h * RET_DV, (h + 1) * RET_DV)
        q = q_ref[:, qs]
        k = k_ref[:, qs]
        v = v_ref[:, vs]
        decay = jnp.where(rel >= 0, jnp.exp(lg * jnp.maximum(rel, 0.0)), 0.0)
        scores = _dot_nt(q, k) * decay
        inner = _dot(scores.astype(BF16), v)
        s_prev = state_ref[h]
        cross = _dot(q, s_prev.astype(BF16)) * jnp.exp(lg * (idx + 1.0))
        o = inner + cross
        w_state = jnp.exp(lg * (c_true - 1.0 - idx))
        kw = (k.astype(F32) * w_state).astype(BF16)
        state_ref[h] = s_prev * jnp.exp(lg * c_true) + _dot_tn(kw, v)

        mu = o.mean(axis=-1, keepdims=True)
        var = jnp.square(o - mu).mean(axis=-1, keepdims=True)
        on = (o - mu) * lax.rsqrt(var + LN_EPS) * gn_ref[:, vs]
        z_ref[:, vs] = (_silu(g_ref[:, vs]) * on).astype(BF16)

    @pl.when(c == nc - 1)
    def _():
        sout_ref[...] = state_ref[...]


def retention(lg, qk, v, g, gn, s0, n_batch, c_pad, c_true):
    m = qk.shape[0]
    nc = m // n_batch // c_pad
    row = lambda b, c: b * nc + c
    state = (None, RET_HEADS, RET_DK, RET_DV)
    return pl.pallas_call(
        functools.partial(_ret_kernel, c_pad=c_pad, c_true=float(c_true), nc=nc),
        out_shape=[jax.ShapeDtypeStruct((m, RET_VW), BF16),
                   jax.ShapeDtypeStruct((n_batch, RET_HEADS, RET_DK, RET_DV), F32)],
        grid=(n_batch, nc),
        in_specs=[pl.BlockSpec(memory_space=pltpu.SMEM),
                  pl.BlockSpec((c_pad, D_MODEL), lambda b, c: (row(b, c), 0)),
                  pl.BlockSpec((c_pad, D_MODEL), lambda b, c: (row(b, c), 1)),
                  pl.BlockSpec((c_pad, RET_VW), lambda b, c: (row(b, c), 0)),
                  pl.BlockSpec((c_pad, RET_VW), lambda b, c: (row(b, c), 0)),
                  pl.BlockSpec((1, RET_VW), lambda b, c: (0, 0)),
                  pl.BlockSpec(state, lambda b, c: (b, 0, 0, 0))],
        out_specs=[pl.BlockSpec((c_pad, RET_VW), lambda b, c: (row(b, c), 0)),
                   pl.BlockSpec(state, lambda b, c: (b, 0, 0, 0))],
        scratch_shapes=[pltpu.VMEM((RET_HEADS, RET_DK, RET_DV), F32)],
        compiler_params=_cp(2),
        name="retention",
    )(lg, qk, qk, v, g, gn, s0)


DEC_CHUNK = 2048


def _dec_scores_kernel(pt_ref, ik_hbm, iq_ref, iw_ref, ikn_ref, o_ref, buf_ref, sem_ref,
                       *, layer, t_len, n_pages, n_batch):
    b = pl.program_id(0)
    past = n_pages * PAGE_SIZE

    def issue(bb, slot):
        def body(p, c):
            pltpu.make_async_copy(ik_hbm.at[layer, pt_ref[bb, p]],
                                  buf_ref.at[slot, pl.ds(p * PAGE_SIZE, PAGE_SIZE)],
                                  sem_ref.at[slot]).start()
            return c
        lax.fori_loop(0, n_pages, body, 0)

    @pl.when(b == 0)
    def _():
        issue(0, 0)

    @pl.when(b + 1 < n_batch)
    def _():
        issue(b + 1, (b + 1) % 2)

    slot = b % 2
    pltpu.make_async_copy(buf_ref.at[slot], buf_ref.at[slot], sem_ref.at[slot]).wait()
    iq = iq_ref[...]
    wf = iw_ref[...] * (IDX_DIM ** -0.5)

    def head_sum(keys_bf16):
        r = jnp.maximum(_dot_nt(iq, keys_bf16), 0.0) * wf
        return r.reshape(t_len, IDX_HEADS, r.shape[-1]).sum(axis=1)

    chunk = min(DEC_CHUNK, past)
    for c in range(past // chunk):
        sl = slice(c * chunk, (c + 1) * chunk)
        o_ref[:, sl] = head_sum(buf_ref[slot, sl, :].astype(BF16))
    sn = head_sum(ikn_ref[...])
    qi = lax.broadcasted_iota(jnp.int32, sn.shape, 0)
    si = lax.broadcasted_iota(jnp.int32, sn.shape, 1)
    o_ref[:, past:] = jnp.where(si < t_len, jnp.where(si <= qi, sn, NEG), -jnp.inf)


def dec_scores(page_table, cache_idx_k, layer, iq, iw, ik_new, t_len):
    n_batch, n_pages = page_table.shape
    past = n_pages * PAGE_SIZE
    assert past % min(DEC_CHUNK, past) == 0
    rows = t_len * IDX_HEADS
    return pl.pallas_call(
        functools.partial(_dec_scores_kernel, layer=layer, t_len=t_len, n_pages=n_pages,
                          n_batch=n_batch),
        out_shape=jax.ShapeDtypeStruct((n_batch, t_len, past + LANES), F32),
        grid_spec=pltpu.PrefetchScalarGridSpec(
            num_scalar_prefetch=1,
            grid=(n_batch,),
            in_specs=[pl.BlockSpec(memory_space=pl.ANY),
                      pl.BlockSpec((None, rows, IDX_DIM), lambda b, pt: (b, 0, 0)),
                      pl.BlockSpec((None, rows, 1), lambda b, pt: (b, 0, 0)),
                      pl.BlockSpec((None, LANES, IDX_DIM), lambda b, pt: (b, 0, 0))],
            out_specs=pl.BlockSpec((None, t_len, past + LANES), lambda b, pt: (b, 0, 0)),
            scratch_shapes=[pltpu.VMEM((2, past, IDX_DIM), F32),
                            pltpu.SemaphoreType.DMA((2,))]),
        compiler_params=_cp(1),
        name="dec_scores",
    )(page_table, cache_idx_k, iq, iw, ik_new)


def _dec_topk_kernel(s_ref, o_ref, x_ref, *, n_sel):
    x_ref[...] = s_ref[...]
    blk = lax.broadcasted_iota(jnp.int32, x_ref.shape, 0)
    lane = lax.broadcasted_iota(jnp.int32, x_ref.shape, 2)
    key_id = blk * LANES + lane
    big = jnp.int32(2 ** 30)

    def body(j, c):
        x = x_ref[...]
        m = x.max(axis=0).max(axis=-1, keepdims=True)
        pick = jnp.where(x == m[None], key_id, big).min(axis=0).min(axis=-1, keepdims=True)
        x_ref[...] = jnp.where(key_id == pick[None], -jnp.inf, x)
        o_ref[j] = pick
        return c

    lax.fori_loop(0, n_sel, body, 0)


def dec_topk(s_blocks, n_sel):
    nb, rows, _ = s_blocks.shape
    return pl.pallas_call(
        functools.partial(_dec_topk_kernel, n_sel=n_sel),
        out_shape=jax.ShapeDtypeStruct((n_sel, rows, 1), jnp.int32),
        grid=(1,),
        in_specs=[pl.BlockSpec((nb, rows, LANES), lambda i: (0, 0, 0))],
        out_specs=pl.BlockSpec((n_sel, rows, 1), lambda i: (0, 0, 0)),
        scratch_shapes=[pltpu.VMEM((nb, rows, LANES), F32)],
        compiler_params=_cp(1),
        name="dec_topk",
    )(s_blocks)


def _dec_attn_kernel(idx_ref, pt_ref, ck_hbm, cv_hbm, kn_hbm, vn_hbm, q_ref, idxv_ref, rbt_ref, o_ref,
                     kbuf_ref, vbuf_ref, sem_ref, *, layer, t_len, n_pages, n_q, n_sel):
    r = pl.program_id(0)
    past = n_pages * PAGE_SIZE

    def issue(rr, slot):
        bb = rr // t_len

        def body(j, c):
            key = idx_ref[rr * n_sel + j]

            @pl.when(key < past)
            def _():
                phys = pt_ref[bb, key // PAGE_SIZE]
                row = key % PAGE_SIZE
                pltpu.make_async_copy(ck_hbm.at[layer, phys, row], kbuf_ref.at[slot, j],
                                      sem_ref.at[0, slot]).start()
                pltpu.make_async_copy(cv_hbm.at[layer, phys, row], vbuf_ref.at[slot, j],
                                      sem_ref.at[1, slot]).start()

            @pl.when(key >= past)
            def _():
                row = bb * t_len + jnp.minimum(key - past, t_len - 1)
                pltpu.make_async_copy(kn_hbm.at[row], kbuf_ref.at[slot, j], sem_ref.at[0, slot]).start()
                pltpu.make_async_copy(vn_hbm.at[row], vbuf_ref.at[slot, j], sem_ref.at[1, slot]).start()

            return c
        lax.fori_loop(0, n_sel, body, 0)

    @pl.when(r == 0)
    def _():
        issue(0, 0)

    @pl.when(r + 1 < n_q)
    def _():
        issue(r + 1, (r + 1) % 2)

    slot = r % 2
    pltpu.make_async_copy(kbuf_ref.at[slot], kbuf_ref.at[slot], sem_ref.at[0, slot]).wait()
    pltpu.make_async_copy(vbuf_ref.at[slot], vbuf_ref.at[slot], sem_ref.at[1, slot]).wait()

    qb = q_ref[...].astype(BF16)
    dist = (past + r % t_len) - idxv_ref[...]
    bucket = _rel_bucket(jnp.maximum(dist, 0))
    s_rows = []
    for h in range(N_HEADS):
        kh = kbuf_ref[slot, :, h, :].astype(BF16)
        s_rows.append(_dot_nt(qb, kh)[h:h + 1, :])
    s = jnp.concatenate(s_rows, axis=0)
    bias = jnp.zeros(s.shape, F32)
    for bkt in range(REL_BUCKETS):
        bias = jnp.where(bucket == bkt, rbt_ref[:, bkt:bkt + 1], bias)
    s = jnp.where(dist >= 0, s + bias, NEG)
    p = jnp.exp(s - s.max(axis=-1, keepdims=True))
    p = (p / p.sum(axis=-1, keepdims=True)).astype(BF16)
    for h in range(N_HEADS):
        vh = vbuf_ref[slot, :, h, :].astype(BF16)
        o_ref[h:h + 1, :] = _dot(p, vh)[h:h + 1, :]


def dec_attn(idx_flat, page_table, idx_rows, q, cache_k, cache_v, layer, k_new, v_new, rbt, t_len, n_sel):
    n_batch, n_pages = page_table.shape
    n_q = n_batch * t_len
    slab = (N_HEADS, HEAD_DIM)
    return pl.pallas_call(
        functools.partial(_dec_attn_kernel, layer=layer, t_len=t_len, n_pages=n_pages, n_q=n_q,
                          n_sel=n_sel),
        out_shape=jax.ShapeDtypeStruct((n_q,) + slab, F32),
        grid_spec=pltpu.PrefetchScalarGridSpec(
            num_scalar_prefetch=2,
            grid=(n_q,),
            in_specs=[pl.BlockSpec(memory_space=pl.ANY),
                      pl.BlockSpec(memory_space=pl.ANY),
                      pl.BlockSpec(memory_space=pl.ANY),
                      pl.BlockSpec(memory_space=pl.ANY),
                      pl.BlockSpec((None,) + slab, lambda r, ix, pt: (r, 0, 0)),
                      pl.BlockSpec((None, 1, n_sel), lambda r, ix, pt: (r, 0, 0)),
                      pl.BlockSpec((N_HEADS, REL_BUCKETS), lambda r, ix, pt: (0, 0))],
            out_specs=pl.BlockSpec((None,) + slab, lambda r, ix, pt: (r, 0, 0)),
            scratch_shapes=[pltpu.VMEM((2, n_sel) + slab, F32),
                            pltpu.VMEM((2, n_sel) + slab, F32),
                            pltpu.SemaphoreType.DMA((2, 2))]),
        compiler_params=_cp(1),
        name="dec_attn",
    )(idx_flat, page_table, cache_k, cache_v, k_new, v_new, q, idx_rows, rbt)


ATT_T = 256
IDX_TQ = 128
DEC_PAD = 16


def _attn_in_proj(xb, w_in, layer, tm, q_dtype=BF16):
    q, = proj(xb, w_in, layer, 0, D_MODEL, 1024, tm, [q_dtype], scale=HEAD_DIM ** -0.5, name="proj_q")
    k, kb = proj(xb, w_in, layer, D_MODEL, D_MODEL, 1024, tm, [F32, BF16], name="proj_k")
    v, vb = proj(xb, w_in, layer, 2 * D_MODEL, D_MODEL, 1024, tm, [F32, BF16], name="proj_v")
    iq, = proj(xb, w_in, layer, 3 * D_MODEL, IDX_HEADS * IDX_DIM, 1024, tm, [BF16], name="proj_iq")
    o4 = 3 * D_MODEL + IDX_HEADS * IDX_DIM
    w_tail = jnp.pad(w_in[layer, :, o4:], ((0, 0), (0, 2 * LANES - (IDX_DIM + IDX_HEADS))))
    tail, = proj(xb, w_tail, 0, 0, 2 * LANES, 2 * LANES, tm, [F32], name="proj_tail")
    ik = tail[:, :IDX_DIM]
    iw = tail[:, IDX_DIM:IDX_DIM + IDX_HEADS] * IDX_HEADS ** -0.5
    return q, k, v, kb, vb, iq, ik, iw


def _attn_prompt(xb, w_in, w_out_args, layer, rel_tiles, n_batch):
    m = xb.shape[0]
    t_len = m // n_batch
    q, k, v, kb, vb, iq, ik, iw = _attn_in_proj(xb, w_in, layer, 1024)
    ikt = jnp.swapaxes(ik.astype(BF16).reshape(n_batch, t_len, IDX_DIM), 1, 2)
    mask = idx_mask(iq, iw, ikt, n_batch, IDX_TQ)
    kt = jnp.swapaxes(kb.reshape(n_batch, t_len, D_MODEL), 1, 2)
    o = attn(q, kt, vb, mask, rel_tiles, n_batch, ATT_T)
    k = k.reshape(n_batch, t_len, N_HEADS, HEAD_DIM)
    v = v.reshape(n_batch, t_len, N_HEADS, HEAD_DIM)
    return o, k, v, ik.reshape(n_batch, t_len, IDX_DIM)


def _attn_sample(xb, w_in, layer, cache_k, cache_v, cache_idx_k, page_table, rbt, n_batch):
    m = xb.shape[0]
    t_len = m // n_batch
    q, k_new, v_new, _, _, iq, ik, iw = _attn_in_proj(xb, w_in, layer, m, q_dtype=F32)
    iq_r = iq.reshape(n_batch, t_len * IDX_HEADS, IDX_DIM)
    iw_r = iw.reshape(n_batch, t_len * IDX_HEADS, 1)
    ik_pad = jnp.pad(ik.reshape(n_batch, t_len, IDX_DIM), ((0, 0), (0, LANES - t_len), (0, 0)))
    sc = dec_scores(page_table, cache_idx_k, layer, iq_r, iw_r, ik_pad.astype(BF16), t_len)
    n_keys = sc.shape[-1]
    n_sel = min(TOPK_MAX, (n_keys - LANES + t_len) // 4)
    s_blocks = jnp.transpose(sc.reshape(m, n_keys // LANES, LANES), (1, 0, 2))
    sel = dec_topk(s_blocks, n_sel)
    sel = jnp.swapaxes(sel[:, :, 0], 0, 1)
    slab = (m, N_HEADS, HEAD_DIM)
    o = dec_attn(sel.reshape(-1), page_table, sel[:, None, :], q.reshape(slab), cache_k, cache_v, layer,
                 k_new.reshape(slab), v_new.reshape(slab), rbt, t_len, n_sel)
    return (o.reshape(m, D_MODEL).astype(BF16), k_new.reshape(n_batch, t_len, N_HEADS, HEAD_DIM),
            v_new.reshape(n_batch, t_len, N_HEADS, HEAD_DIM), ik.reshape(n_batch, t_len, IDX_DIM))


def _rot_tables(pos):
    half = RET_DK // 2
    inv = jnp.exp(-math.log(ROPE_BASE) * jnp.arange(half, dtype=F32) / half)
    ang = pos.astype(F32)[:, None] * inv[None, :]
    return jnp.cos(ang), jnp.sin(ang)


def _retention_mixer(xb, w_in, layer, gn, s0, lg, cos, sin, n_batch, chunk, pad_to):
    m = xb.shape[0]
    t_len = m // n_batch
    tm = min(1024, m)
    qk = proj_rot(xb, w_in, layer, cos, sin, 1024, tm)
    v, = proj(xb, w_in, layer, 2 * D_MODEL, RET_VW, 1024, tm, [BF16], name="proj_rv")
    g, = proj(xb, w_in, layer, 2 * D_MODEL + RET_VW, RET_VW, 1024, tm, [F32], name="proj_rg")
    if pad_to != t_len:
        padr = lambda a: jnp.pad(a.reshape(n_batch, t_len, -1),
                                 ((0, 0), (0, pad_to - t_len), (0, 0))).reshape(n_batch * pad_to, -1)
        z, s_fin = retention(lg, padr(qk), padr(v), padr(g), gn, s0, n_batch, pad_to, chunk)
        z = z.reshape(n_batch, pad_to, RET_VW)[:, :t_len].reshape(m, RET_VW)
    else:
        z, s_fin = retention(lg, qk, v, g, gn, s0, n_batch, chunk, chunk)
    return z, s_fin


MOE_TILE = 512
MOE_GU_TN = D_FF_EXPERT // 2


def _moe_dense(xf, xb, w_router_pad, w_gu, w_down_flat, layer, g, b):
    meta, _ = router(xf, w_router_pad, 512)
    gates_t = jnp.swapaxes(meta[:, :N_EXPERTS], 0, 1)[:, :, None]
    h = moe_gu(xb, w_gu, layer, gates_t, 256, 1024)
    return down_ln(h, w_down_flat, layer, xf, g, b, 512, 512, D_FF_EXPERT)


def _moe_sparse(xf, w_router_pad, w_gu, w_down, layer, g, b):
    m = xf.shape[0]
    meta, cnt = router(xf, w_router_pad, 512)
    col = lambda c: meta[:, c].astype(jnp.int32)
    counts = cnt[0, :N_EXPERTS].astype(jnp.int32)
    n_tile = (counts + MOE_TILE - 1) // MOE_TILE
    end_tile = jnp.cumsum(n_tile)
    start_row = (end_tile - n_tile) * MOE_TILE
    pos1 = start_row[col(META_E1)] + col(META_R1)
    pos2 = start_row[col(META_E2)] + col(META_R2)
    max_tiles = (2 * m + N_EXPERTS * (MOE_TILE - 1)) // MOE_TILE
    tiles = jnp.arange(max_tiles, dtype=jnp.int32)
    tile_expert = jnp.minimum(jnp.sum(tiles[:, None] >= end_tile[None, :], axis=1), N_EXPERTS - 1)
    n_valid = end_tile[-1:].astype(jnp.int32)
    xg = moe_dispatch(pos1, pos2, start_row + counts, end_tile * MOE_TILE, n_valid, xf, MOE_TILE, max_tiles)
    te = tile_expert.astype(jnp.int32)
    hg = moe_gu_grouped(te, n_valid, xg, w_gu, layer, MOE_TILE, MOE_GU_TN)
    yg = moe_down_grouped(te, n_valid, hg, w_down, layer, MOE_TILE, 512)
    return moe_combine_ln(pos1, pos2, yg, meta, xf, g, b, 256)


def kernel(x_prompt, x_sample, cache_k, cache_v, cache_idx_k, state_ret, page_table, rel_bias,
           w_in_attn, w_out_attn, w_in_ret, ret_gn_g, w_out_ret, w_ffn_gu, w_ffn_down,
           w_router, w_exp_gu, w_exp_down, ln_g, ln_b):
    bp, tp, _ = x_prompt.shape
    bs, ts, _ = x_sample.shape
    past = page_table.shape[1] * PAGE_SIZE
    mp, ms = bp * tp, bs * ts
    xp = x_prompt.reshape(mp, D_MODEL)
    xs = x_sample.reshape(ms, D_MODEL)
    xpb, xsb = xp.astype(BF16), xs.astype(BF16)

    rel_tiles = bias_tiles(rel_bias, ATT_T)
    rbt = rel_bias.T
    w_out_attn, w_out_ret, w_ffn_down = (w.astype(BF16) for w in (w_out_attn, w_out_ret, w_ffn_down))
    lg = jnp.log1p(-jnp.exp2(-5.0 - jnp.arange(RET_HEADS, dtype=F32)))
    cos_p, sin_p = _rot_tables(jnp.tile(jnp.arange(tp, dtype=jnp.int32), bp))
    cos_s, sin_s = _rot_tables(jnp.tile(past + jnp.arange(ts, dtype=jnp.int32), bs))
    w_exp_down_flat = w_exp_down.reshape(w_exp_down.shape[0], N_EXPERTS * D_FF_EXPERT, D_MODEL)
    zero_state = jnp.zeros((bp, RET_HEADS, RET_DK, RET_DV), F32)

    kp_l, vp_l, ikp_l, sp_l = [], [], [], []
    ks_l, vs_l, iks_l, ss_l = [], [], [], []
    for i in range(DEPTH):
        j = i // 2
        g0, b0 = ln_g[i, 0][None, :], ln_b[i, 0][None, :]
        g1, b1 = ln_g[i, 1][None, :], ln_b[i, 1][None, :]
        if i % 2 == 0:
            op, kp, vp, ikp = _attn_prompt(xpb, w_in_attn, None, j, rel_tiles, bp)
            os_, k_s, v_s, iks = _attn_sample(xsb, w_in_attn, j, cache_k, cache_v, cache_idx_k,
                                              page_table, rbt, bs)
            kp_l.append(kp); vp_l.append(vp); ikp_l.append(ikp)
            ks_l.append(k_s); vs_l.append(v_s); iks_l.append(iks)
            xp, xpb = down_ln(op, w_out_attn, j, xp, g0, b0, 512, 512, D_MODEL)
            xs, xsb = down_ln(os_, w_out_attn, j, xs, g0, b0, 512, 512, D_MODEL)
        else:
            gn = ret_gn_g[j][None, :]
            zp, sp = _retention_mixer(xpb, w_in_ret, j, gn, zero_state, lg, cos_p, sin_p,
                                      bp, min(RET_CHUNK, tp), tp)
            zs, ss = _retention_mixer(xsb, w_in_ret, j, gn, state_ret[j], lg, cos_s, sin_s,
                                      bs, ts, DEC_PAD)
            sp_l.append(sp); ss_l.append(ss)
            xp, xpb = down_ln(zp, w_out_ret, j, xp, g0, b0, 512, 512, RET_VW // 2)
            xs, xsb = down_ln(zs, w_out_ret, j, xs, g0, b0, 512, 512, RET_VW // 2)
        if i % 2 == 0:
            hp = ffn_gu(xpb, w_ffn_gu, j, 512, 1024)
            hs = ffn_gu(xsb, w_ffn_gu, j, 512, 1024)
            xp, xpb = down_ln(hp, w_ffn_down, j, xp, g1, b1, 512, 512, D_FF // 2)
            xs, xsb = down_ln(hs, w_ffn_down, j, xs, g1, b1, 512, 512, D_FF // 2)
        else:
            w_r = jnp.pad(w_router[j], ((0, 0), (0, LANES - N_EXPERTS)))
            xp, xpb = _moe_sparse(xp, w_r, w_exp_gu, w_exp_down, j, g1, b1)
            xs, xsb = _moe_dense(xs, xsb, w_r, w_exp_gu, w_exp_down_flat, j, g1, b1)

    return (xp.reshape(bp, tp, D_MODEL), xs.reshape(bs, ts, D_MODEL),
            jnp.stack(kp_l), jnp.stack(vp_l), jnp.stack(ikp_l), jnp.stack(sp_l),
            jnp.stack(ks_l), jnp.stack(vs_l), jnp.stack(iks_l), jnp.stack(ss_l))
```

```python
import functools
import math

import jax
import jax.numpy as jnp
import numpy as np
from jax import lax
from jax.experimental import pallas as pl
from jax.experimental.pallas import tpu as pltpu

F32 = jnp.float32
BF16 = jnp.bfloat16

D_MODEL = 2048
PAGE_SIZE = 128
N_HEADS = 16
HEAD_DIM = 128
IDX_HEADS = 16
IDX_DIM = 128
TOPK_MAX = 256
REL_BUCKETS = 32
REL_MAX_DIST = 128
RET_HEADS = 8
RET_DK = 256
RET_DV = 512
RET_VW = RET_HEADS * RET_DV
RET_CHUNK = 128
ROPE_BASE = 10000.0
D_FF = 5632
N_EXPERTS = 8
D_FF_EXPERT = 2816
DEPTH = 4
DN_ALPHA = (2 * DEPTH) ** 0.25
LN_EPS = 1e-5
NEG = -1e30
LOG2E = math.log2(math.e)

LANES = 128
VMEM_LIMIT = 56 * 1024 * 1024
INT_MIN = -2 ** 31
NEG_KEY = int(np.float32(NEG).view(np.int32)) ^ 0x7FFFFFFF


def _cp(n_axes, vmem=VMEM_LIMIT):
    return pltpu.CompilerParams(dimension_semantics=("arbitrary",) * n_axes,
                                vmem_limit_bytes=vmem)


def _silu(x):
    return x * (1.0 / (1.0 + jnp.exp(-x)))


def _dot(a, b):
    return jnp.dot(a, b, preferred_element_type=F32)


def _dot_nt(a, b):
    return lax.dot_general(a, b, (((1,), (1,)), ((), ())), preferred_element_type=F32)


def _dot_tn(a, b):
    return lax.dot_general(a, b, (((0,), (0,)), ((), ())), preferred_element_type=F32)


def _proj_kernel(a_ref, w_ref, *rest, n_out, scale, transposed):
    out_refs = rest[:n_out]
    wb_ref = rest[-1]

    @pl.when(pl.program_id(1) == 0)
    def _():
        wb_ref[...] = w_ref[...].astype(BF16)

    acc = _dot(a_ref[...], wb_ref[...])
    if scale != 1.0:
        acc = acc * scale
    for o in out_refs:
        o[...] = acc.astype(o.dtype)
    if transposed:
        rest[n_out][...] = acc.T.astype(BF16)


def proj(a, w, layer, col0, ncols, tn, tm, out_dtypes, scale=1.0, name="proj", transposed_batches=0):
    m, k = a.shape
    tm = min(tm, m)
    assert col0 % tn == 0 and ncols % tn == 0 and m % tm == 0
    off = col0 // tn
    if w.ndim == 3:
        w_spec = pl.BlockSpec((None, k, tn), lambda j, i: (layer, 0, j + off))
    else:
        w_spec = pl.BlockSpec((k, tn), lambda j, i: (0, j + off))
    out_shape = [jax.ShapeDtypeStruct((m, ncols), dt) for dt in out_dtypes]
    out_specs = [pl.BlockSpec((tm, tn), lambda j, i: (i, j)) for _ in out_dtypes]
    if transposed_batches:
        per = m // transposed_batches // tm
        out_shape.append(jax.ShapeDtypeStruct((transposed_batches, ncols, m // transposed_batches), BF16))
        out_specs.append(pl.BlockSpec((None, tn, tm), lambda j, i: (i // per, j, i % per)))
    outs = pl.pallas_call(
        functools.partial(_proj_kernel, n_out=len(out_dtypes), scale=scale,
                          transposed=bool(transposed_batches)),
        out_shape=out_shape,
        grid=(ncols // tn, m // tm),
        in_specs=[pl.BlockSpec((tm, k), lambda j, i: (i, 0)), w_spec],
        out_specs=out_specs,
        scratch_shapes=[pltpu.VMEM((k, tn), BF16)],
        compiler_params=_cp(2),
        name=name,
    )(a, w)
    return outs


def _proj_rot_kernel(a_ref, w_ref, cos_ref, sin_ref, o_ref, wb_ref, *, heads_per_tile, q_tiles):
    j = pl.program_id(0)

    @pl.when(pl.program_id(1) == 0)
    def _():
        wb_ref[...] = w_ref[...].astype(BF16)

    acc = _dot(a_ref[...], wb_ref[...])
    scale = jnp.where(j >= q_tiles, RET_DK ** -0.5, 1.0).astype(F32)
    c = cos_ref[...]
    s = sin_ref[...]
    half = RET_DK // 2
    for hh in range(heads_per_tile):
        x1 = acc[:, hh * RET_DK: hh * RET_DK + half]
        x2 = acc[:, hh * RET_DK + half: (hh + 1) * RET_DK]
        o_ref[:, hh * RET_DK: hh * RET_DK + half] = ((x1 * c - x2 * s) * scale).astype(BF16)
        o_ref[:, hh * RET_DK + half: (hh + 1) * RET_DK] = ((x1 * s + x2 * c) * scale).astype(BF16)


def proj_rot(a, w, layer, cos, sin, tn, tm):
    m, k = a.shape
    tm = min(tm, m)
    ncols = 2 * D_MODEL
    return pl.pallas_call(
        functools.partial(_proj_rot_kernel, heads_per_tile=tn // RET_DK, q_tiles=D_MODEL // tn),
        out_shape=jax.ShapeDtypeStruct((m, ncols), BF16),
        grid=(ncols // tn, m // tm),
        in_specs=[pl.BlockSpec((tm, k), lambda j, i: (i, 0)),
                  pl.BlockSpec((None, k, tn), lambda j, i: (layer, 0, j)),
                  pl.BlockSpec((tm, RET_DK // 2), lambda j, i: (i, 0)),
                  pl.BlockSpec((tm, RET_DK // 2), lambda j, i: (i, 0))],
        out_specs=pl.BlockSpec((tm, tn), lambda j, i: (i, j)),
        scratch_shapes=[pltpu.VMEM((k, tn), BF16)],
        compiler_params=_cp(2),
        name="proj_rot",
    )(a, w, cos, sin)


def _gu_kernel(a_ref, wg_ref, wu_ref, *rest, gated, inner_axis):
    if gated:
        gate_ref, o_ref, wgb_ref, wub_ref = rest
    else:
        o_ref, wgb_ref, wub_ref = rest

    @pl.when(pl.program_id(inner_axis) == 0)
    def _():
        wgb_ref[...] = wg_ref[...].astype(BF16)
        wub_ref[...] = wu_ref[...].astype(BF16)

    a = a_ref[...]
    h = _silu(_dot(a, wgb_ref[...])) * _dot(a, wub_ref[...])
    if gated:
        h = h * gate_ref[...]
    o_ref[...] = h.astype(BF16)


def ffn_gu(a, w_gu, layer, tn, tm):
    m, k = a.shape
    tm = min(tm, m)
    nj = D_FF // tn
    return pl.pallas_call(
        functools.partial(_gu_kernel, gated=False, inner_axis=1),
        out_shape=jax.ShapeDtypeStruct((m, D_FF), BF16),
        grid=(nj, m // tm),
        in_specs=[pl.BlockSpec((tm, k), lambda j, i: (i, 0)),
                  pl.BlockSpec((None, k, tn), lambda j, i: (layer, 0, j)),
                  pl.BlockSpec((None, k, tn), lambda j, i: (layer, 0, j + nj))],
        out_specs=pl.BlockSpec((tm, tn), lambda j, i: (i, j)),
        scratch_shapes=[pltpu.VMEM((k, tn), BF16), pltpu.VMEM((k, tn), BF16)],
        compiler_params=_cp(2),
        name="ffn_gu",
    )(a, w_gu, w_gu)


def moe_gu(a, w_gu, layer, gates_t, tn, tm):
    m, k = a.shape
    tm = min(tm, m)
    nj = D_FF_EXPERT // tn
    return pl.pallas_call(
        functools.partial(_gu_kernel, gated=True, inner_axis=2),
        out_shape=jax.ShapeDtypeStruct((m, N_EXPERTS * D_FF_EXPERT), BF16),
        grid=(N_EXPERTS, nj, m // tm),
        in_specs=[pl.BlockSpec((tm, k), lambda e, j, i: (i, 0)),
                  pl.BlockSpec((None, None, k, tn), lambda e, j, i: (layer, e, 0, j)),
                  pl.BlockSpec((None, None, k, tn), lambda e, j, i: (layer, e, 0, j + nj)),
                  pl.BlockSpec((None, tm, 1), lambda e, j, i: (e, i, 0))],
        out_specs=pl.BlockSpec((tm, tn), lambda e, j, i: (i, e * nj + j)),
        scratch_shapes=[pltpu.VMEM((k, tn), BF16), pltpu.VMEM((k, tn), BF16)],
        compiler_params=_cp(3),
        name="moe_gu",
    )(a, w_gu, w_gu, gates_t)


def _down_ln_kernel(a_ref, w_ref, x_ref, g_ref, b_ref, of_ref, ob_ref, row_ref, *, nj, nk, tn):
    k = pl.program_id(1)
    j = pl.program_id(2)
    y = _dot(a_ref[...], w_ref[...].astype(BF16))

    @pl.when(k == 0)
    def _():
        row_ref[j] = y

    @pl.when(k > 0)
    def _():
        row_ref[j] = row_ref[j] + y

    @pl.when((j == nj - 1) & (k == nk - 1))
    def _():
        zs = [DN_ALPHA * x_ref[:, jj * tn:(jj + 1) * tn] + row_ref[jj] for jj in range(nj)]
        tot = zs[0].sum(axis=-1, keepdims=True)
        for z in zs[1:]:
            tot = tot + z.sum(axis=-1, keepdims=True)
        mu = tot * (1.0 / D_MODEL)
        sq = None
        for z in zs:
            t = jnp.square(z - mu).sum(axis=-1, keepdims=True)
            sq = t if sq is None else sq + t
        rstd = lax.rsqrt(sq * (1.0 / D_MODEL) + LN_EPS)
        for jj, z in enumerate(zs):
            sl = slice(jj * tn, (jj + 1) * tn)
            o = (z - mu) * rstd * g_ref[:, sl] + b_ref[:, sl]
            of_ref[:, sl] = o
            ob_ref[:, sl] = o.astype(BF16)


def down_ln(a, w, layer, x, g, b, tm, tn, tk):
    m, kk = a.shape
    tm = min(tm, m)
    n = D_MODEL
    assert kk % tk == 0 and n % tn == 0 and m % tm == 0
    nj, nk = n // tn, kk // tk
    return pl.pallas_call(
        functools.partial(_down_ln_kernel, nj=nj, nk=nk, tn=tn),
        out_shape=[jax.ShapeDtypeStruct((m, n), F32), jax.ShapeDtypeStruct((m, n), BF16)],
        grid=(m // tm, nk, nj),
        in_specs=[pl.BlockSpec((tm, tk), lambda i, k, j: (i, k)),
                  pl.BlockSpec((None, tk, tn), lambda i, k, j: (layer, k, j)),
                  pl.BlockSpec((tm, n), lambda i, k, j: (i, 0)),
                  pl.BlockSpec((1, n), lambda i, k, j: (0, 0)),
                  pl.BlockSpec((1, n), lambda i, k, j: (0, 0))],
        out_specs=[pl.BlockSpec((tm, n), lambda i, k, j: (i, 0)),
                   pl.BlockSpec((tm, n), lambda i, k, j: (i, 0))],
        scratch_shapes=[pltpu.VMEM((nj, tm, tn), F32)],
        compiler_params=_cp(3),
        name="down_ln",
    )(a, w, x, g, b)


def _router_kernel(x_ref, w_ref, o_ref, cnt_ref, carry_ref):
    x = x_ref[...]
    w = w_ref[...]
    xh = x.astype(BF16)
    xl = (x - xh.astype(F32)).astype(BF16)
    wh = w.astype(BF16)
    wl = (w - wh.astype(F32)).astype(BF16)
    lg = _dot(xh, wh) + (_dot(xh, wl) + _dot(xl, wh))
    lane = lax.broadcasted_iota(jnp.int32, lg.shape, 1)
    lg = jnp.where(lane < N_EXPERTS, lg, -jnp.inf)
    m1 = lg.max(axis=-1, keepdims=True)
    i1 = jnp.where(lg == m1, lane, LANES).min(axis=-1, keepdims=True)
    lg2 = jnp.where(lane == i1, -jnp.inf, lg)
    m2 = lg2.max(axis=-1, keepdims=True)
    i2 = jnp.where(lg2 == m2, lane, LANES).min(axis=-1, keepdims=True)
    e2 = jnp.exp(m2 - m1)
    den = 1.0 + e2
    g1 = 1.0 / den
    g2 = e2 / den
    @pl.when(pl.program_id(0) == 0)
    def _():
        carry_ref[...] = jnp.zeros(carry_ref.shape, F32)

    tm = lg.shape[0]
    hit = jnp.where(lane == i1, 1.0, jnp.where(lane == i2, 1.0, 0.0))
    rr = lax.broadcasted_iota(jnp.int32, (tm, tm), 0)
    cc = lax.broadcasted_iota(jnp.int32, (tm, tm), 1)
    tri = jnp.where(cc < rr, 1.0, 0.0).astype(BF16)
    before = _dot(tri, hit.astype(BF16)) + carry_ref[...]
    r1 = jnp.where(lane == i1, before, 0.0).sum(axis=-1, keepdims=True)
    r2 = jnp.where(lane == i2, before, 0.0).sum(axis=-1, keepdims=True)
    carry_ref[...] = carry_ref[...] + hit.sum(axis=0, keepdims=True)
    cnt_ref[...] = carry_ref[...]
    meta = jnp.where(lane == i1, g1, 0.0) + jnp.where(lane == i2, g2, 0.0)
    for col, val in ((META_E1, i1.astype(F32)), (META_E2, i2.astype(F32)), (META_G1, g1),
                     (META_G2, g2), (META_R1, r1), (META_R2, r2)):
        meta = jnp.where(lane == col, val, meta)
    o_ref[...] = meta


META_E1, META_E2, META_G1, META_G2, META_R1, META_R2 = 8, 9, 10, 11, 12, 13


def router(x, w_pad, tm):
    m, k = x.shape
    tm = min(tm, m)
    return pl.pallas_call(
        _router_kernel,
        out_shape=[jax.ShapeDtypeStruct((m, LANES), F32), jax.ShapeDtypeStruct((1, LANES), F32)],
        grid=(m // tm,),
        in_specs=[pl.BlockSpec((tm, k), lambda i: (i, 0)),
                  pl.BlockSpec((k, LANES), lambda i: (0, 0))],
        out_specs=[pl.BlockSpec((tm, LANES), lambda i: (i, 0)),
                   pl.BlockSpec((1, LANES), lambda i: (0, 0))],
        scratch_shapes=[pltpu.VMEM((1, LANES), F32)],
        compiler_params=_cp(1),
        name="router",
    )(x, w_pad)


def _dispatch_kernel(p1_ref, p2_ref, lo_ref, hi_ref, nv_ref, x_hbm, o_ref, src_ref, buf_ref, sem_ref,
                     *, m, tile, n_tiles):
    t = pl.program_id(0)
    nv = nv_ref[0]

    @pl.when(t == 0)
    def _():
        for e in range(N_EXPERTS):
            def clear(i, c):
                src_ref[i] = 0
                return c
            lax.fori_loop(lo_ref[e], hi_ref[e], clear, 0)

        def invert(n, c):
            src_ref[p1_ref[n]] = n
            src_ref[p2_ref[n]] = n
            return c
        lax.fori_loop(0, m, invert, 0, unroll=8)

    def issue(tt, slot):
        def body(i, c):
            tok = src_ref[tt * tile + i]
            pltpu.make_async_copy(x_hbm.at[pl.ds(tok, 1)], buf_ref.at[slot, pl.ds(i, 1)],
                                  sem_ref.at[slot]).start()
            return c
        lax.fori_loop(0, tile, body, 0, unroll=8)

    @pl.when(t == 0)
    def _():
        issue(0, 0)

    @pl.when(t + 1 < nv)
    def _():
        issue(t + 1, (t + 1) % 2)

    slot = t % 2

    @pl.when(t < nv)
    def _():
        pltpu.make_async_copy(buf_ref.at[slot], buf_ref.at[slot], sem_ref.at[slot]).wait()
        o_ref[...] = buf_ref[slot].astype(BF16)

    @pl.when(t >= nv)
    def _():
        o_ref[...] = jnp.zeros(o_ref.shape, o_ref.dtype)


def moe_dispatch(pos1, pos2, pad_lo, pad_hi, n_valid, x, tile, n_tiles):
    m, d = x.shape
    return pl.pallas_call(
        functools.partial(_dispatch_kernel, m=m, tile=tile, n_tiles=n_tiles),
        out_shape=jax.ShapeDtypeStruct((n_tiles * tile, d), BF16),
        grid_spec=pltpu.PrefetchScalarGridSpec(
            num_scalar_prefetch=5,
            grid=(n_tiles,),
            in_specs=[pl.BlockSpec(memory_space=pl.ANY)],
            out_specs=pl.BlockSpec((tile, d), lambda t, *_: (t, 0)),
            scratch_shapes=[pltpu.SMEM((n_tiles * tile,), jnp.int32),
                            pltpu.VMEM((2, tile, d), F32),
                            pltpu.SemaphoreType.DMA((2,))]),
        compiler_params=_cp(1),
        name="moe_dispatch",
    )(pos1, pos2, pad_lo, pad_hi, n_valid, x)


def _tile_ids(t, te_ref, nv_ref):
    nv = nv_ref[0]
    cur = te_ref[jnp.minimum(t, nv - 1)]
    prev = te_ref[jnp.minimum(jnp.maximum(t - 1, 0), nv - 1)]
    return nv, (t == 0) | (cur != prev)


def _ggu_kernel(te_ref, nv_ref, a_ref, wg_ref, wu_ref, o_ref, wgb_ref, wub_ref):
    t = pl.program_id(1)
    nv, new_expert = _tile_ids(t, te_ref, nv_ref)

    @pl.when(new_expert)
    def _():
        wgb_ref[...] = wg_ref[...].astype(BF16)
        wub_ref[...] = wu_ref[...].astype(BF16)

    @pl.when(t < nv)
    def _():
        a = a_ref[...]
        o_ref[...] = (_silu(_dot(a, wgb_ref[...])) * _dot(a, wub_ref[...])).astype(BF16)

    @pl.when(t >= nv)
    def _():
        o_ref[...] = jnp.zeros(o_ref.shape, o_ref.dtype)


def moe_gu_grouped(tile_expert, n_valid, a, w_gu, layer, tile, tn):
    r, k = a.shape
    nj = D_FF_EXPERT // tn
    row = lambda t, nv: jnp.minimum(t, nv[0] - 1)
    return pl.pallas_call(
        _ggu_kernel,
        out_shape=jax.ShapeDtypeStruct((r, D_FF_EXPERT), BF16),
        grid_spec=pltpu.PrefetchScalarGridSpec(
            num_scalar_prefetch=2,
            grid=(nj, r // tile),
            in_specs=[pl.BlockSpec((tile, k), lambda j, t, te, nv: (row(t, nv), 0)),
                      pl.BlockSpec((None, None, k, tn),
                                   lambda j, t, te, nv: (layer, te[row(t, nv)], 0, j),
                                   pipeline_mode=pl.Buffered(1)),
                      pl.BlockSpec((None, None, k, tn),
                                   lambda j, t, te, nv: (layer, te[row(t, nv)], 0, j + nj),
                                   pipeline_mode=pl.Buffered(1))],
            out_specs=pl.BlockSpec((tile, tn), lambda j, t, te, nv: (t, j)),
            scratch_shapes=[pltpu.VMEM((k, tn), BF16), pltpu.VMEM((k, tn), BF16)]),
        compiler_params=_cp(2),
        name="moe_gu_grouped",
    )(tile_expert, n_valid, a, w_gu, w_gu)


def _gdown_kernel(te_ref, nv_ref, a_ref, w_ref, o_ref, wb_ref):
    t = pl.program_id(1)
    nv, new_expert = _tile_ids(t, te_ref, nv_ref)

    @pl.when(new_expert)
    def _():
        wb_ref[...] = w_ref[...].astype(BF16)

    @pl.when(t < nv)
    def _():
        o_ref[...] = _dot(a_ref[...], wb_ref[...])

    @pl.when(t >= nv)
    def _():
        o_ref[...] = jnp.zeros(o_ref.shape, o_ref.dtype)


def moe_down_grouped(tile_expert, n_valid, a, w_down, layer, tile, tn):
    r, k = a.shape
    row = lambda t, nv: jnp.minimum(t, nv[0] - 1)
    return pl.pallas_call(
        _gdown_kernel,
        out_shape=jax.ShapeDtypeStruct((r, D_MODEL), F32),
        grid_spec=pltpu.PrefetchScalarGridSpec(
            num_scalar_prefetch=2,
            grid=(D_MODEL // tn, r // tile),
            in_specs=[pl.BlockSpec((tile, k), lambda j, t, te, nv: (row(t, nv), 0)),
                      pl.BlockSpec((None, None, k, tn),
                                   lambda j, t, te, nv: (layer, te[row(t, nv)], 0, j),
                                   pipeline_mode=pl.Buffered(1))],
            out_specs=pl.BlockSpec((tile, tn), lambda j, t, te, nv: (t, j)),
            scratch_shapes=[pltpu.VMEM((k, tn), BF16)]),
        compiler_params=_cp(2),
        name="moe_down_grouped",
    )(tile_expert, n_valid, a, w_down)


def _combine_kernel(p1_ref, p2_ref, y_hbm, meta_ref, x_ref, g_ref, b_ref, of_ref, ob_ref,
                    ybuf_ref, sem_ref, *, tm, n_tiles):
    i = pl.program_id(0)

    def issue(ii, slot):
        def body(r, c):
            n = ii * tm + r
            pltpu.make_async_copy(y_hbm.at[pl.ds(p1_ref[n], 1)], ybuf_ref.at[slot, 0, pl.ds(r, 1)],
                                  sem_ref.at[slot]).start()
            pltpu.make_async_copy(y_hbm.at[pl.ds(p2_ref[n], 1)], ybuf_ref.at[slot, 1, pl.ds(r, 1)],
                                  sem_ref.at[slot]).start()
            return c
        lax.fori_loop(0, tm, body, 0)

    @pl.when(i == 0)
    def _():
        issue(0, 0)

    @pl.when(i + 1 < n_tiles)
    def _():
        issue(i + 1, (i + 1) % 2)

    slot = i % 2
    pltpu.make_async_copy(ybuf_ref.at[slot], ybuf_ref.at[slot], sem_ref.at[slot]).wait()
    meta = meta_ref[...]
    lane = lax.broadcasted_iota(jnp.int32, meta.shape, 1)
    g1 = jnp.where(lane == META_G1, meta, 0.0).sum(axis=-1, keepdims=True)
    g2 = jnp.where(lane == META_G2, meta, 0.0).sum(axis=-1, keepdims=True)
    z = DN_ALPHA * x_ref[...] + (g1 * ybuf_ref[slot, 0] + g2 * ybuf_ref[slot, 1])
    mu = z.mean(axis=-1, keepdims=True)
    var = jnp.square(z - mu).mean(axis=-1, keepdims=True)
    o = (z - mu) * lax.rsqrt(var + LN_EPS) * g_ref[...] + b_ref[...]
    of_ref[...] = o
    ob_ref[...] = o.astype(BF16)


def moe_combine_ln(pos1, pos2, y, meta, x, g, b, tm):
    m, d = x.shape
    n_tiles = m // tm
    tok = lambda i, p1, p2: (i, 0)
    vec = lambda i, p1, p2: (0, 0)
    return pl.pallas_call(
        functools.partial(_combine_kernel, tm=tm, n_tiles=n_tiles),
        out_shape=[jax.ShapeDtypeStruct((m, d), F32), jax.ShapeDtypeStruct((m, d), BF16)],
        grid_spec=pltpu.PrefetchScalarGridSpec(
            num_scalar_prefetch=2,
            grid=(n_tiles,),
            in_specs=[pl.BlockSpec(memory_space=pl.ANY),
                      pl.BlockSpec((tm, LANES), tok),
                      pl.BlockSpec((tm, d), tok),
                      pl.BlockSpec((1, d), vec),
                      pl.BlockSpec((1, d), vec)],
            out_specs=[pl.BlockSpec((tm, d), tok), pl.BlockSpec((tm, d), tok)],
            scratch_shapes=[pltpu.VMEM((2, 2, tm, d), F32),
                            pltpu.SemaphoreType.DMA((2,))]),
        compiler_params=_cp(1),
        name="moe_combine_ln",
    )(pos1, pos2, y, meta, x, g, b)


def _sort_key(x):
    k = pltpu.bitcast(x, jnp.int32)
    return jnp.where(k < 0, k ^ jnp.int32(0x7FFFFFFF), k)


def _kth_largest_key(count_ge, n_sel, shape):
    zero = jnp.zeros(shape, jnp.int32)
    t0 = jnp.where(count_ge(zero) >= n_sel, zero, jnp.full(shape, INT_MIN, jnp.int32))

    def body(i, t):
        cand = t | jnp.left_shift(jnp.int32(1), 30 - i)
        return jnp.where(count_ge(cand) >= n_sel, cand, t)

    return lax.fori_loop(0, 31, body, t0)


IDX_GROUPS = 4


def _idx_mask_kernel(iq_ref, iw_ref, ikt_ref, o_ref, key_ref, *, tq, n_sel):
    i = pl.program_id(1)
    s_len = ikt_ref.shape[1]
    per = s_len // tq // IDX_GROUPS
    wf = iw_ref[...] * (IDX_DIM ** -0.5)

    def run(s_eff):
        ikt = ikt_ref[:, :s_eff]
        sc = None
        for h in range(IDX_HEADS):
            d = _dot(iq_ref[:, h * IDX_DIM:(h + 1) * IDX_DIM], ikt)
            t = jnp.maximum(d, 0.0) * wf[:, h:h + 1]
            sc = t if sc is None else sc + t
        qpos = i * tq + lax.broadcasted_iota(jnp.int32, (tq, s_eff), 0)
        kpos = lax.broadcasted_iota(jnp.int32, (tq, s_eff), 1)
        valid = kpos <= qpos
        key_ref[:, :s_eff] = _sort_key(jnp.where(valid, sc, NEG))
        n_tail = float(s_len - s_eff)

        def count_ge(cand):
            c = jnp.where(key_ref[:, :s_eff] >= cand, 1.0, 0.0).sum(axis=-1, keepdims=True)
            return c + jnp.where(cand <= NEG_KEY, n_tail, 0.0)

        thr = _kth_largest_key(count_ge, float(n_sel), (tq, 1))
        sel = (key_ref[:, :s_eff] >= thr) & valid
        o_ref[:, :s_eff] = jnp.where(sel, 0.0, NEG).astype(o_ref.dtype)
        if s_eff < s_len:
            o_ref[:, s_eff:] = jnp.full((tq, s_len - s_eff), NEG, o_ref.dtype)

    for g in range(IDX_GROUPS):
        pl.when(i // per == g)(functools.partial(run, (g + 1) * per * tq))


def idx_mask(iq, iw, ikt, n_batch, tq):
    m = iq.shape[0]
    s_len = ikt.shape[2]
    nq = s_len // tq
    assert nq % IDX_GROUPS == 0
    n_sel = min(TOPK_MAX, s_len // 4)
    return pl.pallas_call(
        functools.partial(_idx_mask_kernel, tq=tq, n_sel=n_sel),
        out_shape=jax.ShapeDtypeStruct((n_batch, s_len, s_len), BF16),
        grid=(n_batch, nq),
        in_specs=[pl.BlockSpec((tq, IDX_HEADS * IDX_DIM), lambda b, i: (b * nq + i, 0)),
                  pl.BlockSpec((tq, IDX_HEADS), lambda b, i: (b * nq + i, 0)),
                  pl.BlockSpec((None, IDX_DIM, s_len), lambda b, i: (b, 0, 0))],
        out_specs=pl.BlockSpec((None, tq, s_len), lambda b, i: (b, i, 0)),
        scratch_shapes=[pltpu.VMEM((tq, s_len), jnp.int32)],
        compiler_params=_cp(2),
        name="idx_mask",
    )(iq, iw, ikt)


def _rel_bucket(dist):
    max_exact = REL_BUCKETS // 2
    large = max_exact + (jnp.log(jnp.maximum(dist, 1).astype(F32) / max_exact)
                         / math.log(REL_MAX_DIST / max_exact) * (REL_BUCKETS - max_exact)).astype(jnp.int32)
    large = jnp.minimum(large, REL_BUCKETS - 1)
    return jnp.where(dist < max_exact, dist, large)


def _bias_tiles_kernel(rb_ref, o_ref, *, t):
    d = pl.program_id(0)
    ii = lax.broadcasted_iota(jnp.int32, (t, t), 0)
    jj = lax.broadcasted_iota(jnp.int32, (t, t), 1)
    bucket = _rel_bucket(jnp.maximum(d * t + ii - jj, 0))
    for h in range(N_HEADS):
        acc = jnp.zeros((t, t), F32)
        for bkt in range(REL_BUCKETS):
            acc = jnp.where(bucket == bkt, rb_ref[bkt, h] * LOG2E, acc)
        o_ref[h] = acc


def bias_tiles(rel_bias, t):
    assert 2 * t - (t - 1) >= REL_MAX_DIST
    return pl.pallas_call(
        functools.partial(_bias_tiles_kernel, t=t),
        out_shape=jax.ShapeDtypeStruct((3, N_HEADS, t, t), F32),
        grid=(3,),
        in_specs=[pl.BlockSpec(memory_space=pltpu.SMEM)],
        out_specs=pl.BlockSpec((None, N_HEADS, t, t), lambda d: (d, 0, 0, 0)),
        compiler_params=_cp(1),
        name="bias_tiles",
    )(rel_bias)


def _attn_kernel(it_ref, jt_ref, q_ref, kt_ref, v_ref, mask_ref, tz_ref, o_ref, m_ref, acc_ref):
    step = pl.program_id(1)
    i = it_ref[step]
    j = jt_ref[step]
    t = q_ref.shape[0]

    @pl.when(j == 0)
    def _():
        m_ref[...] = jnp.full(m_ref.shape, -jnp.inf, F32)
        acc_ref[...] = jnp.zeros(acc_ref.shape, F32)

    maskf = mask_ref[...].astype(F32)
    ones = jnp.ones((t, HEAD_DIM), BF16)
    twice = lambda a: jnp.concatenate([a] * (t // LANES), axis=1)
    for h in range(N_HEADS):
        sl = slice(h * HEAD_DIM, (h + 1) * HEAD_DIM)
        s = _dot(q_ref[:, sl], kt_ref[sl, :]) + tz_ref[h] + maskf
        m_prev = m_ref[h]
        m_new = jnp.maximum(m_prev, s.max(axis=-1, keepdims=True))
        alpha = jnp.exp2(m_prev - m_new)
        p = jnp.exp2(s - twice(m_new))
        pv = _dot(p.astype(BF16), jnp.concatenate([v_ref[:, sl], ones], axis=1))
        acc_ref[h] = jnp.concatenate([alpha, alpha], axis=1) * acc_ref[h] + pv
        m_ref[h] = m_new

    @pl.when(j == i)
    def _():
        for h in range(N_HEADS):
            a = acc_ref[h]
            o_ref[:, h * HEAD_DIM:(h + 1) * HEAD_DIM] = (a[:, :HEAD_DIM] / a[:, HEAD_DIM:]).astype(o_ref.dtype)


def attn(q, kt, v, mask, tz, n_batch, t):
    m = q.shape[0]
    nq = m // n_batch // t
    pairs = [(i, j) for i in range(nq) for j in range(i + 1)]
    it = jnp.asarray([p[0] for p in pairs], jnp.int32)
    jt = jnp.asarray([p[1] for p in pairs], jnp.int32)
    return pl.pallas_call(
        _attn_kernel,
        out_shape=jax.ShapeDtypeStruct((m, D_MODEL), BF16),
        grid_spec=pltpu.PrefetchScalarGridSpec(
            num_scalar_prefetch=2,
            grid=(n_batch, len(pairs)),
            in_specs=[pl.BlockSpec((t, D_MODEL), lambda b, s, it, jt: (b * nq + it[s], 0)),
                      pl.BlockSpec((None, D_MODEL, t), lambda b, s, it, jt: (b, 0, jt[s])),
                      pl.BlockSpec((t, D_MODEL), lambda b, s, it, jt: (b * nq + jt[s], 0)),
                      pl.BlockSpec((None, t, t), lambda b, s, it, jt: (b, it[s], jt[s])),
                      pl.BlockSpec((None, N_HEADS, t, t),
                                   lambda b, s, it, jt: (jnp.minimum(it[s] - jt[s], 2), 0, 0, 0))],
            out_specs=pl.BlockSpec((t, D_MODEL), lambda b, s, it, jt: (b * nq + it[s], 0)),
            scratch_shapes=[pltpu.VMEM((N_HEADS, t, LANES), F32),
                            pltpu.VMEM((N_HEADS, t, 2 * HEAD_DIM), F32)]),
        compiler_params=_cp(2),
        name="attn",
    )(it, jt, q, kt, v, mask, tz)


def _ret_kernel(lg_ref, q_ref, k_ref, v_ref, g_ref, gn_ref, s0_ref, z_ref, sout_ref, state_ref,
                *, c_pad, c_true, nc):
    c = pl.program_id(1)

    @pl.when(c == 0)
    def _():
        state_ref[...] = s0_ref[...]

    ii = lax.broadcasted_iota(jnp.int32, (c_pad, c_pad), 0)
    jj = lax.broadcasted_iota(jnp.int32, (c_pad, c_pad), 1)
    rel = (ii - jj).astype(F32)
    idx = lax.broadcasted_iota(jnp.int32, (c_pad, 1), 0).astype(F32)
    for h in range(RET_HEADS):
        lg = lg_ref[h]
        qs = slice(h * RET_DK, (h + 1) * RET_DK)
        vs = slice(h * RET_DV, (h + 1) * RET_DV)
        q = q_ref[:, qs]
        k = k_ref[:, qs]
        v = v_ref[:, vs]
        decay = jnp.where(rel >= 0, jnp.exp(lg * jnp.maximum(rel, 0.0)), 0.0)
        scores = _dot_nt(q, k) * decay
        inner = _dot(scores.astype(BF16), v)
        s_prev = state_ref[h]
        cross = _dot(q, s_prev.astype(BF16)) * jnp.exp(lg * (idx + 1.0))
        o = inner + cross
        w_state = jnp.exp(lg * (c_true - 1.0 - idx))
        kw = (k.astype(F32) * w_state).astype(BF16)
        state_ref[h] = s_prev * jnp.exp(lg * c_true) + _dot_tn(kw, v)

        mu = o.mean(axis=-1, keepdims=True)
        var = jnp.square(o - mu).mean(axis=-1, keepdims=True)
        on = (o - mu) * lax.rsqrt(var + LN_EPS) * gn_ref[:, vs]
        z_ref[:, vs] = (_silu(g_ref[:, vs]) * on).astype(BF16)

    @pl.when(c == nc - 1)
    def _():
        sout_ref[...] = state_ref[...]


def retention(lg, qk, v, g, gn, s0, n_batch, c_pad, c_true):
    m = qk.shape[0]
    nc = m // n_batch // c_pad
    row = lambda b, c: b * nc + c
    state = (None, RET_HEADS, RET_DK, RET_DV)
    return pl.pallas_call(
        functools.partial(_ret_kernel, c_pad=c_pad, c_true=float(c_true), nc=nc),
        out_shape=[jax.ShapeDtypeStruct((m, RET_VW), BF16),
                   jax.ShapeDtypeStruct((n_batch, RET_HEADS, RET_DK, RET_DV), F32)],
        grid=(n_batch, nc),
        in_specs=[pl.BlockSpec(memory_space=pltpu.SMEM),
                  pl.BlockSpec((c_pad, D_MODEL), lambda b, c: (row(b, c), 0)),
                  pl.BlockSpec((c_pad, D_MODEL), lambda b, c: (row(b, c), 1)),
                  pl.BlockSpec((c_pad, RET_VW), lambda b, c: (row(b, c), 0)),
                  pl.BlockSpec((c_pad, RET_VW), lambda b, c: (row(b, c), 0)),
                  pl.BlockSpec((1, RET_VW), lambda b, c: (0, 0)),
                  pl.BlockSpec(state, lambda b, c: (b, 0, 0, 0))],
        out_specs=[pl.BlockSpec((c_pad, RET_VW), lambda b, c: (row(b, c), 0)),
                   pl.BlockSpec(state, lambda b, c: (b, 0, 0, 0))],
        scratch_shapes=[pltpu.VMEM((RET_HEADS, RET_DK, RET_DV), F32)],
        compiler_params=_cp(2),
        name="retention",
    )(lg, qk, qk, v, g, gn, s0)


DEC_CHUNK = 2048


def _dec_scores_kernel(pt_ref, ik_hbm, iq_ref, iw_ref, ikn_ref, o_ref, buf_ref, sem_ref,
                       *, layer, t_len, n_pages, n_batch):
    b = pl.program_id(0)
    past = n_pages * PAGE_SIZE

    def issue(bb, slot):
        def body(p, c):
            pltpu.make_async_copy(ik_hbm.at[layer, pt_ref[bb, p]],
                                  buf_ref.at[slot, pl.ds(p * PAGE_SIZE, PAGE_SIZE)],
                                  sem_ref.at[slot]).start()
            return c
        lax.fori_loop(0, n_pages, body, 0)

    @pl.when(b == 0)
    def _():
        issue(0, 0)

    @pl.when(b + 1 < n_batch)
    def _():
        issue(b + 1, (b + 1) % 2)

    slot = b % 2
    pltpu.make_async_copy(buf_ref.at[slot], buf_ref.at[slot], sem_ref.at[slot]).wait()
    iq = iq_ref[...]
    wf = iw_ref[...] * (IDX_DIM ** -0.5)

    def head_sum(keys_bf16):
        r = jnp.maximum(_dot_nt(iq, keys_bf16), 0.0) * wf
        return r.reshape(t_len, IDX_HEADS, r.shape[-1]).sum(axis=1)

    chunk = min(DEC_CHUNK, past)
    for c in range(past // chunk):
        sl = slice(c * chunk, (c + 1) * chunk)
        o_ref[:, sl] = head_sum(buf_ref[slot, sl, :].astype(BF16))
    sn = head_sum(ikn_ref[...])
    qi = lax.broadcasted_iota(jnp.int32, sn.shape, 0)
    si = lax.broadcasted_iota(jnp.int32, sn.shape, 1)
    o_ref[:, past:] = jnp.where(si < t_len, jnp.where(si <= qi, sn, NEG), -jnp.inf)


def dec_scores(page_table, cache_idx_k, layer, iq, iw, ik_new, t_len):
    n_batch, n_pages = page_table.shape
    past = n_pages * PAGE_SIZE
    assert past % min(DEC_CHUNK, past) == 0
    rows = t_len * IDX_HEADS
    return pl.pallas_call(
        functools.partial(_dec_scores_kernel, layer=layer, t_len=t_len, n_pages=n_pages,
                          n_batch=n_batch),
        out_shape=jax.ShapeDtypeStruct((n_batch, t_len, past + LANES), F32),
        grid_spec=pltpu.PrefetchScalarGridSpec(
            num_scalar_prefetch=1,
            grid=(n_batch,),
            in_specs=[pl.BlockSpec(memory_space=pl.ANY),
                      pl.BlockSpec((None, rows, IDX_DIM), lambda b, pt: (b, 0, 0)),
                      pl.BlockSpec((None, rows, 1), lambda b, pt: (b, 0, 0)),
                      pl.BlockSpec((None, LANES, IDX_DIM), lambda b, pt: (b, 0, 0))],
            out_specs=pl.BlockSpec((None, t_len, past + LANES), lambda b, pt: (b, 0, 0)),
            scratch_shapes=[pltpu.VMEM((2, past, IDX_DIM), F32),
                            pltpu.SemaphoreType.DMA((2,))]),
        compiler_params=_cp(1),
        name="dec_scores",
    )(page_table, cache_idx_k, iq, iw, ik_new)


def _dec_topk_kernel(s_ref, o_ref, x_ref, *, n_sel):
    x_ref[...] = s_ref[...]
    blk = lax.broadcasted_iota(jnp.int32, x_ref.shape, 0)
    lane = lax.broadcasted_iota(jnp.int32, x_ref.shape, 2)
    key_id = blk * LANES + lane
    big = jnp.int32(2 ** 30)

    def body(j, c):
        x = x_ref[...]
        m = x.max(axis=0).max(axis=-1, keepdims=True)
        pick = jnp.where(x == m[None], key_id, big).min(axis=0).min(axis=-1, keepdims=True)
        x_ref[...] = jnp.where(key_id == pick[None], -jnp.inf, x)
        o_ref[j] = pick
        return c

    lax.fori_loop(0, n_sel, body, 0)


def dec_topk(s_blocks, n_sel):
    nb, rows, _ = s_blocks.shape
    return pl.pallas_call(
        functools.partial(_dec_topk_kernel, n_sel=n_sel),
        out_shape=jax.ShapeDtypeStruct((n_sel, rows, 1), jnp.int32),
        grid=(1,),
        in_specs=[pl.BlockSpec((nb, rows, LANES), lambda i: (0, 0, 0))],
        out_specs=pl.BlockSpec((n_sel, rows, 1), lambda i: (0, 0, 0)),
        scratch_shapes=[pltpu.VMEM((nb, rows, LANES), F32)],
        compiler_params=_cp(1),
        name="dec_topk",
    )(s_blocks)


def _dec_attn_kernel(idx_ref, pt_ref, ck_hbm, cv_hbm, kn_hbm, vn_hbm, q_ref, idxv_ref, rbt_ref, o_ref,
                     kbuf_ref, vbuf_ref, sem_ref, *, layer, t_len, n_pages, n_q, n_sel):
    r = pl.program_id(0)
    past = n_pages * PAGE_SIZE

    def issue(rr, slot):
        bb = rr // t_len

        def body(j, c):
            key = idx_ref[rr * n_sel + j]

            @pl.when(key < past)
            def _():
                phys = pt_ref[bb, key // PAGE_SIZE]
                row = key % PAGE_SIZE
                pltpu.make_async_copy(ck_hbm.at[layer, phys, row], kbuf_ref.at[slot, j],
                                      sem_ref.at[0, slot]).start()
                pltpu.make_async_copy(cv_hbm.at[layer, phys, row], vbuf_ref.at[slot, j],
                                      sem_ref.at[1, slot]).start()

            @pl.when(key >= past)
            def _():
                row = bb * t_len + jnp.minimum(key - past, t_len - 1)
                pltpu.make_async_copy(kn_hbm.at[row], kbuf_ref.at[slot, j], sem_ref.at[0, slot]).start()
                pltpu.make_async_copy(vn_hbm.at[row], vbuf_ref.at[slot, j], sem_ref.at[1, slot]).start()

            return c
        lax.fori_loop(0, n_sel, body, 0, unroll=4)

    @pl.when(r == 0)
    def _():
        issue(0, 0)

    @pl.when(r + 1 < n_q)
    def _():
        issue(r + 1, (r + 1) % 2)

    slot = r % 2
    pltpu.make_async_copy(kbuf_ref.at[slot], kbuf_ref.at[slot], sem_ref.at[0, slot]).wait()
    pltpu.make_async_copy(vbuf_ref.at[slot], vbuf_ref.at[slot], sem_ref.at[1, slot]).wait()

    qb = q_ref[...].astype(BF16)
    dist = (past + r % t_len) - idxv_ref[...]
    bucket = _rel_bucket(jnp.maximum(dist, 0))
    s_rows = []
    for h in range(N_HEADS):
        kh = kbuf_ref[slot, :, h, :].astype(BF16)
        s_rows.append(_dot_nt(qb, kh)[h:h + 1, :])
    s = jnp.concatenate(s_rows, axis=0)
    bias = jnp.zeros(s.shape, F32)
    for bkt in range(REL_BUCKETS):
        bias = jnp.where(bucket == bkt, rbt_ref[:, bkt:bkt + 1], bias)
    s = jnp.where(dist >= 0, s + bias, NEG)
    p = jnp.exp(s - s.max(axis=-1, keepdims=True))
    p = (p / p.sum(axis=-1, keepdims=True)).astype(BF16)
    for h in range(N_HEADS):
        vh = vbuf_ref[slot, :, h, :].astype(BF16)
        o_ref[h:h + 1, :] = _dot(p, vh)[h:h + 1, :]


def dec_attn(idx_flat, page_table, idx_rows, q, cache_k, cache_v, layer, k_new, v_new, rbt, t_len, n_sel):
    n_batch, n_pages = page_table.shape
    n_q = n_batch * t_len
    slab = (N_HEADS, HEAD_DIM)
    return pl.pallas_call(
        functools.partial(_dec_attn_kernel, layer=layer, t_len=t_len, n_pages=n_pages, n_q=n_q,
                          n_sel=n_sel),
        out_shape=jax.ShapeDtypeStruct((n_q,) + slab, F32),
        grid_spec=pltpu.PrefetchScalarGridSpec(
            num_scalar_prefetch=2,
            grid=(n_q,),
            in_specs=[pl.BlockSpec(memory_space=pl.ANY),
                      pl.BlockSpec(memory_space=pl.ANY),
                      pl.BlockSpec(memory_space=pl.ANY),
                      pl.BlockSpec(memory_space=pl.ANY),
                      pl.BlockSpec((None,) + slab, lambda r, ix, pt: (r, 0, 0)),
                      pl.BlockSpec((None, 1, n_sel), lambda r, ix, pt: (r, 0, 0)),
                      pl.BlockSpec((N_HEADS, REL_BUCKETS), lambda r, ix, pt: (0, 0))],
            out_specs=pl.BlockSpec((None,) + slab, lambda r, ix, pt: (r, 0, 0)),
            scratch_shapes=[pltpu.VMEM((2, n_sel) + slab, F32),
                            pltpu.VMEM((2, n_sel) + slab, F32),
                            pltpu.SemaphoreType.DMA((2, 2))]),
        compiler_params=_cp(1),
        name="dec_attn",
    )(idx_flat, page_table, cache_k, cache_v, k_new, v_new, q, idx_rows, rbt)


ATT_T = 256
IDX_TQ = 256
DEC_PAD = 16


def _attn_in_proj(xb, w_in, layer, tm, prompt_batches=0):
    if prompt_batches:
        q, = proj(xb, w_in, layer, 0, D_MODEL, 1024, tm, [BF16], scale=HEAD_DIM ** -0.5 * LOG2E,
                  name="proj_q")
        k, kb = proj(xb, w_in, layer, D_MODEL, D_MODEL, 1024, tm, [F32], name="proj_k",
                     transposed_batches=prompt_batches)
        v, vb = proj(xb, w_in, layer, 2 * D_MODEL, D_MODEL, 1024, tm, [F32, BF16], name="proj_v")
    else:
        q, = proj(xb, w_in, layer, 0, D_MODEL, 1024, tm, [F32], scale=HEAD_DIM ** -0.5, name="proj_q")
        k, = proj(xb, w_in, layer, D_MODEL, D_MODEL, 1024, tm, [F32], name="proj_k")
        v, = proj(xb, w_in, layer, 2 * D_MODEL, D_MODEL, 1024, tm, [F32], name="proj_v")
        kb = vb = None
    iq, = proj(xb, w_in, layer, 3 * D_MODEL, IDX_HEADS * IDX_DIM, 1024, tm, [BF16], name="proj_iq")
    o4 = 3 * D_MODEL + IDX_HEADS * IDX_DIM
    w_tail = jnp.pad(w_in[layer, :, o4:], ((0, 0), (0, 2 * LANES - (IDX_DIM + IDX_HEADS))))
    tail, = proj(xb, w_tail, 0, 0, 2 * LANES, 2 * LANES, tm, [F32], name="proj_tail")
    ik = tail[:, :IDX_DIM]
    iw = tail[:, IDX_DIM:IDX_DIM + IDX_HEADS] * IDX_HEADS ** -0.5
    return q, k, v, kb, vb, iq, ik, iw


def _attn_prompt(xb, w_in, w_out_args, layer, rel_tiles, n_batch):
    m = xb.shape[0]
    t_len = m // n_batch
    q, k, v, kt, vb, iq, ik, iw = _attn_in_proj(xb, w_in, layer, 1024, prompt_batches=n_batch)
    ikt = jnp.swapaxes(ik.astype(BF16).reshape(n_batch, t_len, IDX_DIM), 1, 2)
    mask = idx_mask(iq, iw, ikt, n_batch, IDX_TQ)
    o = attn(q, kt, vb, mask, rel_tiles, n_batch, ATT_T)
    k = k.reshape(n_batch, t_len, N_HEADS, HEAD_DIM)
    v = v.reshape(n_batch, t_len, N_HEADS, HEAD_DIM)
    return o, k, v, ik.reshape(n_batch, t_len, IDX_DIM)


def _attn_sample(xb, w_in, layer, cache_k, cache_v, cache_idx_k, page_table, rbt, n_batch):
    m = xb.shape[0]
    t_len = m // n_batch
    q, k_new, v_new, _, _, iq, ik, iw = _attn_in_proj(xb, w_in, layer, m)
    iq_r = iq.reshape(n_batch, t_len * IDX_HEADS, IDX_DIM)
    iw_r = iw.reshape(n_batch, t_len * IDX_HEADS, 1)
    ik_pad = jnp.pad(ik.reshape(n_batch, t_len, IDX_DIM), ((0, 0), (0, LANES - t_len), (0, 0)))
    sc = dec_scores(page_table, cache_idx_k, layer, iq_r, iw_r, ik_pad.astype(BF16), t_len)
    n_keys = sc.shape[-1]
    n_sel = min(TOPK_MAX, (n_keys - LANES + t_len) // 4)
    s_blocks = jnp.transpose(sc.reshape(m, n_keys // LANES, LANES), (1, 0, 2))
    sel = dec_topk(s_blocks, n_sel)
    sel = jnp.swapaxes(sel[:, :, 0], 0, 1)
    slab = (m, N_HEADS, HEAD_DIM)
    o = dec_attn(sel.reshape(-1), page_table, sel[:, None, :], q.reshape(slab), cache_k, cache_v, layer,
                 k_new.reshape(slab), v_new.reshape(slab), rbt, t_len, n_sel)
    return (o.reshape(m, D_MODEL).astype(BF16), k_new.reshape(n_batch, t_len, N_HEADS, HEAD_DIM),
            v_new.reshape(n_batch, t_len, N_HEADS, HEAD_DIM), ik.reshape(n_batch, t_len, IDX_DIM))


def _rot_tables(pos):
    half = RET_DK // 2
    inv = jnp.exp(-math.log(ROPE_BASE) * jnp.arange(half, dtype=F32) / half)
    ang = pos.astype(F32)[:, None] * inv[None, :]
    return jnp.cos(ang), jnp.sin(ang)


def _retention_mixer(xb, w_in, layer, gn, s0, lg, cos, sin, n_batch, chunk, pad_to):
    m = xb.shape[0]
    t_len = m // n_batch
    tm = min(1024, m)
    qk = proj_rot(xb, w_in, layer, cos, sin, 1024, tm)
    v, = proj(xb, w_in, layer, 2 * D_MODEL, RET_VW, 1024, tm, [BF16], name="proj_rv")
    g, = proj(xb, w_in, layer, 2 * D_MODEL + RET_VW, RET_VW, 1024, tm, [F32], name="proj_rg")
    if pad_to != t_len:
        padr = lambda a: jnp.pad(a.reshape(n_batch, t_len, -1),
                                 ((0, 0), (0, pad_to - t_len), (0, 0))).reshape(n_batch * pad_to, -1)
        z, s_fin = retention(lg, padr(qk), padr(v), padr(g), gn, s0, n_batch, pad_to, chunk)
        z = z.reshape(n_batch, pad_to, RET_VW)[:, :t_len].reshape(m, RET_VW)
    else:
        z, s_fin = retention(lg, qk, v, g, gn, s0, n_batch, chunk, chunk)
    return z, s_fin


MOE_TILE = 512
MOE_GU_TN = D_FF_EXPERT // 2


def _moe_dense(xf, xb, w_router_pad, w_gu, w_down_flat, layer, g, b):
    meta, _ = router(xf, w_router_pad, 512)
    gates_t = jnp.swapaxes(meta[:, :N_EXPERTS], 0, 1)[:, :, None]
    h = moe_gu(xb, w_gu, layer, gates_t, 256, 1024)
    return down_ln(h, w_down_flat, layer, xf, g, b, 512, 512, D_FF_EXPERT)


def _moe_sparse(xf, w_router_pad, w_gu, w_down, layer, g, b):
    m = xf.shape[0]
    meta, cnt = router(xf, w_router_pad, 512)
    col = lambda c: meta[:, c].astype(jnp.int32)
    counts = cnt[0, :N_EXPERTS].astype(jnp.int32)
    n_tile = (counts + MOE_TILE - 1) // MOE_TILE
    end_tile = jnp.cumsum(n_tile)
    start_row = (end_tile - n_tile) * MOE_TILE
    pos1 = start_row[col(META_E1)] + col(META_R1)
    pos2 = start_row[col(META_E2)] + col(META_R2)
    max_tiles = (2 * m + N_EXPERTS * (MOE_TILE - 1)) // MOE_TILE
    tiles = jnp.arange(max_tiles, dtype=jnp.int32)
    tile_expert = jnp.minimum(jnp.sum(tiles[:, None] >= end_tile[None, :], axis=1), N_EXPERTS - 1)
    n_valid = end_tile[-1:].astype(jnp.int32)
    xg = moe_dispatch(pos1, pos2, start_row + counts, end_tile * MOE_TILE, n_valid, xf, MOE_TILE, max_tiles)
    te = tile_expert.astype(jnp.int32)
    hg = moe_gu_grouped(te, n_valid, xg, w_gu, layer, MOE_TILE, MOE_GU_TN)
    yg = moe_down_grouped(te, n_valid, hg, w_down, layer, MOE_TILE, 1024)
    return moe_combine_ln(pos1, pos2, yg, meta, xf, g, b, 256)


def kernel(x_prompt, x_sample, cache_k, cache_v, cache_idx_k, state_ret, page_table, rel_bias,
           w_in_attn, w_out_attn, w_in_ret, ret_gn_g, w_out_ret, w_ffn_gu, w_ffn_down,
           w_router, w_exp_gu, w_exp_down, ln_g, ln_b):
    bp, tp, _ = x_prompt.shape
    bs, ts, _ = x_sample.shape
    past = page_table.shape[1] * PAGE_SIZE
    mp, ms = bp * tp, bs * ts
    xp = x_prompt.reshape(mp, D_MODEL)
    xs = x_sample.reshape(ms, D_MODEL)
    xpb, xsb = xp.astype(BF16), xs.astype(BF16)

    rel_tiles = bias_tiles(rel_bias, ATT_T)
    rbt = rel_bias.T
    w_out_attn, w_out_ret, w_ffn_down = (w.astype(BF16) for w in (w_out_attn, w_out_ret, w_ffn_down))
    lg = jnp.log1p(-jnp.exp2(-5.0 - jnp.arange(RET_HEADS, dtype=F32)))
    cos_p, sin_p = _rot_tables(jnp.tile(jnp.arange(tp, dtype=jnp.int32), bp))
    cos_s, sin_s = _rot_tables(jnp.tile(past + jnp.arange(ts, dtype=jnp.int32), bs))
    w_exp_down_flat = w_exp_down.reshape(w_exp_down.shape[0], N_EXPERTS * D_FF_EXPERT, D_MODEL)
    zero_state = jnp.zeros((bp, RET_HEADS, RET_DK, RET_DV), F32)

    kp_l, vp_l, ikp_l, sp_l = [], [], [], []
    ks_l, vs_l, iks_l, ss_l = [], [], [], []
    for i in range(DEPTH):
        j = i // 2
        g0, b0 = ln_g[i, 0][None, :], ln_b[i, 0][None, :]
        g1, b1 = ln_g[i, 1][None, :], ln_b[i, 1][None, :]
        if i % 2 == 0:
            op, kp, vp, ikp = _attn_prompt(xpb, w_in_attn, None, j, rel_tiles, bp)
            os_, k_s, v_s, iks = _attn_sample(xsb, w_in_attn, j, cache_k, cache_v, cache_idx_k,
                                              page_table, rbt, bs)
            kp_l.append(kp); vp_l.append(vp); ikp_l.append(ikp)
            ks_l.append(k_s); vs_l.append(v_s); iks_l.append(iks)
            xp, xpb = down_ln(op, w_out_attn, j, xp, g0, b0, 512, 512, D_MODEL)
            xs, xsb = down_ln(os_, w_out_attn, j, xs, g0, b0, 512, 512, D_MODEL)
        else:
            gn = ret_gn_g[j][None, :]
            zp, sp = _retention_mixer(xpb, w_in_ret, j, gn, zero_state, lg, cos_p, sin_p,
                                      bp, min(RET_CHUNK, tp), tp)
            zs, ss = _retention_mixer(xsb, w_in_ret, j, gn, state_ret[j], lg, cos_s, sin_s,
                                      bs, ts, DEC_PAD)
            sp_l.append(sp); ss_l.append(ss)
            xp, xpb = down_ln(zp, w_out_ret, j, xp, g0, b0, 512, 512, RET_VW // 2)
            xs, xsb = down_ln(zs, w_out_ret, j, xs, g0, b0, 512, 512, RET_VW // 2)
        if i % 2 == 0:
            hp = ffn_gu(xpb, w_ffn_gu, j, 512, 1024)
            hs = ffn_gu(xsb, w_ffn_gu, j, 512, 1024)
            xp, xpb = down_ln(hp, w_ffn_down, j, xp, g1, b1, 512, 512, D_FF // 2)
            xs, xsb = down_ln(hs, w_ffn_down, j, xs, g1, b1, 512, 512, D_FF // 2)
        else:
            w_r = jnp.pad(w_router[j], ((0, 0), (0, LANES - N_EXPERTS)))
            xp, xpb = _moe_sparse(xp, w_r, w_exp_gu, w_exp_down, j, g1, b1)
            xs, xsb = _moe_dense(xs, xsb, w_r, w_exp_gu, w_exp_down_flat, j, g1, b1)

    return (xp.reshape(bp, tp, D_MODEL), xs.reshape(bs, ts, D_MODEL),
            jnp.stack(kp_l), jnp.stack(vp_l), jnp.stack(ikp_l), jnp.stack(sp_l),
            jnp.stack(ks_l), jnp.stack(vs_l), jnp.stack(iks_l), jnp.stack(ss_l))
```

```python
import functools
import math

import jax
import jax.numpy as jnp
import numpy as np
from jax import lax
from jax.experimental import pallas as pl
from jax.experimental.pallas import tpu as pltpu

F32 = jnp.float32
BF16 = jnp.bfloat16

D_MODEL = 2048
PAGE_SIZE = 128
N_HEADS = 16
HEAD_DIM = 128
IDX_HEADS = 16
IDX_DIM = 128
TOPK_MAX = 256
REL_BUCKETS = 32
REL_MAX_DIST = 128
RET_HEADS = 8
RET_DK = 256
RET_DV = 512
RET_VW = RET_HEADS * RET_DV
RET_CHUNK = 128
ROPE_BASE = 10000.0
D_FF = 5632
N_EXPERTS = 8
D_FF_EXPERT = 2816
DEPTH = 4
DN_ALPHA = (2 * DEPTH) ** 0.25
LN_EPS = 1e-5
NEG = -1e30
LOG2E = math.log2(math.e)

LANES = 128
VMEM_LIMIT = 56 * 1024 * 1024
INT_MIN = -2 ** 31
NEG_KEY = int(np.float32(NEG).view(np.int32)) ^ 0x7FFFFFFF


def _cp(n_axes, vmem=VMEM_LIMIT):
    return pltpu.CompilerParams(dimension_semantics=("arbitrary",) * n_axes,
                                vmem_limit_bytes=vmem)


def _silu(x):
    return x * (1.0 / (1.0 + jnp.exp(-x)))


def _dot(a, b):
    return jnp.dot(a, b, preferred_element_type=F32)


def _dot_nt(a, b):
    return lax.dot_general(a, b, (((1,), (1,)), ((), ())), preferred_element_type=F32)


def _dot_tn(a, b):
    return lax.dot_general(a, b, (((0,), (0,)), ((), ())), preferred_element_type=F32)


def _proj_kernel(a_ref, w_ref, *rest, n_out, scale, transposed):
    out_refs = rest[:n_out]
    wb_ref = rest[-1]

    @pl.when(pl.program_id(1) == 0)
    def _():
        wb_ref[...] = w_ref[...].astype(BF16)

    acc = _dot(a_ref[...], wb_ref[...])
    if scale != 1.0:
        acc = acc * scale
    for o in out_refs:
        o[...] = acc.astype(o.dtype)
    if transposed:
        rest[n_out][...] = acc.T.astype(BF16)


def proj(a, w, layer, col0, ncols, tn, tm, out_dtypes, scale=1.0, name="proj", transposed_batches=0):
    m, k = a.shape
    tm = min(tm, m)
    assert col0 % tn == 0 and ncols % tn == 0 and m % tm == 0
    off = col0 // tn
    if w.ndim == 3:
        w_spec = pl.BlockSpec((None, k, tn), lambda j, i: (layer, 0, j + off))
    else:
        w_spec = pl.BlockSpec((k, tn), lambda j, i: (0, j + off))
    out_shape = [jax.ShapeDtypeStruct((m, ncols), dt) for dt in out_dtypes]
    out_specs = [pl.BlockSpec((tm, tn), lambda j, i: (i, j)) for _ in out_dtypes]
    if transposed_batches:
        per = m // transposed_batches // tm
        out_shape.append(jax.ShapeDtypeStruct((transposed_batches, ncols, m // transposed_batches), BF16))
        out_specs.append(pl.BlockSpec((None, tn, tm), lambda j, i: (i // per, j, i % per)))
    outs = pl.pallas_call(
        functools.partial(_proj_kernel, n_out=len(out_dtypes), scale=scale,
                          transposed=bool(transposed_batches)),
        out_shape=out_shape,
        grid=(ncols // tn, m // tm),
        in_specs=[pl.BlockSpec((tm, k), lambda j, i: (i, 0)), w_spec],
        out_specs=out_specs,
        scratch_shapes=[pltpu.VMEM((k, tn), BF16)],
        compiler_params=_cp(2),
        name=name,
    )(a, w)
    return outs


def _proj_rot_kernel(a_ref, w_ref, cos_ref, sin_ref, o_ref, wb_ref, *, heads_per_tile, q_tiles):
    j = pl.program_id(0)

    @pl.when(pl.program_id(1) == 0)
    def _():
        wb_ref[...] = w_ref[...].astype(BF16)

    acc = _dot(a_ref[...], wb_ref[...])
    scale = jnp.where(j >= q_tiles, RET_DK ** -0.5, 1.0).astype(F32)
    c = cos_ref[...]
    s = sin_ref[...]
    half = RET_DK // 2
    for hh in range(heads_per_tile):
        x1 = acc[:, hh * RET_DK: hh * RET_DK + half]
        x2 = acc[:, hh * RET_DK + half: (hh + 1) * RET_DK]
        o_ref[:, hh * RET_DK: hh * RET_DK + half] = ((x1 * c - x2 * s) * scale).astype(BF16)
        o_ref[:, hh * RET_DK + half: (hh + 1) * RET_DK] = ((x1 * s + x2 * c) * scale).astype(BF16)


def proj_rot(a, w, layer, cos, sin, tn, tm):
    m, k = a.shape
    tm = min(tm, m)
    ncols = 2 * D_MODEL
    return pl.pallas_call(
        functools.partial(_proj_rot_kernel, heads_per_tile=tn // RET_DK, q_tiles=D_MODEL // tn),
        out_shape=jax.ShapeDtypeStruct((m, ncols), BF16),
        grid=(ncols // tn, m // tm),
        in_specs=[pl.BlockSpec((tm, k), lambda j, i: (i, 0)),
                  pl.BlockSpec((None, k, tn), lambda j, i: (layer, 0, j)),
                  pl.BlockSpec((tm, RET_DK // 2), lambda j, i: (i, 0)),
                  pl.BlockSpec((tm, RET_DK // 2), lambda j, i: (i, 0))],
        out_specs=pl.BlockSpec((tm, tn), lambda j, i: (i, j)),
        scratch_shapes=[pltpu.VMEM((k, tn), BF16)],
        compiler_params=_cp(2),
        name="proj_rot",
    )(a, w, cos, sin)


def _gu_kernel(a_ref, wg_ref, wu_ref, o_ref, wgb_ref, wub_ref):
    @pl.when(pl.program_id(1) == 0)
    def _():
        wgb_ref[...] = wg_ref[...].astype(BF16)
        wub_ref[...] = wu_ref[...].astype(BF16)

    a = a_ref[...]
    o_ref[...] = (_silu(_dot(a, wgb_ref[...])) * _dot(a, wub_ref[...])).astype(BF16)


def ffn_gu(a, w_gu, layer, tn, tm):
    m, k = a.shape
    tm = min(tm, m)
    nj = D_FF // tn
    return pl.pallas_call(
        _gu_kernel,
        out_shape=jax.ShapeDtypeStruct((m, D_FF), BF16),
        grid=(nj, m // tm),
        in_specs=[pl.BlockSpec((tm, k), lambda j, i: (i, 0)),
                  pl.BlockSpec((None, k, tn), lambda j, i: (layer, 0, j)),
                  pl.BlockSpec((None, k, tn), lambda j, i: (layer, 0, j + nj))],
        out_specs=pl.BlockSpec((tm, tn), lambda j, i: (i, j)),
        scratch_shapes=[pltpu.VMEM((k, tn), BF16), pltpu.VMEM((k, tn), BF16)],
        compiler_params=_cp(2),
        name="ffn_gu",
    )(a, w_gu, w_gu)


def _down_ln_kernel(a_ref, w_ref, x_ref, g_ref, b_ref, of_ref, ob_ref, row_ref, *, nj, nk, tn):
    k = pl.program_id(1)
    j = pl.program_id(2)
    y = _dot(a_ref[...], w_ref[...].astype(BF16))

    @pl.when(k == 0)
    def _():
        row_ref[j] = y

    @pl.when(k > 0)
    def _():
        row_ref[j] = row_ref[j] + y

    @pl.when((j == nj - 1) & (k == nk - 1))
    def _():
        zs = [DN_ALPHA * x_ref[:, jj * tn:(jj + 1) * tn] + row_ref[jj] for jj in range(nj)]
        tot = zs[0].sum(axis=-1, keepdims=True)
        for z in zs[1:]:
            tot = tot + z.sum(axis=-1, keepdims=True)
        mu = tot * (1.0 / D_MODEL)
        sq = None
        for z in zs:
            t = jnp.square(z - mu).sum(axis=-1, keepdims=True)
            sq = t if sq is None else sq + t
        rstd = lax.rsqrt(sq * (1.0 / D_MODEL) + LN_EPS)
        for jj, z in enumerate(zs):
            sl = slice(jj * tn, (jj + 1) * tn)
            o = (z - mu) * rstd * g_ref[:, sl] + b_ref[:, sl]
            of_ref[:, sl] = o
            ob_ref[:, sl] = o.astype(BF16)


def down_ln(a, w, layer, x, g, b, tm, tn, tk):
    m, kk = a.shape
    tm = min(tm, m)
    n = D_MODEL
    assert kk % tk == 0 and n % tn == 0 and m % tm == 0
    nj, nk = n // tn, kk // tk
    return pl.pallas_call(
        functools.partial(_down_ln_kernel, nj=nj, nk=nk, tn=tn),
        out_shape=[jax.ShapeDtypeStruct((m, n), F32), jax.ShapeDtypeStruct((m, n), BF16)],
        grid=(m // tm, nk, nj),
        in_specs=[pl.BlockSpec((tm, tk), lambda i, k, j: (i, k)),
                  pl.BlockSpec((None, tk, tn), lambda i, k, j: (layer, k, j)),
                  pl.BlockSpec((tm, n), lambda i, k, j: (i, 0)),
                  pl.BlockSpec((1, n), lambda i, k, j: (0, 0)),
                  pl.BlockSpec((1, n), lambda i, k, j: (0, 0))],
        out_specs=[pl.BlockSpec((tm, n), lambda i, k, j: (i, 0)),
                   pl.BlockSpec((tm, n), lambda i, k, j: (i, 0))],
        scratch_shapes=[pltpu.VMEM((nj, tm, tn), F32)],
        compiler_params=_cp(3),
        name="down_ln",
    )(a, w, x, g, b)


def _router_kernel(x_ref, w_ref, o_ref, cnt_ref, carry_ref):
    x = x_ref[...]
    w = w_ref[...]
    xh = x.astype(BF16)
    xl = (x - xh.astype(F32)).astype(BF16)
    wh = w.astype(BF16)
    wl = (w - wh.astype(F32)).astype(BF16)
    lg = _dot(xh, wh) + (_dot(xh, wl) + _dot(xl, wh))
    lane = lax.broadcasted_iota(jnp.int32, lg.shape, 1)
    lg = jnp.where(lane < N_EXPERTS, lg, -jnp.inf)
    m1 = lg.max(axis=-1, keepdims=True)
    i1 = jnp.where(lg == m1, lane, LANES).min(axis=-1, keepdims=True)
    lg2 = jnp.where(lane == i1, -jnp.inf, lg)
    m2 = lg2.max(axis=-1, keepdims=True)
    i2 = jnp.where(lg2 == m2, lane, LANES).min(axis=-1, keepdims=True)
    e2 = jnp.exp(m2 - m1)
    den = 1.0 + e2
    g1 = 1.0 / den
    g2 = e2 / den
    @pl.when(pl.program_id(0) == 0)
    def _():
        carry_ref[...] = jnp.zeros(carry_ref.shape, F32)

    tm = lg.shape[0]
    hit = jnp.where(lane == i1, 1.0, jnp.where(lane == i2, 1.0, 0.0))
    rr = lax.broadcasted_iota(jnp.int32, (tm, tm), 0)
    cc = lax.broadcasted_iota(jnp.int32, (tm, tm), 1)
    tri = jnp.where(cc < rr, 1.0, 0.0).astype(BF16)
    before = _dot(tri, hit.astype(BF16)) + carry_ref[...]
    r1 = jnp.where(lane == i1, before, 0.0).sum(axis=-1, keepdims=True)
    r2 = jnp.where(lane == i2, before, 0.0).sum(axis=-1, keepdims=True)
    carry_ref[...] = carry_ref[...] + hit.sum(axis=0, keepdims=True)
    cnt_ref[...] = carry_ref[...]
    meta = jnp.where(lane == i1, g1, 0.0) + jnp.where(lane == i2, g2, 0.0)
    for col, val in ((META_E1, i1.astype(F32)), (META_E2, i2.astype(F32)), (META_G1, g1),
                     (META_G2, g2), (META_R1, r1), (META_R2, r2)):
        meta = jnp.where(lane == col, val, meta)
    o_ref[...] = meta


META_E1, META_E2, META_G1, META_G2, META_R1, META_R2 = 8, 9, 10, 11, 12, 13


def router(x, w_pad, tm):
    m, k = x.shape
    tm = min(tm, m)
    return pl.pallas_call(
        _router_kernel,
        out_shape=[jax.ShapeDtypeStruct((m, LANES), F32), jax.ShapeDtypeStruct((1, LANES), F32)],
        grid=(m // tm,),
        in_specs=[pl.BlockSpec((tm, k), lambda i: (i, 0)),
                  pl.BlockSpec((k, LANES), lambda i: (0, 0))],
        out_specs=[pl.BlockSpec((tm, LANES), lambda i: (i, 0)),
                   pl.BlockSpec((1, LANES), lambda i: (0, 0))],
        scratch_shapes=[pltpu.VMEM((1, LANES), F32)],
        compiler_params=_cp(1),
        name="router",
    )(x, w_pad)


def _dispatch_kernel(p1_ref, p2_ref, s1_ref, s2_ref, lo_ref, hi_ref, nv_ref, x_hbm, xs_hbm, o_ref,
                     src_ref, buf_ref, sem_ref, fix_sem_ref, *, m, ms, tile, n_tiles):
    t = pl.program_id(0)
    nv = nv_ref[0]

    @pl.when(t == 0)
    def _():
        for e in range(N_EXPERTS):
            def clear(i, c):
                src_ref[i] = 0
                return c
            lax.fori_loop(lo_ref[e], hi_ref[e], clear, 0)

        def invert(n, c):
            src_ref[p1_ref[n]] = n
            src_ref[p2_ref[n]] = n
            return c
        lax.fori_loop(0, m, invert, 0, unroll=8)

        def extra(n, c):
            src_ref[s1_ref[n]] = 0
            src_ref[s2_ref[n]] = 0
            return c
        lax.fori_loop(0, ms, extra, 0)

    def issue(tt, slot):
        def body(i, c):
            tok = src_ref[tt * tile + i]
            pltpu.make_async_copy(x_hbm.at[pl.ds(tok, 1)], buf_ref.at[slot, pl.ds(i, 1)],
                                  sem_ref.at[slot]).start()
            return c
        lax.fori_loop(0, tile, body, 0, unroll=8)

    @pl.when(t == 0)
    def _():
        issue(0, 0)

    @pl.when(t + 1 < nv)
    def _():
        issue(t + 1, (t + 1) % 2)

    slot = t % 2

    @pl.when(t < nv)
    def _():
        pltpu.make_async_copy(buf_ref.at[slot], buf_ref.at[slot], sem_ref.at[slot]).wait()

        def fix_row(n, pos, k):
            hit = (pos >= t * tile) & (pos < (t + 1) * tile)

            @pl.when(hit)
            def _():
                pltpu.make_async_copy(xs_hbm.at[pl.ds(n, 1)], buf_ref.at[slot, pl.ds(pos - t * tile, 1)],
                                      fix_sem_ref.at[0]).start()
            return k + jnp.where(hit, 1, 0)

        def fix_issue(n, k):
            return fix_row(n, s2_ref[n], fix_row(n, s1_ref[n], k))
        n_fix = lax.fori_loop(0, ms, fix_issue, 0)

        def fix_wait(i, c):
            pltpu.make_async_copy(xs_hbm.at[pl.ds(0, 1)], buf_ref.at[slot, pl.ds(0, 1)],
                                  fix_sem_ref.at[0]).wait()
            return c
        lax.fori_loop(0, n_fix, fix_wait, 0)
        o_ref[...] = buf_ref[slot].astype(BF16)

    @pl.when(t >= nv)
    def _():
        o_ref[...] = jnp.zeros(o_ref.shape, o_ref.dtype)


def moe_dispatch(pos1, pos2, pos1_x, pos2_x, pad_lo, pad_hi, n_valid, x, x_extra, tile, n_tiles):
    m, d = x.shape
    return pl.pallas_call(
        functools.partial(_dispatch_kernel, m=m, ms=x_extra.shape[0], tile=tile, n_tiles=n_tiles),
        out_shape=jax.ShapeDtypeStruct((n_tiles * tile, d), BF16),
        grid_spec=pltpu.PrefetchScalarGridSpec(
            num_scalar_prefetch=7,
            grid=(n_tiles,),
            in_specs=[pl.BlockSpec(memory_space=pl.ANY), pl.BlockSpec(memory_space=pl.ANY)],
            out_specs=pl.BlockSpec((tile, d), lambda t, *_: (t, 0)),
            scratch_shapes=[pltpu.SMEM((n_tiles * tile,), jnp.int32),
                            pltpu.VMEM((2, tile, d), F32),
                            pltpu.SemaphoreType.DMA((2,)),
                            pltpu.SemaphoreType.DMA((1,))]),
        compiler_params=_cp(1),
        name="moe_dispatch",
    )(pos1, pos2, pos1_x, pos2_x, pad_lo, pad_hi, n_valid, x, x_extra)


def _tile_ids(t, te_ref, nv_ref):
    nv = nv_ref[0]
    cur = te_ref[jnp.minimum(t, nv - 1)]
    prev = te_ref[jnp.minimum(jnp.maximum(t - 1, 0), nv - 1)]
    return nv, (t == 0) | (cur != prev)


def _ggu_kernel(te_ref, nv_ref, a_ref, wg_ref, wu_ref, o_ref, wgb_ref, wub_ref):
    t = pl.program_id(1)
    nv, new_expert = _tile_ids(t, te_ref, nv_ref)

    @pl.when(new_expert)
    def _():
        wgb_ref[...] = wg_ref[...].astype(BF16)
        wub_ref[...] = wu_ref[...].astype(BF16)

    @pl.when(t < nv)
    def _():
        a = a_ref[...]
        o_ref[...] = (_silu(_dot(a, wgb_ref[...])) * _dot(a, wub_ref[...])).astype(BF16)

    @pl.when(t >= nv)
    def _():
        o_ref[...] = jnp.zeros(o_ref.shape, o_ref.dtype)


def moe_gu_grouped(tile_expert, n_valid, a, w_gu, layer, tile, tn):
    r, k = a.shape
    nj = D_FF_EXPERT // tn
    row = lambda t, nv: jnp.minimum(t, nv[0] - 1)
    return pl.pallas_call(
        _ggu_kernel,
        out_shape=jax.ShapeDtypeStruct((r, D_FF_EXPERT), BF16),
        grid_spec=pltpu.PrefetchScalarGridSpec(
            num_scalar_prefetch=2,
            grid=(nj, r // tile),
            in_specs=[pl.BlockSpec((tile, k), lambda j, t, te, nv: (row(t, nv), 0)),
                      pl.BlockSpec((None, None, k, tn),
                                   lambda j, t, te, nv: (layer, te[row(t, nv)], 0, j),
                                   pipeline_mode=pl.Buffered(1)),
                      pl.BlockSpec((None, None, k, tn),
                                   lambda j, t, te, nv: (layer, te[row(t, nv)], 0, j + nj),
                                   pipeline_mode=pl.Buffered(1))],
            out_specs=pl.BlockSpec((tile, tn), lambda j, t, te, nv: (t, j)),
            scratch_shapes=[pltpu.VMEM((k, tn), BF16), pltpu.VMEM((k, tn), BF16)]),
        compiler_params=_cp(2),
        name="moe_gu_grouped",
    )(tile_expert, n_valid, a, w_gu, w_gu)


def _gdown_kernel(te_ref, nv_ref, a_ref, w_ref, o_ref, wb_ref):
    t = pl.program_id(1)
    nv, new_expert = _tile_ids(t, te_ref, nv_ref)

    @pl.when(new_expert)
    def _():
        wb_ref[...] = w_ref[...].astype(BF16)

    @pl.when(t < nv)
    def _():
        o_ref[...] = _dot(a_ref[...], wb_ref[...])

    @pl.when(t >= nv)
    def _():
        o_ref[...] = jnp.zeros(o_ref.shape, o_ref.dtype)


def moe_down_grouped(tile_expert, n_valid, a, w_down, layer, tile, tn):
    r, k = a.shape
    row = lambda t, nv: jnp.minimum(t, nv[0] - 1)
    return pl.pallas_call(
        _gdown_kernel,
        out_shape=jax.ShapeDtypeStruct((r, D_MODEL), F32),
        grid_spec=pltpu.PrefetchScalarGridSpec(
            num_scalar_prefetch=2,
            grid=(D_MODEL // tn, r // tile),
            in_specs=[pl.BlockSpec((tile, k), lambda j, t, te, nv: (row(t, nv), 0)),
                      pl.BlockSpec((None, None, k, tn),
                                   lambda j, t, te, nv: (layer, te[row(t, nv)], 0, j),
                                   pipeline_mode=pl.Buffered(1))],
            out_specs=pl.BlockSpec((tile, tn), lambda j, t, te, nv: (t, j)),
            scratch_shapes=[pltpu.VMEM((k, tn), BF16)]),
        compiler_params=_cp(2),
        name="moe_down_grouped",
    )(tile_expert, n_valid, a, w_down)


def _combine_kernel(p1_ref, p2_ref, y_hbm, meta_ref, x_ref, g_ref, b_ref, of_ref, ob_ref,
                    ybuf_ref, sem_ref, *, tm, n_tiles):
    i = pl.program_id(0)

    def issue(ii, slot):
        def body(r, c):
            n = ii * tm + r
            pltpu.make_async_copy(y_hbm.at[pl.ds(p1_ref[n], 1)], ybuf_ref.at[slot, 0, pl.ds(r, 1)],
                                  sem_ref.at[slot]).start()
            pltpu.make_async_copy(y_hbm.at[pl.ds(p2_ref[n], 1)], ybuf_ref.at[slot, 1, pl.ds(r, 1)],
                                  sem_ref.at[slot]).start()
            return c
        lax.fori_loop(0, tm, body, 0)

    @pl.when(i == 0)
    def _():
        issue(0, 0)

    @pl.when(i + 1 < n_tiles)
    def _():
        issue(i + 1, (i + 1) % 2)

    slot = i % 2
    pltpu.make_async_copy(ybuf_ref.at[slot], ybuf_ref.at[slot], sem_ref.at[slot]).wait()
    meta = meta_ref[...]
    lane = lax.broadcasted_iota(jnp.int32, meta.shape, 1)
    g1 = jnp.where(lane == META_G1, meta, 0.0).sum(axis=-1, keepdims=True)
    g2 = jnp.where(lane == META_G2, meta, 0.0).sum(axis=-1, keepdims=True)
    z = DN_ALPHA * x_ref[...] + (g1 * ybuf_ref[slot, 0] + g2 * ybuf_ref[slot, 1])
    mu = z.mean(axis=-1, keepdims=True)
    var = jnp.square(z - mu).mean(axis=-1, keepdims=True)
    o = (z - mu) * lax.rsqrt(var + LN_EPS) * g_ref[...] + b_ref[...]
    of_ref[...] = o
    ob_ref[...] = o.astype(BF16)


def moe_combine_ln(pos1, pos2, y, meta, x, g, b, tm):
    m, d = x.shape
    tm = min(tm, m)
    n_tiles = m // tm
    tok = lambda i, p1, p2: (i, 0)
    vec = lambda i, p1, p2: (0, 0)
    return pl.pallas_call(
        functools.partial(_combine_kernel, tm=tm, n_tiles=n_tiles),
        out_shape=[jax.ShapeDtypeStruct((m, d), F32), jax.ShapeDtypeStruct((m, d), BF16)],
        grid_spec=pltpu.PrefetchScalarGridSpec(
            num_scalar_prefetch=2,
            grid=(n_tiles,),
            in_specs=[pl.BlockSpec(memory_space=pl.ANY),
                      pl.BlockSpec((tm, LANES), tok),
                      pl.BlockSpec((tm, d), tok),
                      pl.BlockSpec((1, d), vec),
                      pl.BlockSpec((1, d), vec)],
            out_specs=[pl.BlockSpec((tm, d), tok), pl.BlockSpec((tm, d), tok)],
            scratch_shapes=[pltpu.VMEM((2, 2, tm, d), F32),
                            pltpu.SemaphoreType.DMA((2,))]),
        compiler_params=_cp(1),
        name="moe_combine_ln",
    )(pos1, pos2, y, meta, x, g, b)


def _sort_key(x):
    k = pltpu.bitcast(x, jnp.int32)
    return jnp.where(k < 0, k ^ jnp.int32(0x7FFFFFFF), k)


def _kth_largest_key(count_ge, n_sel, shape):
    zero = jnp.zeros(shape, jnp.int32)
    t0 = jnp.where(count_ge(zero) >= n_sel, zero, jnp.full(shape, INT_MIN, jnp.int32))

    def body(i, t):
        cand = t | jnp.left_shift(jnp.int32(1), 30 - i)
        return jnp.where(count_ge(cand) >= n_sel, cand, t)

    return lax.fori_loop(0, 31, body, t0)


IDX_GROUPS = 4


def _idx_mask_kernel(iq_ref, iw_ref, ikt_ref, o_ref, key_ref, *, tq, n_sel):
    i = pl.program_id(1)
    s_len = ikt_ref.shape[1]
    per = s_len // tq // IDX_GROUPS
    wf = iw_ref[...] * (IDX_DIM ** -0.5)

    def run(s_eff):
        ikt = ikt_ref[:, :s_eff]
        sc = None
        for h in range(IDX_HEADS):
            d = _dot(iq_ref[:, h * IDX_DIM:(h + 1) * IDX_DIM], ikt)
            t = jnp.maximum(d, 0.0) * wf[:, h:h + 1]
            sc = t if sc is None else sc + t
        qpos = i * tq + lax.broadcasted_iota(jnp.int32, (tq, s_eff), 0)
        kpos = lax.broadcasted_iota(jnp.int32, (tq, s_eff), 1)
        valid = kpos <= qpos
        key_ref[:, :s_eff] = _sort_key(jnp.where(valid, sc, NEG))
        n_tail = float(s_len - s_eff)

        def count_ge(cand):
            c = jnp.where(key_ref[:, :s_eff] >= cand, 1.0, 0.0).sum(axis=-1, keepdims=True)
            return c + jnp.where(cand <= NEG_KEY, n_tail, 0.0)

        thr = _kth_largest_key(count_ge, float(n_sel), (tq, 1))
        sel = (key_ref[:, :s_eff] >= thr) & valid
        o_ref[:, :s_eff] = jnp.where(sel, 0.0, NEG).astype(o_ref.dtype)
        if s_eff < s_len:
            o_ref[:, s_eff:] = jnp.full((tq, s_len - s_eff), NEG, o_ref.dtype)

    for g in range(IDX_GROUPS):
        pl.when(i // per == g)(functools.partial(run, (g + 1) * per * tq))


def idx_mask(iq, iw, ikt, n_batch, tq):
    m = iq.shape[0]
    s_len = ikt.shape[2]
    nq = s_len // tq
    assert nq % IDX_GROUPS == 0
    n_sel = min(TOPK_MAX, s_len // 4)
    return pl.pallas_call(
        functools.partial(_idx_mask_kernel, tq=tq, n_sel=n_sel),
        out_shape=jax.ShapeDtypeStruct((n_batch, s_len, s_len), BF16),
        grid=(n_batch, nq),
        in_specs=[pl.BlockSpec((tq, IDX_HEADS * IDX_DIM), lambda b, i: (b * nq + i, 0)),
                  pl.BlockSpec((tq, IDX_HEADS), lambda b, i: (b * nq + i, 0)),
                  pl.BlockSpec((None, IDX_DIM, s_len), lambda b, i: (b, 0, 0))],
        out_specs=pl.BlockSpec((None, tq, s_len), lambda b, i: (b, i, 0)),
        scratch_shapes=[pltpu.VMEM((tq, s_len), jnp.int32)],
        compiler_params=_cp(2),
        name="idx_mask",
    )(iq, iw, ikt)


def _rel_bucket(dist):
    max_exact = REL_BUCKETS // 2
    large = max_exact + (jnp.log(jnp.maximum(dist, 1).astype(F32) / max_exact)
                         / math.log(REL_MAX_DIST / max_exact) * (REL_BUCKETS - max_exact)).astype(jnp.int32)
    large = jnp.minimum(large, REL_BUCKETS - 1)
    return jnp.where(dist < max_exact, dist, large)


def _bias_tiles_kernel(rb_ref, o_ref, *, t):
    d = pl.program_id(0)
    ii = lax.broadcasted_iota(jnp.int32, (t, t), 0)
    jj = lax.broadcasted_iota(jnp.int32, (t, t), 1)
    bucket = _rel_bucket(jnp.maximum(d * t + ii - jj, 0))
    for h in range(N_HEADS):
        acc = jnp.zeros((t, t), F32)
        for bkt in range(REL_BUCKETS):
            acc = jnp.where(bucket == bkt, rb_ref[bkt, h] * LOG2E, acc)
        o_ref[h] = acc


def bias_tiles(rel_bias, t):
    assert 2 * t - (t - 1) >= REL_MAX_DIST
    return pl.pallas_call(
        functools.partial(_bias_tiles_kernel, t=t),
        out_shape=jax.ShapeDtypeStruct((3, N_HEADS, t, t), F32),
        grid=(3,),
        in_specs=[pl.BlockSpec(memory_space=pltpu.SMEM)],
        out_specs=pl.BlockSpec((None, N_HEADS, t, t), lambda d: (d, 0, 0, 0)),
        compiler_params=_cp(1),
        name="bias_tiles",
    )(rel_bias)


def _attn_kernel(it_ref, jt_ref, q_ref, kt_ref, v_ref, mask_ref, tz_ref, o_ref, m_ref, acc_ref):
    step = pl.program_id(1)
    i = it_ref[step]
    j = jt_ref[step]
    t = q_ref.shape[0]

    @pl.when(j == 0)
    def _():
        m_ref[...] = jnp.full(m_ref.shape, -jnp.inf, F32)
        acc_ref[...] = jnp.zeros(acc_ref.shape, F32)

    maskf = mask_ref[...].astype(F32)
    ones = jnp.ones((t, HEAD_DIM), BF16)
    twice = lambda a: jnp.concatenate([a] * (t // LANES), axis=1)
    for h in range(N_HEADS):
        sl = slice(h * HEAD_DIM, (h + 1) * HEAD_DIM)
        s = _dot(q_ref[:, sl], kt_ref[sl, :]) + tz_ref[h] + maskf
        m_prev = m_ref[h]
        m_new = jnp.maximum(m_prev, s.max(axis=-1, keepdims=True))
        alpha = jnp.exp2(m_prev - m_new)
        p = jnp.exp2(s - twice(m_new))
        pv = _dot(p.astype(BF16), jnp.concatenate([v_ref[:, sl], ones], axis=1))
        acc_ref[h] = jnp.concatenate([alpha, alpha], axis=1) * acc_ref[h] + pv
        m_ref[h] = m_new

    @pl.when(j == i)
    def _():
        for h in range(N_HEADS):
            a = acc_ref[h]
            o_ref[:, h * HEAD_DIM:(h + 1) * HEAD_DIM] = (a[:, :HEAD_DIM] / a[:, HEAD_DIM:]).astype(o_ref.dtype)


def attn(q, kt, v, mask, tz, n_batch, t):
    m = q.shape[0]
    nq = m // n_batch // t
    pairs = [(i, j) for i in range(nq) for j in range(i + 1)]
    it = jnp.asarray([p[0] for p in pairs], jnp.int32)
    jt = jnp.asarray([p[1] for p in pairs], jnp.int32)
    return pl.pallas_call(
        _attn_kernel,
        out_shape=jax.ShapeDtypeStruct((m, D_MODEL), BF16),
        grid_spec=pltpu.PrefetchScalarGridSpec(
            num_scalar_prefetch=2,
            grid=(n_batch, len(pairs)),
            in_specs=[pl.BlockSpec((t, D_MODEL), lambda b, s, it, jt: (b * nq + it[s], 0)),
                      pl.BlockSpec((None, D_MODEL, t), lambda b, s, it, jt: (b, 0, jt[s])),
                      pl.BlockSpec((t, D_MODEL), lambda b, s, it, jt: (b * nq + jt[s], 0)),
                      pl.BlockSpec((None, t, t), lambda b, s, it, jt: (b, it[s], jt[s])),
                      pl.BlockSpec((None, N_HEADS, t, t),
                                   lambda b, s, it, jt: (jnp.minimum(it[s] - jt[s], 2), 0, 0, 0))],
            out_specs=pl.BlockSpec((t, D_MODEL), lambda b, s, it, jt: (b * nq + it[s], 0)),
            scratch_shapes=[pltpu.VMEM((N_HEADS, t, LANES), F32),
                            pltpu.VMEM((N_HEADS, t, 2 * HEAD_DIM), F32)]),
        compiler_params=_cp(2),
        name="attn",
    )(it, jt, q, kt, v, mask, tz)


def _ret_kernel(lg_ref, q_ref, k_ref, v_ref, g_ref, gn_ref, s0_ref, z_ref, sout_ref, state_ref,
                *, c_pad, c_true, nc):
    c = pl.program_id(1)

    @pl.when(c == 0)
    def _():
        state_ref[...] = s0_ref[...]

    ii = lax.broadcasted_iota(jnp.int32, (c_pad, c_pad), 0)
    jj = lax.broadcasted_iota(jnp.int32, (c_pad, c_pad), 1)
    rel = (ii - jj).astype(F32)
    idx = lax.broadcasted_iota(jnp.int32, (c_pad, 1), 0).astype(F32)
    for h in range(RET_HEADS):
        lg = lg_ref[h]
        qs = slice(h * RET_DK, (h + 1) * RET_DK)
        vs = slice(h * RET_DV, (h + 1) * RET_DV)
        q = q_ref[:, qs]
        k = k_ref[:, qs]
        v = v_ref[:, vs]
        decay = jnp.where(rel >= 0, jnp.exp(lg * jnp.maximum(rel, 0.0)), 0.0)
        scores = _dot_nt(q, k) * decay
        inner = _dot(scores.astype(BF16), v)
        s_prev = state_ref[h]
        cross = _dot(q, s_prev.astype(BF16)) * jnp.exp(lg * (idx + 1.0))
        o = inner + cross
        w_state = jnp.exp(lg * (c_true - 1.0 - idx))
        kw = (k.astype(F32) * w_state).astype(BF16)
        state_ref[h] = s_prev * jnp.exp(lg * c_true) + _dot_tn(kw, v)

        mu = o.mean(axis=-1, keepdims=True)
        var = jnp.square(o - mu).mean(axis=-1, keepdims=True)
        on = (o - mu) * lax.rsqrt(var + LN_EPS) * gn_ref[:, vs]
        z_ref[:, vs] = (_silu(g_ref[:, vs]) * on).astype(BF16)

    @pl.when(c == nc - 1)
    def _():
        sout_ref[...] = state_ref[...]


def retention(lg, qk, v, g, gn, s0, n_batch, c_pad, c_true):
    m = qk.shape[0]
    nc = m // n_batch // c_pad
    row = lambda b, c: b * nc + c
    state = (None, RET_HEADS, RET_DK, RET_DV)
    return pl.pallas_call(
        functools.partial(_ret_kernel, c_pad=c_pad, c_true=float(c_true), nc=nc),
        out_shape=[jax.ShapeDtypeStruct((m, RET_VW), BF16),
                   jax.ShapeDtypeStruct((n_batch, RET_HEADS, RET_DK, RET_DV), F32)],
        grid=(n_batch, nc),
        in_specs=[pl.BlockSpec(memory_space=pltpu.SMEM),
                  pl.BlockSpec((c_pad, D_MODEL), lambda b, c: (row(b, c), 0)),
                  pl.BlockSpec((c_pad, D_MODEL), lambda b, c: (row(b, c), 1)),
                  pl.BlockSpec((c_pad, RET_VW), lambda b, c: (row(b, c), 0)),
                  pl.BlockSpec((c_pad, RET_VW), lambda b, c: (row(b, c), 0)),
                  pl.BlockSpec((1, RET_VW), lambda b, c: (0, 0)),
                  pl.BlockSpec(state, lambda b, c: (b, 0, 0, 0))],
        out_specs=[pl.BlockSpec((c_pad, RET_VW), lambda b, c: (row(b, c), 0)),
                   pl.BlockSpec(state, lambda b, c: (b, 0, 0, 0))],
        scratch_shapes=[pltpu.VMEM((RET_HEADS, RET_DK, RET_DV), F32)],
        compiler_params=_cp(2),
        name="retention",
    )(lg, qk, qk, v, g, gn, s0)


DEC_CHUNK = 2048


def _dec_scores_kernel(pt_ref, ik_hbm, iq_ref, iw_ref, ikn_ref, o_ref, buf_ref, sem_ref,
                       *, layer, t_len, n_pages, n_batch):
    b = pl.program_id(0)
    past = n_pages * PAGE_SIZE

    def issue(bb, slot):
        def body(p, c):
            pltpu.make_async_copy(ik_hbm.at[layer, pt_ref[bb, p]],
                                  buf_ref.at[slot, pl.ds(p * PAGE_SIZE, PAGE_SIZE)],
                                  sem_ref.at[slot]).start()
            return c
        lax.fori_loop(0, n_pages, body, 0)

    @pl.when(b == 0)
    def _():
        issue(0, 0)

    @pl.when(b + 1 < n_batch)
    def _():
        issue(b + 1, (b + 1) % 2)

    slot = b % 2
    pltpu.make_async_copy(buf_ref.at[slot], buf_ref.at[slot], sem_ref.at[slot]).wait()
    iq = iq_ref[...]
    wf = iw_ref[...] * (IDX_DIM ** -0.5)

    def head_sum(keys_bf16):
        r = jnp.maximum(_dot_nt(iq, keys_bf16), 0.0) * wf
        return r.reshape(t_len, IDX_HEADS, r.shape[-1]).sum(axis=1)

    chunk = min(DEC_CHUNK, past)
    for c in range(past // chunk):
        sl = slice(c * chunk, (c + 1) * chunk)
        o_ref[:, sl] = head_sum(buf_ref[slot, sl, :].astype(BF16))
    sn = head_sum(ikn_ref[...])
    qi = lax.broadcasted_iota(jnp.int32, sn.shape, 0)
    si = lax.broadcasted_iota(jnp.int32, sn.shape, 1)
    o_ref[:, past:] = jnp.where(si < t_len, jnp.where(si <= qi, sn, NEG), -jnp.inf)


def dec_scores(page_table, cache_idx_k, layer, iq, iw, ik_new, t_len):
    n_batch, n_pages = page_table.shape
    past = n_pages * PAGE_SIZE
    assert past % min(DEC_CHUNK, past) == 0
    rows = t_len * IDX_HEADS
    return pl.pallas_call(
        functools.partial(_dec_scores_kernel, layer=layer, t_len=t_len, n_pages=n_pages,
                          n_batch=n_batch),
        out_shape=jax.ShapeDtypeStruct((n_batch, t_len, past + LANES), F32),
        grid_spec=pltpu.PrefetchScalarGridSpec(
            num_scalar_prefetch=1,
            grid=(n_batch,),
            in_specs=[pl.BlockSpec(memory_space=pl.ANY),
                      pl.BlockSpec((None, rows, IDX_DIM), lambda b, pt: (b, 0, 0)),
                      pl.BlockSpec((None, rows, 1), lambda b, pt: (b, 0, 0)),
                      pl.BlockSpec((None, LANES, IDX_DIM), lambda b, pt: (b, 0, 0))],
            out_specs=pl.BlockSpec((None, t_len, past + LANES), lambda b, pt: (b, 0, 0)),
            scratch_shapes=[pltpu.VMEM((2, past, IDX_DIM), F32),
                            pltpu.SemaphoreType.DMA((2,))]),
        compiler_params=_cp(1),
        name="dec_scores",
    )(page_table, cache_idx_k, iq, iw, ik_new)


def _dec_topk_kernel(s_ref, o_ref, x_ref, *, n_sel):
    x_ref[...] = s_ref[...]
    blk = lax.broadcasted_iota(jnp.int32, x_ref.shape, 0)
    lane = lax.broadcasted_iota(jnp.int32, x_ref.shape, 2)
    key_id = blk * LANES + lane
    big = jnp.int32(2 ** 30)

    def body(j, c):
        x = x_ref[...]
        m = x.max(axis=0).max(axis=-1, keepdims=True)
        pick = jnp.where(x == m[None], key_id, big).min(axis=0).min(axis=-1, keepdims=True)
        x_ref[...] = jnp.where(key_id == pick[None], -jnp.inf, x)
        o_ref[j] = pick
        return c

    lax.fori_loop(0, n_sel, body, 0)


def dec_topk(s_blocks, n_sel):
    nb, rows, _ = s_blocks.shape
    return pl.pallas_call(
        functools.partial(_dec_topk_kernel, n_sel=n_sel),
        out_shape=jax.ShapeDtypeStruct((n_sel, rows, 1), jnp.int32),
        grid=(1,),
        in_specs=[pl.BlockSpec((nb, rows, LANES), lambda i: (0, 0, 0))],
        out_specs=pl.BlockSpec((n_sel, rows, 1), lambda i: (0, 0, 0)),
        scratch_shapes=[pltpu.VMEM((nb, rows, LANES), F32)],
        compiler_params=_cp(1),
        name="dec_topk",
    )(s_blocks)


def _dec_attn_kernel(idx_ref, pt_ref, ck_hbm, cv_hbm, kn_hbm, vn_hbm, q_ref, idxv_ref, rbt_ref, o_ref,
                     kbuf_ref, vbuf_ref, sem_ref, *, layer, t_len, n_pages, n_q, n_sel):
    r = pl.program_id(0)
    past = n_pages * PAGE_SIZE

    def issue(rr, slot):
        bb = rr // t_len

        def body(j, c):
            key = idx_ref[rr * n_sel + j]

            @pl.when(key < past)
            def _():
                phys = pt_ref[bb, key // PAGE_SIZE]
                row = key % PAGE_SIZE
                pltpu.make_async_copy(ck_hbm.at[layer, phys, row], kbuf_ref.at[slot, j],
                                      sem_ref.at[0, slot]).start()
                pltpu.make_async_copy(cv_hbm.at[layer, phys, row], vbuf_ref.at[slot, j],
                                      sem_ref.at[1, slot]).start()

            @pl.when(key >= past)
            def _():
                row = bb * t_len + jnp.minimum(key - past, t_len - 1)
                pltpu.make_async_copy(kn_hbm.at[row], kbuf_ref.at[slot, j], sem_ref.at[0, slot]).start()
                pltpu.make_async_copy(vn_hbm.at[row], vbuf_ref.at[slot, j], sem_ref.at[1, slot]).start()

            return c
        lax.fori_loop(0, n_sel, body, 0, unroll=4)

    @pl.when(r == 0)
    def _():
        issue(0, 0)

    @pl.when(r + 1 < n_q)
    def _():
        issue(r + 1, (r + 1) % 2)

    slot = r % 2
    pltpu.make_async_copy(kbuf_ref.at[slot], kbuf_ref.at[slot], sem_ref.at[0, slot]).wait()
    pltpu.make_async_copy(vbuf_ref.at[slot], vbuf_ref.at[slot], sem_ref.at[1, slot]).wait()

    qb = q_ref[...].astype(BF16)
    dist = (past + r % t_len) - idxv_ref[...]
    bucket = _rel_bucket(jnp.maximum(dist, 0))
    s_rows = []
    for h in range(N_HEADS):
        kh = kbuf_ref[slot, :, h, :].astype(BF16)
        s_rows.append(_dot_nt(qb, kh)[h:h + 1, :])
    s = jnp.concatenate(s_rows, axis=0)
    bias = jnp.zeros(s.shape, F32)
    for bkt in range(REL_BUCKETS):
        bias = jnp.where(bucket == bkt, rbt_ref[:, bkt:bkt + 1], bias)
    s = jnp.where(dist >= 0, s + bias, NEG)
    p = jnp.exp(s - s.max(axis=-1, keepdims=True))
    p = (p / p.sum(axis=-1, keepdims=True)).astype(BF16)
    for h in range(N_HEADS):
        vh = vbuf_ref[slot, :, h, :].astype(BF16)
        o_ref[h:h + 1, :] = _dot(p, vh)[h:h + 1, :]


def dec_attn(idx_flat, page_table, idx_rows, q, cache_k, cache_v, layer, k_new, v_new, rbt, t_len, n_sel):
    n_batch, n_pages = page_table.shape
    n_q = n_batch * t_len
    slab = (N_HEADS, HEAD_DIM)
    return pl.pallas_call(
        functools.partial(_dec_attn_kernel, layer=layer, t_len=t_len, n_pages=n_pages, n_q=n_q,
                          n_sel=n_sel),
        out_shape=jax.ShapeDtypeStruct((n_q,) + slab, F32),
        grid_spec=pltpu.PrefetchScalarGridSpec(
            num_scalar_prefetch=2,
            grid=(n_q,),
            in_specs=[pl.BlockSpec(memory_space=pl.ANY),
                      pl.BlockSpec(memory_space=pl.ANY),
                      pl.BlockSpec(memory_space=pl.ANY),
                      pl.BlockSpec(memory_space=pl.ANY),
                      pl.BlockSpec((None,) + slab, lambda r, ix, pt: (r, 0, 0)),
                      pl.BlockSpec((None, 1, n_sel), lambda r, ix, pt: (r, 0, 0)),
                      pl.BlockSpec((N_HEADS, REL_BUCKETS), lambda r, ix, pt: (0, 0))],
            out_specs=pl.BlockSpec((None,) + slab, lambda r, ix, pt: (r, 0, 0)),
            scratch_shapes=[pltpu.VMEM((2, n_sel) + slab, F32),
                            pltpu.VMEM((2, n_sel) + slab, F32),
                            pltpu.SemaphoreType.DMA((2, 2))]),
        compiler_params=_cp(1),
        name="dec_attn",
    )(idx_flat, page_table, cache_k, cache_v, k_new, v_new, q, idx_rows, rbt)


ATT_T = 256
IDX_TQ = 256
DEC_PAD = 16


def _attn_in_proj(xb, w_in, layer, tm, prompt_batches=0):
    if prompt_batches:
        q, = proj(xb, w_in, layer, 0, D_MODEL, 1024, tm, [BF16], scale=HEAD_DIM ** -0.5 * LOG2E,
                  name="proj_q")
        k, kb = proj(xb, w_in, layer, D_MODEL, D_MODEL, 1024, tm, [F32], name="proj_k",
                     transposed_batches=prompt_batches)
        v, vb = proj(xb, w_in, layer, 2 * D_MODEL, D_MODEL, 1024, tm, [F32, BF16], name="proj_v")
    else:
        q, = proj(xb, w_in, layer, 0, D_MODEL, 1024, tm, [F32], scale=HEAD_DIM ** -0.5, name="proj_q")
        k, = proj(xb, w_in, layer, D_MODEL, D_MODEL, 1024, tm, [F32], name="proj_k")
        v, = proj(xb, w_in, layer, 2 * D_MODEL, D_MODEL, 1024, tm, [F32], name="proj_v")
        kb = vb = None
    iq, = proj(xb, w_in, layer, 3 * D_MODEL, IDX_HEADS * IDX_DIM, 1024, tm, [BF16], name="proj_iq")
    o4 = 3 * D_MODEL + IDX_HEADS * IDX_DIM
    w_tail = jnp.pad(w_in[layer, :, o4:], ((0, 0), (0, 2 * LANES - (IDX_DIM + IDX_HEADS))))
    tail, = proj(xb, w_tail, 0, 0, 2 * LANES, 2 * LANES, tm, [F32], name="proj_tail")
    ik = tail[:, :IDX_DIM]
    iw = tail[:, IDX_DIM:IDX_DIM + IDX_HEADS] * IDX_HEADS ** -0.5
    return q, k, v, kb, vb, iq, ik, iw


def _attn_prompt(xb, w_in, w_out_args, layer, rel_tiles, n_batch):
    m = xb.shape[0]
    t_len = m // n_batch
    q, k, v, kt, vb, iq, ik, iw = _attn_in_proj(xb, w_in, layer, 1024, prompt_batches=n_batch)
    ikt = jnp.swapaxes(ik.astype(BF16).reshape(n_batch, t_len, IDX_DIM), 1, 2)
    mask = idx_mask(iq, iw, ikt, n_batch, IDX_TQ)
    o = attn(q, kt, vb, mask, rel_tiles, n_batch, ATT_T)
    k = k.reshape(n_batch, t_len, N_HEADS, HEAD_DIM)
    v = v.reshape(n_batch, t_len, N_HEADS, HEAD_DIM)
    return o, k, v, ik.reshape(n_batch, t_len, IDX_DIM)


def _attn_sample(xb, w_in, layer, cache_k, cache_v, cache_idx_k, page_table, rbt, n_batch):
    m = xb.shape[0]
    t_len = m // n_batch
    q, k_new, v_new, _, _, iq, ik, iw = _attn_in_proj(xb, w_in, layer, m)
    iq_r = iq.reshape(n_batch, t_len * IDX_HEADS, IDX_DIM)
    iw_r = iw.reshape(n_batch, t_len * IDX_HEADS, 1)
    ik_pad = jnp.pad(ik.reshape(n_batch, t_len, IDX_DIM), ((0, 0), (0, LANES - t_len), (0, 0)))
    sc = dec_scores(page_table, cache_idx_k, layer, iq_r, iw_r, ik_pad.astype(BF16), t_len)
    n_keys = sc.shape[-1]
    n_sel = min(TOPK_MAX, (n_keys - LANES + t_len) // 4)
    s_blocks = jnp.transpose(sc.reshape(m, n_keys // LANES, LANES), (1, 0, 2))
    sel = dec_topk(s_blocks, n_sel)
    sel = jnp.swapaxes(sel[:, :, 0], 0, 1)
    slab = (m, N_HEADS, HEAD_DIM)
    o = dec_attn(sel.reshape(-1), page_table, sel[:, None, :], q.reshape(slab), cache_k, cache_v, layer,
                 k_new.reshape(slab), v_new.reshape(slab), rbt, t_len, n_sel)
    return (o.reshape(m, D_MODEL).astype(BF16), k_new.reshape(n_batch, t_len, N_HEADS, HEAD_DIM),
            v_new.reshape(n_batch, t_len, N_HEADS, HEAD_DIM), ik.reshape(n_batch, t_len, IDX_DIM))


def _rot_tables(pos):
    half = RET_DK // 2
    inv = jnp.exp(-math.log(ROPE_BASE) * jnp.arange(half, dtype=F32) / half)
    ang = pos.astype(F32)[:, None] * inv[None, :]
    return jnp.cos(ang), jnp.sin(ang)


def _retention_mixer(xb, w_in, layer, gn, s0, lg, cos, sin, n_batch, chunk, pad_to):
    m = xb.shape[0]
    t_len = m // n_batch
    tm = min(1024, m)
    qk = proj_rot(xb, w_in, layer, cos, sin, 1024, tm)
    v, = proj(xb, w_in, layer, 2 * D_MODEL, RET_VW, 1024, tm, [BF16], name="proj_rv")
    g, = proj(xb, w_in, layer, 2 * D_MODEL + RET_VW, RET_VW, 1024, tm, [F32], name="proj_rg")
    if pad_to != t_len:
        padr = lambda a: jnp.pad(a.reshape(n_batch, t_len, -1),
                                 ((0, 0), (0, pad_to - t_len), (0, 0))).reshape(n_batch * pad_to, -1)
        z, s_fin = retention(lg, padr(qk), padr(v), padr(g), gn, s0, n_batch, pad_to, chunk)
        z = z.reshape(n_batch, pad_to, RET_VW)[:, :t_len].reshape(m, RET_VW)
    else:
        z, s_fin = retention(lg, qk, v, g, gn, s0, n_batch, chunk, chunk)
    return z, s_fin


MOE_TILE = 512
MOE_GU_TN = D_FF_EXPERT // 2


def _moe_sparse(xf, xf_x, w_router_pad, w_gu, w_down, layer, g, b):
    m, mx = xf.shape[0], xf_x.shape[0]
    meta, cnt = router(xf, w_router_pad, 512)
    meta_x, cnt_x = router(xf_x, w_router_pad, 512)
    col = lambda mt, c: mt[:, c].astype(jnp.int32)
    counts_main = cnt[0, :N_EXPERTS].astype(jnp.int32)
    counts = counts_main + cnt_x[0, :N_EXPERTS].astype(jnp.int32)
    n_tile = (counts + MOE_TILE - 1) // MOE_TILE
    end_tile = jnp.cumsum(n_tile)
    start_row = (end_tile - n_tile) * MOE_TILE
    start_x = start_row + counts_main
    pos1 = start_row[col(meta, META_E1)] + col(meta, META_R1)
    pos2 = start_row[col(meta, META_E2)] + col(meta, META_R2)
    pos1_x = start_x[col(meta_x, META_E1)] + col(meta_x, META_R1)
    pos2_x = start_x[col(meta_x, META_E2)] + col(meta_x, META_R2)
    max_tiles = (2 * (m + mx) + N_EXPERTS * (MOE_TILE - 1)) // MOE_TILE
    tiles = jnp.arange(max_tiles, dtype=jnp.int32)
    tile_expert = jnp.minimum(jnp.sum(tiles[:, None] >= end_tile[None, :], axis=1), N_EXPERTS - 1)
    n_valid = end_tile[-1:].astype(jnp.int32)
    xg = moe_dispatch(pos1, pos2, pos1_x, pos2_x, start_row + counts, end_tile * MOE_TILE, n_valid,
                      xf, xf_x, MOE_TILE, max_tiles)
    te = tile_expert.astype(jnp.int32)
    hg = moe_gu_grouped(te, n_valid, xg, w_gu, layer, MOE_TILE, MOE_GU_TN)
    yg = moe_down_grouped(te, n_valid, hg, w_down, layer, MOE_TILE, 1024)
    return (moe_combine_ln(pos1, pos2, yg, meta, xf, g, b, 256),
            moe_combine_ln(pos1_x, pos2_x, yg, meta_x, xf_x, g, b, 256))


def kernel(x_prompt, x_sample, cache_k, cache_v, cache_idx_k, state_ret, page_table, rel_bias,
           w_in_attn, w_out_attn, w_in_ret, ret_gn_g, w_out_ret, w_ffn_gu, w_ffn_down,
           w_router, w_exp_gu, w_exp_down, ln_g, ln_b):
    bp, tp, _ = x_prompt.shape
    bs, ts, _ = x_sample.shape
    past = page_table.shape[1] * PAGE_SIZE
    mp, ms = bp * tp, bs * ts
    xp = x_prompt.reshape(mp, D_MODEL)
    xs = x_sample.reshape(ms, D_MODEL)
    xpb, xsb = xp.astype(BF16), xs.astype(BF16)

    rel_tiles = bias_tiles(rel_bias, ATT_T)
    rbt = rel_bias.T
    w_out_attn, w_out_ret, w_ffn_down = (w.astype(BF16) for w in (w_out_attn, w_out_ret, w_ffn_down))
    lg = jnp.log1p(-jnp.exp2(-5.0 - jnp.arange(RET_HEADS, dtype=F32)))
    cos_p, sin_p = _rot_tables(jnp.tile(jnp.arange(tp, dtype=jnp.int32), bp))
    cos_s, sin_s = _rot_tables(jnp.tile(past + jnp.arange(ts, dtype=jnp.int32), bs))
    zero_state = jnp.zeros((bp, RET_HEADS, RET_DK, RET_DV), F32)

    kp_l, vp_l, ikp_l, sp_l = [], [], [], []
    ks_l, vs_l, iks_l, ss_l = [], [], [], []
    for i in range(DEPTH):
        j = i // 2
        g0, b0 = ln_g[i, 0][None, :], ln_b[i, 0][None, :]
        g1, b1 = ln_g[i, 1][None, :], ln_b[i, 1][None, :]
        if i % 2 == 0:
            op, kp, vp, ikp = _attn_prompt(xpb, w_in_attn, None, j, rel_tiles, bp)
            os_, k_s, v_s, iks = _attn_sample(xsb, w_in_attn, j, cache_k, cache_v, cache_idx_k,
                                              page_table, rbt, bs)
            kp_l.append(kp); vp_l.append(vp); ikp_l.append(ikp)
            ks_l.append(k_s); vs_l.append(v_s); iks_l.append(iks)
            xp, xpb = down_ln(op, w_out_attn, j, xp, g0, b0, 512, 512, D_MODEL)
            xs, xsb = down_ln(os_, w_out_attn, j, xs, g0, b0, 512, 512, D_MODEL)
        else:
            gn = ret_gn_g[j][None, :]
            zp, sp = _retention_mixer(xpb, w_in_ret, j, gn, zero_state, lg, cos_p, sin_p,
                                      bp, min(RET_CHUNK, tp), tp)
            zs, ss = _retention_mixer(xsb, w_in_ret, j, gn, state_ret[j], lg, cos_s, sin_s,
                                      bs, ts, DEC_PAD)
            sp_l.append(sp); ss_l.append(ss)
            xp, xpb = down_ln(zp, w_out_ret, j, xp, g0, b0, 512, 512, RET_VW // 2)
            xs, xsb = down_ln(zs, w_out_ret, j, xs, g0, b0, 512, 512, RET_VW // 2)
        if i % 2 == 0:
            hp = ffn_gu(xpb, w_ffn_gu, j, 512, 1024)
            hs = ffn_gu(xsb, w_ffn_gu, j, 512, 1024)
            xp, xpb = down_ln(hp, w_ffn_down, j, xp, g1, b1, 512, 512, D_FF // 2)
            xs, xsb = down_ln(hs, w_ffn_down, j, xs, g1, b1, 512, 512, D_FF // 2)
        else:
            w_r = jnp.pad(w_router[j], ((0, 0), (0, LANES - N_EXPERTS)))
            (xp, xpb), (xs, xsb) = _moe_sparse(xp, xs, w_r, w_exp_gu, w_exp_down, j, g1, b1)

    return (xp.reshape(bp, tp, D_MODEL), xs.reshape(bs, ts, D_MODEL),
            jnp.stack(kp_l), jnp.stack(vp_l), jnp.stack(ikp_l), jnp.stack(sp_l),
            jnp.stack(ks_l), jnp.stack(vs_l), jnp.stack(iks_l), jnp.stack(ss_l))
```

```python
import functools
import math

import jax
import jax.numpy as jnp
import numpy as np
from jax import lax
from jax.experimental import pallas as pl
from jax.experimental.pallas import tpu as pltpu

F32 = jnp.float32
BF16 = jnp.bfloat16

D_MODEL = 2048
PAGE_SIZE = 128
N_HEADS = 16
HEAD_DIM = 128
IDX_HEADS = 16
IDX_DIM = 128
TOPK_MAX = 256
REL_BUCKETS = 32
REL_MAX_DIST = 128
RET_HEADS = 8
RET_DK = 256
RET_DV = 512
RET_VW = RET_HEADS * RET_DV
RET_CHUNK = 128
ROPE_BASE = 10000.0
D_FF = 5632
N_EXPERTS = 8
D_FF_EXPERT = 2816
DEPTH = 4
DN_ALPHA = (2 * DEPTH) ** 0.25
LN_EPS = 1e-5
NEG = -1e30
LOG2E = math.log2(math.e)

LANES = 128
VMEM_LIMIT = 56 * 1024 * 1024
INT_MIN = -2 ** 31
NEG_KEY = int(np.float32(NEG).view(np.int32)) ^ 0x7FFFFFFF


def _cp(n_axes, vmem=VMEM_LIMIT):
    return pltpu.CompilerParams(dimension_semantics=("arbitrary",) * n_axes,
                                vmem_limit_bytes=vmem)


def _silu(x):
    return x * (1.0 / (1.0 + jnp.exp(-x)))


def _dot(a, b):
    return jnp.dot(a, b, preferred_element_type=F32)


def _dot_nt(a, b):
    return lax.dot_general(a, b, (((1,), (1,)), ((), ())), preferred_element_type=F32)


def _dot_tn(a, b):
    return lax.dot_general(a, b, (((0,), (0,)), ((), ())), preferred_element_type=F32)


def _proj_kernel(a_ref, w_ref, *rest, n_out, scale, transposed):
    out_refs = rest[:n_out]
    wb_ref = rest[-1]

    @pl.when(pl.program_id(1) == 0)
    def _():
        wb_ref[...] = w_ref[...].astype(BF16)

    acc = _dot(a_ref[...], wb_ref[...])
    if scale != 1.0:
        acc = acc * scale
    for o in out_refs:
        o[...] = acc.astype(o.dtype)
    if transposed:
        rest[n_out][...] = acc.T.astype(BF16)


def proj(a, w, layer, col0, ncols, tn, tm, out_dtypes, scale=1.0, name="proj", transposed_batches=0):
    m, k = a.shape
    tm = min(tm, m)
    assert col0 % tn == 0 and ncols % tn == 0 and m % tm == 0
    off = col0 // tn
    if w.ndim == 3:
        w_spec = pl.BlockSpec((None, k, tn), lambda j, i: (layer, 0, j + off))
    else:
        w_spec = pl.BlockSpec((k, tn), lambda j, i: (0, j + off))
    out_shape = [jax.ShapeDtypeStruct((m, ncols), dt) for dt in out_dtypes]
    out_specs = [pl.BlockSpec((tm, tn), lambda j, i: (i, j)) for _ in out_dtypes]
    if transposed_batches:
        per = m // transposed_batches // tm
        out_shape.append(jax.ShapeDtypeStruct((transposed_batches, ncols, m // transposed_batches), BF16))
        out_specs.append(pl.BlockSpec((None, tn, tm), lambda j, i: (i // per, j, i % per)))
    outs = pl.pallas_call(
        functools.partial(_proj_kernel, n_out=len(out_dtypes), scale=scale,
                          transposed=bool(transposed_batches)),
        out_shape=out_shape,
        grid=(ncols // tn, m // tm),
        in_specs=[pl.BlockSpec((tm, k), lambda j, i: (i, 0)), w_spec],
        out_specs=out_specs,
        scratch_shapes=[pltpu.VMEM((k, tn), BF16)],
        compiler_params=_cp(2),
        name=name,
    )(a, w)
    return outs


def _proj_rot_kernel(a_ref, w_ref, cos_ref, sin_ref, o_ref, wb_ref, *, heads_per_tile, q_tiles):
    j = pl.program_id(0)

    @pl.when(pl.program_id(1) == 0)
    def _():
        wb_ref[...] = w_ref[...].astype(BF16)

    acc = _dot(a_ref[...], wb_ref[...])
    scale = jnp.where(j >= q_tiles, RET_DK ** -0.5, 1.0).astype(F32)
    c = cos_ref[...]
    s = sin_ref[...]
    half = RET_DK // 2
    for hh in range(heads_per_tile):
        x1 = acc[:, hh * RET_DK: hh * RET_DK + half]
        x2 = acc[:, hh * RET_DK + half: (hh + 1) * RET_DK]
        o_ref[:, hh * RET_DK: hh * RET_DK + half] = ((x1 * c - x2 * s) * scale).astype(BF16)
        o_ref[:, hh * RET_DK + half: (hh + 1) * RET_DK] = ((x1 * s + x2 * c) * scale).astype(BF16)


def proj_rot(a, w, layer, cos, sin, tn, tm):
    m, k = a.shape
    tm = min(tm, m)
    ncols = 2 * D_MODEL
    return pl.pallas_call(
        functools.partial(_proj_rot_kernel, heads_per_tile=tn // RET_DK, q_tiles=D_MODEL // tn),
        out_shape=jax.ShapeDtypeStruct((m, ncols), BF16),
        grid=(ncols // tn, m // tm),
        in_specs=[pl.BlockSpec((tm, k), lambda j, i: (i, 0)),
                  pl.BlockSpec((None, k, tn), lambda j, i: (layer, 0, j)),
                  pl.BlockSpec((tm, RET_DK // 2), lambda j, i: (i, 0)),
                  pl.BlockSpec((tm, RET_DK // 2), lambda j, i: (i, 0))],
        out_specs=pl.BlockSpec((tm, tn), lambda j, i: (i, j)),
        scratch_shapes=[pltpu.VMEM((k, tn), BF16)],
        compiler_params=_cp(2),
        name="proj_rot",
    )(a, w, cos, sin)


def _gu_kernel(a_ref, wg_ref, wu_ref, o_ref, wgb_ref, wub_ref):
    @pl.when(pl.program_id(1) == 0)
    def _():
        wgb_ref[...] = wg_ref[...].astype(BF16)
        wub_ref[...] = wu_ref[...].astype(BF16)

    a = a_ref[...]
    o_ref[...] = (_silu(_dot(a, wgb_ref[...])) * _dot(a, wub_ref[...])).astype(BF16)


def ffn_gu(a, w_gu, layer, tn, tm):
    m, k = a.shape
    tm = min(tm, m)
    nj = D_FF // tn
    return pl.pallas_call(
        _gu_kernel,
        out_shape=jax.ShapeDtypeStruct((m, D_FF), BF16),
        grid=(nj, m // tm),
        in_specs=[pl.BlockSpec((tm, k), lambda j, i: (i, 0)),
                  pl.BlockSpec((None, k, tn), lambda j, i: (layer, 0, j)),
                  pl.BlockSpec((None, k, tn), lambda j, i: (layer, 0, j + nj))],
        out_specs=pl.BlockSpec((tm, tn), lambda j, i: (i, j)),
        scratch_shapes=[pltpu.VMEM((k, tn), BF16), pltpu.VMEM((k, tn), BF16)],
        compiler_params=_cp(2),
        name="ffn_gu",
    )(a, w_gu, w_gu)


def _down_ln_kernel(a_ref, w_ref, x_ref, g_ref, b_ref, of_ref, ob_ref, row_ref, *, nj, nk, tn):
    k = pl.program_id(1)
    j = pl.program_id(2)
    y = _dot(a_ref[...], w_ref[...].astype(BF16))

    @pl.when(k == 0)
    def _():
        row_ref[j] = y

    @pl.when(k > 0)
    def _():
        row_ref[j] = row_ref[j] + y

    @pl.when((j == nj - 1) & (k == nk - 1))
    def _():
        zs = [DN_ALPHA * x_ref[:, jj * tn:(jj + 1) * tn] + row_ref[jj] for jj in range(nj)]
        tot = zs[0].sum(axis=-1, keepdims=True)
        for z in zs[1:]:
            tot = tot + z.sum(axis=-1, keepdims=True)
        mu = tot * (1.0 / D_MODEL)
        sq = None
        for z in zs:
            t = jnp.square(z - mu).sum(axis=-1, keepdims=True)
            sq = t if sq is None else sq + t
        rstd = lax.rsqrt(sq * (1.0 / D_MODEL) + LN_EPS)
        for jj, z in enumerate(zs):
            sl = slice(jj * tn, (jj + 1) * tn)
            o = (z - mu) * rstd * g_ref[:, sl] + b_ref[:, sl]
            of_ref[:, sl] = o
            ob_ref[:, sl] = o.astype(BF16)


def down_ln(a, w, layer, x, g, b, tm, tn, tk):
    m, kk = a.shape
    tm = min(tm, m)
    n = D_MODEL
    assert kk % tk == 0 and n % tn == 0 and m % tm == 0
    nj, nk = n // tn, kk // tk
    return pl.pallas_call(
        functools.partial(_down_ln_kernel, nj=nj, nk=nk, tn=tn),
        out_shape=[jax.ShapeDtypeStruct((m, n), F32), jax.ShapeDtypeStruct((m, n), BF16)],
        grid=(m // tm, nk, nj),
        in_specs=[pl.BlockSpec((tm, tk), lambda i, k, j: (i, k)),
                  pl.BlockSpec((None, tk, tn), lambda i, k, j: (layer, k, j)),
                  pl.BlockSpec((tm, n), lambda i, k, j: (i, 0)),
                  pl.BlockSpec((1, n), lambda i, k, j: (0, 0)),
                  pl.BlockSpec((1, n), lambda i, k, j: (0, 0))],
        out_specs=[pl.BlockSpec((tm, n), lambda i, k, j: (i, 0)),
                   pl.BlockSpec((tm, n), lambda i, k, j: (i, 0))],
        scratch_shapes=[pltpu.VMEM((nj, tm, tn), F32)],
        compiler_params=_cp(3),
        name="down_ln",
    )(a, w, x, g, b)


def _router_kernel(x_ref, w_ref, o_ref, cnt_ref, carry_ref):
    x = x_ref[...]
    w = w_ref[...]
    xh = x.astype(BF16)
    xl = (x - xh.astype(F32)).astype(BF16)
    wh = w.astype(BF16)
    wl = (w - wh.astype(F32)).astype(BF16)
    lg = _dot(xh, wh) + (_dot(xh, wl) + _dot(xl, wh))
    lane = lax.broadcasted_iota(jnp.int32, lg.shape, 1)
    lg = jnp.where(lane < N_EXPERTS, lg, -jnp.inf)
    m1 = lg.max(axis=-1, keepdims=True)
    i1 = jnp.where(lg == m1, lane, LANES).min(axis=-1, keepdims=True)
    lg2 = jnp.where(lane == i1, -jnp.inf, lg)
    m2 = lg2.max(axis=-1, keepdims=True)
    i2 = jnp.where(lg2 == m2, lane, LANES).min(axis=-1, keepdims=True)
    e2 = jnp.exp(m2 - m1)
    den = 1.0 + e2
    g1 = 1.0 / den
    g2 = e2 / den
    @pl.when(pl.program_id(0) == 0)
    def _():
        carry_ref[...] = jnp.zeros(carry_ref.shape, F32)

    tm = lg.shape[0]
    hit = jnp.where(lane == i1, 1.0, jnp.where(lane == i2, 1.0, 0.0))
    rr = lax.broadcasted_iota(jnp.int32, (tm, tm), 0)
    cc = lax.broadcasted_iota(jnp.int32, (tm, tm), 1)
    tri = jnp.where(cc < rr, 1.0, 0.0).astype(BF16)
    before = _dot(tri, hit.astype(BF16)) + carry_ref[...]
    r1 = jnp.where(lane == i1, before, 0.0).sum(axis=-1, keepdims=True)
    r2 = jnp.where(lane == i2, before, 0.0).sum(axis=-1, keepdims=True)
    carry_ref[...] = carry_ref[...] + hit.sum(axis=0, keepdims=True)
    cnt_ref[...] = carry_ref[...]
    meta = jnp.where(lane == i1, g1, 0.0) + jnp.where(lane == i2, g2, 0.0)
    for col, val in ((META_E1, i1.astype(F32)), (META_E2, i2.astype(F32)), (META_G1, g1),
                     (META_G2, g2), (META_R1, r1), (META_R2, r2)):
        meta = jnp.where(lane == col, val, meta)
    o_ref[...] = meta


META_E1, META_E2, META_G1, META_G2, META_R1, META_R2 = 8, 9, 10, 11, 12, 13


def router(x, w_pad, tm):
    m, k = x.shape
    tm = min(tm, m)
    return pl.pallas_call(
        _router_kernel,
        out_shape=[jax.ShapeDtypeStruct((m, LANES), F32), jax.ShapeDtypeStruct((1, LANES), F32)],
        grid=(m // tm,),
        in_specs=[pl.BlockSpec((tm, k), lambda i: (i, 0)),
                  pl.BlockSpec((k, LANES), lambda i: (0, 0))],
        out_specs=[pl.BlockSpec((tm, LANES), lambda i: (i, 0)),
                   pl.BlockSpec((1, LANES), lambda i: (0, 0))],
        scratch_shapes=[pltpu.VMEM((1, LANES), F32)],
        compiler_params=_cp(1),
        name="router",
    )(x, w_pad)


def _dispatch_kernel(p1_ref, p2_ref, s1_ref, s2_ref, lo_ref, hi_ref, nv_ref, x_hbm, xs_hbm, o_ref,
                     src_ref, buf_ref, sem_ref, fix_sem_ref, *, m, ms, tile, n_tiles):
    t = pl.program_id(0)
    nv = nv_ref[0]

    @pl.when(t == 0)
    def _():
        for e in range(N_EXPERTS):
            def clear(i, c):
                src_ref[i] = 0
                return c
            lax.fori_loop(lo_ref[e], hi_ref[e], clear, 0)

        def invert(n, c):
            src_ref[p1_ref[n]] = n
            src_ref[p2_ref[n]] = n
            return c
        lax.fori_loop(0, m, invert, 0, unroll=8)

        def extra(n, c):
            src_ref[s1_ref[n]] = 0
            src_ref[s2_ref[n]] = 0
            return c
        lax.fori_loop(0, ms, extra, 0)

    def issue(tt, slot):
        def body(i, c):
            tok = src_ref[tt * tile + i]
            pltpu.make_async_copy(x_hbm.at[pl.ds(tok, 1)], buf_ref.at[slot, pl.ds(i, 1)],
                                  sem_ref.at[slot]).start()
            return c
        lax.fori_loop(0, tile, body, 0, unroll=8)

    @pl.when(t == 0)
    def _():
        issue(0, 0)

    @pl.when(t + 1 < nv)
    def _():
        issue(t + 1, (t + 1) % 2)

    slot = t % 2

    @pl.when(t < nv)
    def _():
        pltpu.make_async_copy(buf_ref.at[slot], buf_ref.at[slot], sem_ref.at[slot]).wait()

        def fix_row(n, pos, k):
            hit = (pos >= t * tile) & (pos < (t + 1) * tile)

            @pl.when(hit)
            def _():
                pltpu.make_async_copy(xs_hbm.at[pl.ds(n, 1)], buf_ref.at[slot, pl.ds(pos - t * tile, 1)],
                                      fix_sem_ref.at[0]).start()
            return k + jnp.where(hit, 1, 0)

        def fix_issue(n, k):
            return fix_row(n, s2_ref[n], fix_row(n, s1_ref[n], k))
        n_fix = lax.fori_loop(0, ms, fix_issue, 0)

        def fix_wait(i, c):
            pltpu.make_async_copy(xs_hbm.at[pl.ds(0, 1)], buf_ref.at[slot, pl.ds(0, 1)],
                                  fix_sem_ref.at[0]).wait()
            return c
        lax.fori_loop(0, n_fix, fix_wait, 0)
        o_ref[...] = buf_ref[slot].astype(BF16)

    @pl.when(t >= nv)
    def _():
        o_ref[...] = jnp.zeros(o_ref.shape, o_ref.dtype)


def moe_dispatch(pos1, pos2, pos1_x, pos2_x, pad_lo, pad_hi, n_valid, x, x_extra, tile, n_tiles):
    m, d = x.shape
    return pl.pallas_call(
        functools.partial(_dispatch_kernel, m=m, ms=x_extra.shape[0], tile=tile, n_tiles=n_tiles),
        out_shape=jax.ShapeDtypeStruct((n_tiles * tile, d), BF16),
        grid_spec=pltpu.PrefetchScalarGridSpec(
            num_scalar_prefetch=7,
            grid=(n_tiles,),
            in_specs=[pl.BlockSpec(memory_space=pl.ANY), pl.BlockSpec(memory_space=pl.ANY)],
            out_specs=pl.BlockSpec((tile, d), lambda t, *_: (t, 0)),
            scratch_shapes=[pltpu.SMEM((n_tiles * tile,), jnp.int32),
                            pltpu.VMEM((2, tile, d), F32),
                            pltpu.SemaphoreType.DMA((2,)),
                            pltpu.SemaphoreType.DMA((1,))]),
        compiler_params=_cp(1),
        name="moe_dispatch",
    )(pos1, pos2, pos1_x, pos2_x, pad_lo, pad_hi, n_valid, x, x_extra)


def _tile_ids(t, te_ref, nv_ref):
    nv = nv_ref[0]
    cur = te_ref[jnp.minimum(t, nv - 1)]
    prev = te_ref[jnp.minimum(jnp.maximum(t - 1, 0), nv - 1)]
    return nv, (t == 0) | (cur != prev)


def _ggu_kernel(te_ref, nv_ref, a_ref, wg_ref, wu_ref, o_ref, wgb_ref, wub_ref):
    t = pl.program_id(1)
    nv, new_expert = _tile_ids(t, te_ref, nv_ref)

    @pl.when(new_expert)
    def _():
        wgb_ref[...] = wg_ref[...].astype(BF16)
        wub_ref[...] = wu_ref[...].astype(BF16)

    @pl.when(t < nv)
    def _():
        a = a_ref[...]
        o_ref[...] = (_silu(_dot(a, wgb_ref[...])) * _dot(a, wub_ref[...])).astype(BF16)

    @pl.when(t >= nv)
    def _():
        o_ref[...] = jnp.zeros(o_ref.shape, o_ref.dtype)


def moe_gu_grouped(tile_expert, n_valid, a, w_gu, layer, tile, tn):
    r, k = a.shape
    nj = D_FF_EXPERT // tn
    row = lambda t, nv: jnp.minimum(t, nv[0] - 1)
    return pl.pallas_call(
        _ggu_kernel,
        out_shape=jax.ShapeDtypeStruct((r, D_FF_EXPERT), BF16),
        grid_spec=pltpu.PrefetchScalarGridSpec(
            num_scalar_prefetch=2,
            grid=(nj, r // tile),
            in_specs=[pl.BlockSpec((tile, k), lambda j, t, te, nv: (row(t, nv), 0)),
                      pl.BlockSpec((None, None, k, tn),
                                   lambda j, t, te, nv: (layer, te[row(t, nv)], 0, j),
                                   pipeline_mode=pl.Buffered(1)),
                      pl.BlockSpec((None, None, k, tn),
                                   lambda j, t, te, nv: (layer, te[row(t, nv)], 0, j + nj),
                                   pipeline_mode=pl.Buffered(1))],
            out_specs=pl.BlockSpec((tile, tn), lambda j, t, te, nv: (t, j)),
            scratch_shapes=[pltpu.VMEM((k, tn), BF16), pltpu.VMEM((k, tn), BF16)]),
        compiler_params=_cp(2),
        name="moe_gu_grouped",
    )(tile_expert, n_valid, a, w_gu, w_gu)


def _gdown_kernel(te_ref, nv_ref, a_ref, w_ref, o_ref, wb_ref):
    t = pl.program_id(1)
    nv, new_expert = _tile_ids(t, te_ref, nv_ref)

    @pl.when(new_expert)
    def _():
        wb_ref[...] = w_ref[...].astype(BF16)

    @pl.when(t < nv)
    def _():
        o_ref[...] = _dot(a_ref[...], wb_ref[...])

    @pl.when(t >= nv)
    def _():
        o_ref[...] = jnp.zeros(o_ref.shape, o_ref.dtype)


def moe_down_grouped(tile_expert, n_valid, a, w_down, layer, tile, tn):
    r, k = a.shape
    row = lambda t, nv: jnp.minimum(t, nv[0] - 1)
    return pl.pallas_call(
        _gdown_kernel,
        out_shape=jax.ShapeDtypeStruct((r, D_MODEL), F32),
        grid_spec=pltpu.PrefetchScalarGridSpec(
            num_scalar_prefetch=2,
            grid=(D_MODEL // tn, r // tile),
            in_specs=[pl.BlockSpec((tile, k), lambda j, t, te, nv: (row(t, nv), 0)),
                      pl.BlockSpec((None, None, k, tn),
                                   lambda j, t, te, nv: (layer, te[row(t, nv)], 0, j),
                                   pipeline_mode=pl.Buffered(1))],
            out_specs=pl.BlockSpec((tile, tn), lambda j, t, te, nv: (t, j)),
            scratch_shapes=[pltpu.VMEM((k, tn), BF16)]),
        compiler_params=_cp(2),
        name="moe_down_grouped",
    )(tile_expert, n_valid, a, w_down)


def _combine_kernel(p1_ref, p2_ref, y_hbm, meta_ref, x_ref, g_ref, b_ref, of_ref, ob_ref,
                    ybuf_ref, sem_ref, *, tm, n_tiles):
    i = pl.program_id(0)

    def issue(ii, slot):
        def body(r, c):
            n = ii * tm + r
            pltpu.make_async_copy(y_hbm.at[pl.ds(p1_ref[n], 1)], ybuf_ref.at[slot, 0, pl.ds(r, 1)],
                                  sem_ref.at[slot]).start()
            pltpu.make_async_copy(y_hbm.at[pl.ds(p2_ref[n], 1)], ybuf_ref.at[slot, 1, pl.ds(r, 1)],
                                  sem_ref.at[slot]).start()
            return c
        lax.fori_loop(0, tm, body, 0)

    @pl.when(i == 0)
    def _():
        issue(0, 0)

    @pl.when(i + 1 < n_tiles)
    def _():
        issue(i + 1, (i + 1) % 2)

    slot = i % 2
    pltpu.make_async_copy(ybuf_ref.at[slot], ybuf_ref.at[slot], sem_ref.at[slot]).wait()
    meta = meta_ref[...]
    lane = lax.broadcasted_iota(jnp.int32, meta.shape, 1)
    g1 = jnp.where(lane == META_G1, meta, 0.0).sum(axis=-1, keepdims=True)
    g2 = jnp.where(lane == META_G2, meta, 0.0).sum(axis=-1, keepdims=True)
    z = DN_ALPHA * x_ref[...] + (g1 * ybuf_ref[slot, 0] + g2 * ybuf_ref[slot, 1])
    mu = z.mean(axis=-1, keepdims=True)
    var = jnp.square(z - mu).mean(axis=-1, keepdims=True)
    o = (z - mu) * lax.rsqrt(var + LN_EPS) * g_ref[...] + b_ref[...]
    of_ref[...] = o
    ob_ref[...] = o.astype(BF16)


def moe_combine_ln(pos1, pos2, y, meta, x, g, b, tm):
    m, d = x.shape
    tm = min(tm, m)
    n_tiles = m // tm
    tok = lambda i, p1, p2: (i, 0)
    vec = lambda i, p1, p2: (0, 0)
    return pl.pallas_call(
        functools.partial(_combine_kernel, tm=tm, n_tiles=n_tiles),
        out_shape=[jax.ShapeDtypeStruct((m, d), F32), jax.ShapeDtypeStruct((m, d), BF16)],
        grid_spec=pltpu.PrefetchScalarGridSpec(
            num_scalar_prefetch=2,
            grid=(n_tiles,),
            in_specs=[pl.BlockSpec(memory_space=pl.ANY),
                      pl.BlockSpec((tm, LANES), tok),
                      pl.BlockSpec((tm, d), tok),
                      pl.BlockSpec((1, d), vec),
                      pl.BlockSpec((1, d), vec)],
            out_specs=[pl.BlockSpec((tm, d), tok), pl.BlockSpec((tm, d), tok)],
            scratch_shapes=[pltpu.VMEM((2, 2, tm, d), F32),
                            pltpu.SemaphoreType.DMA((2,))]),
        compiler_params=_cp(1),
        name="moe_combine_ln",
    )(pos1, pos2, y, meta, x, g, b)


def _sort_key(x):
    k = pltpu.bitcast(x, jnp.int32)
    return jnp.where(k < 0, k ^ jnp.int32(0x7FFFFFFF), k)


def _kth_largest_key(count_ge, n_sel, shape):
    zero = jnp.zeros(shape, jnp.int32)
    t0 = jnp.where(count_ge(zero) >= n_sel, zero, jnp.full(shape, INT_MIN, jnp.int32))

    def body(i, t):
        cand = t | jnp.left_shift(jnp.int32(1), 30 - i)
        return jnp.where(count_ge(cand) >= n_sel, cand, t)

    return lax.fori_loop(0, 31, body, t0)


IDX_GROUPS = 8


def _idx_mask_kernel(iq_ref, iw_ref, ikt_ref, o_ref, key_ref, *, tq, n_sel):
    i = pl.program_id(1)
    s_len = ikt_ref.shape[1]
    per = s_len // tq // IDX_GROUPS
    wf = iw_ref[...] * (IDX_DIM ** -0.5)

    def run(s_eff):
        ikt = ikt_ref[:, :s_eff]
        sc = None
        for h in range(IDX_HEADS):
            d = _dot(iq_ref[:, h * IDX_DIM:(h + 1) * IDX_DIM], ikt)
            t = jnp.maximum(d, 0.0) * wf[:, h:h + 1]
            sc = t if sc is None else sc + t
        qpos = i * tq + lax.broadcasted_iota(jnp.int32, (tq, s_eff), 0)
        kpos = lax.broadcasted_iota(jnp.int32, (tq, s_eff), 1)
        valid = kpos <= qpos
        key_ref[:, :s_eff] = _sort_key(jnp.where(valid, sc, NEG))
        n_tail = float(s_len - s_eff)

        def count_ge(cand):
            c = jnp.where(key_ref[:, :s_eff] >= cand, 1.0, 0.0).sum(axis=-1, keepdims=True)
            return c + jnp.where(cand <= NEG_KEY, n_tail, 0.0)

        thr = _kth_largest_key(count_ge, float(n_sel), (tq, 1))
        sel = (key_ref[:, :s_eff] >= thr) & valid
        o_ref[:, :s_eff] = jnp.where(sel, 0.0, NEG).astype(o_ref.dtype)
        if s_eff < s_len:
            o_ref[:, s_eff:] = jnp.full((tq, s_len - s_eff), NEG, o_ref.dtype)

    for g in range(IDX_GROUPS):
        pl.when(i // per == g)(functools.partial(run, (g + 1) * per * tq))


def idx_mask(iq, iw, ikt, n_batch, tq):
    m = iq.shape[0]
    s_len = ikt.shape[2]
    nq = s_len // tq
    assert nq % IDX_GROUPS == 0
    n_sel = min(TOPK_MAX, s_len // 4)
    return pl.pallas_call(
        functools.partial(_idx_mask_kernel, tq=tq, n_sel=n_sel),
        out_shape=jax.ShapeDtypeStruct((n_batch, s_len, s_len), BF16),
        grid=(n_batch, nq),
        in_specs=[pl.BlockSpec((tq, IDX_HEADS * IDX_DIM), lambda b, i: (b * nq + i, 0)),
                  pl.BlockSpec((tq, IDX_HEADS), lambda b, i: (b * nq + i, 0)),
                  pl.BlockSpec((None, IDX_DIM, s_len), lambda b, i: (b, 0, 0))],
        out_specs=pl.BlockSpec((None, tq, s_len), lambda b, i: (b, i, 0)),
        scratch_shapes=[pltpu.VMEM((tq, s_len), jnp.int32)],
        compiler_params=_cp(2),
        name="idx_mask",
    )(iq, iw, ikt)


def _rel_bucket(dist):
    max_exact = REL_BUCKETS // 2
    large = max_exact + (jnp.log(jnp.maximum(dist, 1).astype(F32) / max_exact)
                         / math.log(REL_MAX_DIST / max_exact) * (REL_BUCKETS - max_exact)).astype(jnp.int32)
    large = jnp.minimum(large, REL_BUCKETS - 1)
    return jnp.where(dist < max_exact, dist, large)


def _bias_tiles_kernel(rb_ref, o_ref, *, t):
    d = pl.program_id(0)
    ii = lax.broadcasted_iota(jnp.int32, (t, t), 0)
    jj = lax.broadcasted_iota(jnp.int32, (t, t), 1)
    bucket = _rel_bucket(jnp.maximum(d * t + ii - jj, 0))
    for h in range(N_HEADS):
        acc = jnp.zeros((t, t), F32)
        for bkt in range(REL_BUCKETS):
            acc = jnp.where(bucket == bkt, rb_ref[bkt, h] * LOG2E, acc)
        o_ref[h] = acc


def bias_tiles(rel_bias, t):
    assert 2 * t - (t - 1) >= REL_MAX_DIST
    return pl.pallas_call(
        functools.partial(_bias_tiles_kernel, t=t),
        out_shape=jax.ShapeDtypeStruct((3, N_HEADS, t, t), F32),
        grid=(3,),
        in_specs=[pl.BlockSpec(memory_space=pltpu.SMEM)],
        out_specs=pl.BlockSpec((None, N_HEADS, t, t), lambda d: (d, 0, 0, 0)),
        compiler_params=_cp(1),
        name="bias_tiles",
    )(rel_bias)


def _attn_kernel(it_ref, jt_ref, q_ref, kt_ref, v_ref, mask_ref, tz_ref, o_ref, m_ref, acc_ref):
    step = pl.program_id(1)
    i = it_ref[step]
    j = jt_ref[step]
    t = q_ref.shape[0]

    @pl.when(j == 0)
    def _():
        m_ref[...] = jnp.full(m_ref.shape, -jnp.inf, F32)
        acc_ref[...] = jnp.zeros(acc_ref.shape, F32)

    maskf = mask_ref[...].astype(F32)
    ones = jnp.ones((t, HEAD_DIM), BF16)
    twice = lambda a: jnp.concatenate([a] * (t // LANES), axis=1)
    for h in range(N_HEADS):
        sl = slice(h * HEAD_DIM, (h + 1) * HEAD_DIM)
        s = _dot(q_ref[:, sl], kt_ref[sl, :]) + tz_ref[h] + maskf
        m_prev = m_ref[h]
        m_new = jnp.maximum(m_prev, s.max(axis=-1, keepdims=True))
        alpha = jnp.exp2(m_prev - m_new)
        p = jnp.exp2(s - twice(m_new))
        pv = _dot(p.astype(BF16), jnp.concatenate([v_ref[:, sl], ones], axis=1))
        acc_ref[h] = jnp.concatenate([alpha, alpha], axis=1) * acc_ref[h] + pv
        m_ref[h] = m_new

    @pl.when(j == i)
    def _():
        for h in range(N_HEADS):
            a = acc_ref[h]
            o_ref[:, h * HEAD_DIM:(h + 1) * HEAD_DIM] = (a[:, :HEAD_DIM] / a[:, HEAD_DIM:]).astype(o_ref.dtype)


def attn(q, kt, v, mask, tz, n_batch, t):
    m = q.shape[0]
    nq = m // n_batch // t
    pairs = [(i, j) for i in range(nq) for j in range(i + 1)]
    it = jnp.asarray([p[0] for p in pairs], jnp.int32)
    jt = jnp.asarray([p[1] for p in pairs], jnp.int32)
    return pl.pallas_call(
        _attn_kernel,
        out_shape=jax.ShapeDtypeStruct((m, D_MODEL), BF16),
        grid_spec=pltpu.PrefetchScalarGridSpec(
            num_scalar_prefetch=2,
            grid=(n_batch, len(pairs)),
            in_specs=[pl.BlockSpec((t, D_MODEL), lambda b, s, it, jt: (b * nq + it[s], 0)),
                      pl.BlockSpec((None, D_MODEL, t), lambda b, s, it, jt: (b, 0, jt[s])),
                      pl.BlockSpec((t, D_MODEL), lambda b, s, it, jt: (b * nq + jt[s], 0)),
                      pl.BlockSpec((None, t, t), lambda b, s, it, jt: (b, it[s], jt[s])),
                      pl.BlockSpec((None, N_HEADS, t, t),
                                   lambda b, s, it, jt: (jnp.minimum(it[s] - jt[s], 2), 0, 0, 0))],
            out_specs=pl.BlockSpec((t, D_MODEL), lambda b, s, it, jt: (b * nq + it[s], 0)),
            scratch_shapes=[pltpu.VMEM((N_HEADS, t, LANES), F32),
                            pltpu.VMEM((N_HEADS, t, 2 * HEAD_DIM), F32)]),
        compiler_params=_cp(2),
        name="attn",
    )(it, jt, q, kt, v, mask, tz)


def _ret_kernel(lg_ref, q_ref, k_ref, v_ref, g_ref, gn_ref, s0_ref, z_ref, sout_ref, state_ref,
                *, c_pad, c_true, nc):
    c = pl.program_id(1)

    @pl.when(c == 0)
    def _():
        state_ref[...] = s0_ref[...]

    ii = lax.broadcasted_iota(jnp.int32, (c_pad, c_pad), 0)
    jj = lax.broadcasted_iota(jnp.int32, (c_pad, c_pad), 1)
    rel = (ii - jj).astype(F32)
    idx = lax.broadcasted_iota(jnp.int32, (c_pad, 1), 0).astype(F32)
    for h in range(RET_HEADS):
        lg = lg_ref[h]
        qs = slice(h * RET_DK, (h + 1) * RET_DK)
        vs = slice(h * RET_DV, (h + 1) * RET_DV)
        q = q_ref[:, qs]
        k = k_ref[:, qs]
        v = v_ref[:, vs]
        decay = jnp.where(rel >= 0, jnp.exp(lg * jnp.maximum(rel, 0.0)), 0.0)
        scores = _dot_nt(q, k) * decay
        inner = _dot(scores.astype(BF16), v)
        s_prev = state_ref[h]
        cross = _dot(q, s_prev.astype(BF16)) * jnp.exp(lg * (idx + 1.0))
        o = inner + cross
        w_state = jnp.exp(lg * (c_true - 1.0 - idx))
        kw = (k.astype(F32) * w_state).astype(BF16)
        state_ref[h] = s_prev * jnp.exp(lg * c_true) + _dot_tn(kw, v)

        mu = o.mean(axis=-1, keepdims=True)
        var = jnp.square(o - mu).mean(axis=-1, keepdims=True)
        on = (o - mu) * lax.rsqrt(var + LN_EPS) * gn_ref[:, vs]
        z_ref[:, vs] = (_silu(g_ref[:, vs]) * on).astype(BF16)

    @pl.when(c == nc - 1)
    def _():
        sout_ref[...] = state_ref[...]


def retention(lg, qk, v, g, gn, s0, s_layer, n_batch, c_pad, c_true):
    m = qk.shape[0]
    nc = m // n_batch // c_pad
    row = lambda b, c: b * nc + c
    state = (None, RET_HEADS, RET_DK, RET_DV)
    return pl.pallas_call(
        functools.partial(_ret_kernel, c_pad=c_pad, c_true=float(c_true), nc=nc),
        out_shape=[jax.ShapeDtypeStruct((m, RET_VW), BF16),
                   jax.ShapeDtypeStruct((n_batch, RET_HEADS, RET_DK, RET_DV), F32)],
        grid=(n_batch, nc),
        in_specs=[pl.BlockSpec(memory_space=pltpu.SMEM),
                  pl.BlockSpec((c_pad, D_MODEL), lambda b, c: (row(b, c), 0)),
                  pl.BlockSpec((c_pad, D_MODEL), lambda b, c: (row(b, c), 1)),
                  pl.BlockSpec((c_pad, RET_VW), lambda b, c: (row(b, c), 0)),
                  pl.BlockSpec((c_pad, RET_VW), lambda b, c: (row(b, c), 0)),
                  pl.BlockSpec((1, RET_VW), lambda b, c: (0, 0)),
                  pl.BlockSpec((None,) + state, lambda b, c: (s_layer, b, 0, 0, 0))],
        out_specs=[pl.BlockSpec((c_pad, RET_VW), lambda b, c: (row(b, c), 0)),
                   pl.BlockSpec(state, lambda b, c: (b, 0, 0, 0))],
        scratch_shapes=[pltpu.VMEM((RET_HEADS, RET_DK, RET_DV), F32)],
        compiler_params=_cp(2),
        name="retention",
    )(lg, qk, qk, v, g, gn, s0)


DEC_CHUNK = 2048


def _dec_scores_kernel(pt_ref, ik_hbm, iq_ref, iw_ref, ikn_ref, o_ref, buf_ref, sem_ref,
                       *, layer, t_len, n_pages, n_batch):
    b = pl.program_id(0)
    past = n_pages * PAGE_SIZE

    def issue(bb, slot):
        def body(p, c):
            pltpu.make_async_copy(ik_hbm.at[layer, pt_ref[bb, p]],
                                  buf_ref.at[slot, pl.ds(p * PAGE_SIZE, PAGE_SIZE)],
                                  sem_ref.at[slot]).start()
            return c
        lax.fori_loop(0, n_pages, body, 0)

    @pl.when(b == 0)
    def _():
        issue(0, 0)

    @pl.when(b + 1 < n_batch)
    def _():
        issue(b + 1, (b + 1) % 2)

    slot = b % 2
    pltpu.make_async_copy(buf_ref.at[slot], buf_ref.at[slot], sem_ref.at[slot]).wait()
    iq = iq_ref[...]
    wf = iw_ref[...] * (IDX_DIM ** -0.5)

    def head_sum(keys_bf16):
        r = jnp.maximum(_dot_nt(iq, keys_bf16), 0.0) * wf
        return r.reshape(t_len, IDX_HEADS, r.shape[-1]).sum(axis=1)

    chunk = min(DEC_CHUNK, past)
    for c in range(past // chunk):
        sl = slice(c * chunk, (c + 1) * chunk)
        o_ref[:, sl] = head_sum(buf_ref[slot, sl, :].astype(BF16))
    sn = head_sum(ikn_ref[...])
    qi = lax.broadcasted_iota(jnp.int32, sn.shape, 0)
    si = lax.broadcasted_iota(jnp.int32, sn.shape, 1)
    o_ref[:, past:] = jnp.where(si < t_len, jnp.where(si <= qi, sn, NEG), -jnp.inf)


def dec_scores(page_table, cache_idx_k, layer, iq, iw, ik_new, t_len):
    n_batch, n_pages = page_table.shape
    past = n_pages * PAGE_SIZE
    assert past % min(DEC_CHUNK, past) == 0
    rows = t_len * IDX_HEADS
    return pl.pallas_call(
        functools.partial(_dec_scores_kernel, layer=layer, t_len=t_len, n_pages=n_pages,
                          n_batch=n_batch),
        out_shape=jax.ShapeDtypeStruct((n_batch, t_len, past + LANES), F32),
        grid_spec=pltpu.PrefetchScalarGridSpec(
            num_scalar_prefetch=1,
            grid=(n_batch,),
            in_specs=[pl.BlockSpec(memory_space=pl.ANY),
                      pl.BlockSpec((None, rows, IDX_DIM), lambda b, pt: (b, 0, 0)),
                      pl.BlockSpec((None, rows, 1), lambda b, pt: (b, 0, 0)),
                      pl.BlockSpec((None, LANES, IDX_DIM), lambda b, pt: (b, 0, 0))],
            out_specs=pl.BlockSpec((None, t_len, past + LANES), lambda b, pt: (b, 0, 0)),
            scratch_shapes=[pltpu.VMEM((2, past, IDX_DIM), F32),
                            pltpu.SemaphoreType.DMA((2,))]),
        compiler_params=_cp(1),
        name="dec_scores",
    )(page_table, cache_idx_k, iq, iw, ik_new)


def _dec_topk_kernel(s_ref, o_ref, x_ref, *, n_sel):
    x_ref[...] = s_ref[...]
    blk = lax.broadcasted_iota(jnp.int32, x_ref.shape, 0)
    lane = lax.broadcasted_iota(jnp.int32, x_ref.shape, 2)
    key_id = blk * LANES + lane
    big = jnp.int32(2 ** 30)

    def body(j, c):
        x = x_ref[...]
        m = x.max(axis=0).max(axis=-1, keepdims=True)
        pick = jnp.where(x == m[None], key_id, big).min(axis=0).min(axis=-1, keepdims=True)
        x_ref[...] = jnp.where(key_id == pick[None], -jnp.inf, x)
        o_ref[j] = pick
        return c

    lax.fori_loop(0, n_sel, body, 0)


def dec_topk(s_blocks, n_sel):
    nb, rows, _ = s_blocks.shape
    return pl.pallas_call(
        functools.partial(_dec_topk_kernel, n_sel=n_sel),
        out_shape=jax.ShapeDtypeStruct((n_sel, rows, 1), jnp.int32),
        grid=(1,),
        in_specs=[pl.BlockSpec((nb, rows, LANES), lambda i: (0, 0, 0))],
        out_specs=pl.BlockSpec((n_sel, rows, 1), lambda i: (0, 0, 0)),
        scratch_shapes=[pltpu.VMEM((nb, rows, LANES), F32)],
        compiler_params=_cp(1),
        name="dec_topk",
    )(s_blocks)


def _dec_attn_kernel(idx_ref, pt_ref, ck_hbm, cv_hbm, kn_hbm, vn_hbm, q_ref, idxv_ref, rbt_ref, o_ref,
                     kbuf_ref, vbuf_ref, sem_ref, *, layer, t_len, n_pages, n_q, n_sel):
    r = pl.program_id(0)
    past = n_pages * PAGE_SIZE

    def issue(rr, slot):
        bb = rr // t_len

        def body(j, c):
            key = idx_ref[rr * n_sel + j]

            @pl.when(key < past)
            def _():
                phys = pt_ref[bb, key // PAGE_SIZE]
                row = key % PAGE_SIZE
                pltpu.make_async_copy(ck_hbm.at[layer, phys, row], kbuf_ref.at[slot, j],
                                      sem_ref.at[0, slot]).start()
                pltpu.make_async_copy(cv_hbm.at[layer, phys, row], vbuf_ref.at[slot, j],
                                      sem_ref.at[1, slot]).start()

            @pl.when(key >= past)
            def _():
                row = bb * t_len + jnp.minimum(key - past, t_len - 1)
                pltpu.make_async_copy(kn_hbm.at[row], kbuf_ref.at[slot, j], sem_ref.at[0, slot]).start()
                pltpu.make_async_copy(vn_hbm.at[row], vbuf_ref.at[slot, j], sem_ref.at[1, slot]).start()

            return c
        lax.fori_loop(0, n_sel, body, 0, unroll=4)

    @pl.when(r == 0)
    def _():
        issue(0, 0)

    @pl.when(r + 1 < n_q)
    def _():
        issue(r + 1, (r + 1) % 2)

    slot = r % 2
    pltpu.make_async_copy(kbuf_ref.at[slot], kbuf_ref.at[slot], sem_ref.at[0, slot]).wait()
    pltpu.make_async_copy(vbuf_ref.at[slot], vbuf_ref.at[slot], sem_ref.at[1, slot]).wait()

    qb = q_ref[...].astype(BF16)
    dist = (past + r % t_len) - idxv_ref[...]
    bucket = _rel_bucket(jnp.maximum(dist, 0))
    s_rows = []
    for h in range(N_HEADS):
        kh = kbuf_ref[slot, :, h, :].astype(BF16)
        s_rows.append(_dot_nt(qb, kh)[h:h + 1, :])
    s = jnp.concatenate(s_rows, axis=0)
    bias = jnp.zeros(s.shape, F32)
    for bkt in range(REL_BUCKETS):
        bias = jnp.where(bucket == bkt, rbt_ref[:, bkt:bkt + 1], bias)
    s = jnp.where(dist >= 0, s + bias, NEG)
    p = jnp.exp(s - s.max(axis=-1, keepdims=True))
    p = (p / p.sum(axis=-1, keepdims=True)).astype(BF16)
    for h in range(N_HEADS):
        vh = vbuf_ref[slot, :, h, :].astype(BF16)
        o_ref[h:h + 1, :] = _dot(p, vh)[h:h + 1, :]


def dec_attn(idx_flat, page_table, idx_rows, q, cache_k, cache_v, layer, k_new, v_new, rbt, t_len, n_sel):
    n_batch, n_pages = page_table.shape
    n_q = n_batch * t_len
    slab = (N_HEADS, HEAD_DIM)
    return pl.pallas_call(
        functools.partial(_dec_attn_kernel, layer=layer, t_len=t_len, n_pages=n_pages, n_q=n_q,
                          n_sel=n_sel),
        out_shape=jax.ShapeDtypeStruct((n_q,) + slab, F32),
        grid_spec=pltpu.PrefetchScalarGridSpec(
            num_scalar_prefetch=2,
            grid=(n_q,),
            in_specs=[pl.BlockSpec(memory_space=pl.ANY),
                      pl.BlockSpec(memory_space=pl.ANY),
                      pl.BlockSpec(memory_space=pl.ANY),
                      pl.BlockSpec(memory_space=pl.ANY),
                      pl.BlockSpec((None,) + slab, lambda r, ix, pt: (r, 0, 0)),
                      pl.BlockSpec((None, 1, n_sel), lambda r, ix, pt: (r, 0, 0)),
                      pl.BlockSpec((N_HEADS, REL_BUCKETS), lambda r, ix, pt: (0, 0))],
            out_specs=pl.BlockSpec((None,) + slab, lambda r, ix, pt: (r, 0, 0)),
            scratch_shapes=[pltpu.VMEM((2, n_sel) + slab, F32),
                            pltpu.VMEM((2, n_sel) + slab, F32),
                            pltpu.SemaphoreType.DMA((2, 2))]),
        compiler_params=_cp(1),
        name="dec_attn",
    )(idx_flat, page_table, cache_k, cache_v, k_new, v_new, q, idx_rows, rbt)


ATT_T = 256
IDX_TQ = 256
DEC_PAD = 16


def _attn_in_proj(xb, w_in, layer, tm, prompt_batches=0):
    if prompt_batches:
        q, = proj(xb, w_in, layer, 0, D_MODEL, 1024, tm, [BF16], scale=HEAD_DIM ** -0.5 * LOG2E,
                  name="proj_q")
        k, kb = proj(xb, w_in, layer, D_MODEL, D_MODEL, 1024, tm, [F32], name="proj_k",
                     transposed_batches=prompt_batches)
        v, vb = proj(xb, w_in, layer, 2 * D_MODEL, D_MODEL, 1024, tm, [F32, BF16], name="proj_v")
    else:
        q, = proj(xb, w_in, layer, 0, D_MODEL, 1024, tm, [F32], scale=HEAD_DIM ** -0.5, name="proj_q")
        k, = proj(xb, w_in, layer, D_MODEL, D_MODEL, 1024, tm, [F32], name="proj_k")
        v, = proj(xb, w_in, layer, 2 * D_MODEL, D_MODEL, 1024, tm, [F32], name="proj_v")
        kb = vb = None
    iq, = proj(xb, w_in, layer, 3 * D_MODEL, IDX_HEADS * IDX_DIM, 1024, tm, [BF16], name="proj_iq")
    o4 = 3 * D_MODEL + IDX_HEADS * IDX_DIM
    w_tail = jnp.pad(w_in[layer, :, o4:], ((0, 0), (0, 2 * LANES - (IDX_DIM + IDX_HEADS))))
    tail, = proj(xb, w_tail, 0, 0, 2 * LANES, 2 * LANES, tm, [F32], name="proj_tail")
    ik = tail[:, :IDX_DIM]
    iw = tail[:, IDX_DIM:IDX_DIM + IDX_HEADS] * IDX_HEADS ** -0.5
    return q, k, v, kb, vb, iq, ik, iw


def _attn_prompt(xb, w_in, w_out_args, layer, rel_tiles, n_batch):
    m = xb.shape[0]
    t_len = m // n_batch
    q, k, v, kt, vb, iq, ik, iw = _attn_in_proj(xb, w_in, layer, 1024, prompt_batches=n_batch)
    ikt = jnp.swapaxes(ik.astype(BF16).reshape(n_batch, t_len, IDX_DIM), 1, 2)
    mask = idx_mask(iq, iw, ikt, n_batch, IDX_TQ)
    o = attn(q, kt, vb, mask, rel_tiles, n_batch, ATT_T)
    k = k.reshape(n_batch, t_len, N_HEADS, HEAD_DIM)
    v = v.reshape(n_batch, t_len, N_HEADS, HEAD_DIM)
    return o, k, v, ik.reshape(n_batch, t_len, IDX_DIM)


def _attn_sample(xb, w_in, layer, cache_k, cache_v, cache_idx_k, page_table, rbt, n_batch):
    m = xb.shape[0]
    t_len = m // n_batch
    q, k_new, v_new, _, _, iq, ik, iw = _attn_in_proj(xb, w_in, layer, m)
    iq_r = iq.reshape(n_batch, t_len * IDX_HEADS, IDX_DIM)
    iw_r = iw.reshape(n_batch, t_len * IDX_HEADS, 1)
    ik_pad = jnp.pad(ik.reshape(n_batch, t_len, IDX_DIM), ((0, 0), (0, LANES - t_len), (0, 0)))
    sc = dec_scores(page_table, cache_idx_k, layer, iq_r, iw_r, ik_pad.astype(BF16), t_len)
    n_keys = sc.shape[-1]
    n_sel = min(TOPK_MAX, (n_keys - LANES + t_len) // 4)
    s_blocks = jnp.transpose(sc.reshape(m, n_keys // LANES, LANES), (1, 0, 2))
    sel = dec_topk(s_blocks, n_sel)
    sel = jnp.swapaxes(sel[:, :, 0], 0, 1)
    slab = (m, N_HEADS, HEAD_DIM)
    o = dec_attn(sel.reshape(-1), page_table, sel[:, None, :], q.reshape(slab), cache_k, cache_v, layer,
                 k_new.reshape(slab), v_new.reshape(slab), rbt, t_len, n_sel)
    return (o.reshape(m, D_MODEL).astype(BF16), k_new.reshape(n_batch, t_len, N_HEADS, HEAD_DIM),
            v_new.reshape(n_batch, t_len, N_HEADS, HEAD_DIM), ik.reshape(n_batch, t_len, IDX_DIM))


def _rot_tables(pos):
    half = RET_DK // 2
    inv = jnp.exp(-math.log(ROPE_BASE) * jnp.arange(half, dtype=F32) / half)
    ang = pos.astype(F32)[:, None] * inv[None, :]
    return jnp.cos(ang), jnp.sin(ang)


def _retention_mixer(xb, w_in, layer, gn, s0, s_layer, lg, cos, sin, n_batch, chunk, pad_to):
    m = xb.shape[0]
    t_len = m // n_batch
    tm = min(1024, m)
    qk = proj_rot(xb, w_in, layer, cos, sin, 1024, tm)
    v, = proj(xb, w_in, layer, 2 * D_MODEL, RET_VW, 1024, tm, [BF16], name="proj_rv")
    g, = proj(xb, w_in, layer, 2 * D_MODEL + RET_VW, RET_VW, 1024, tm, [F32], name="proj_rg")
    if pad_to != t_len:
        padr = lambda a: jnp.pad(a.reshape(n_batch, t_len, -1),
                                 ((0, 0), (0, pad_to - t_len), (0, 0))).reshape(n_batch * pad_to, -1)
        z, s_fin = retention(lg, padr(qk), padr(v), padr(g), gn, s0, s_layer, n_batch, pad_to, chunk)
        z = z.reshape(n_batch, pad_to, RET_VW)[:, :t_len].reshape(m, RET_VW)
    else:
        z, s_fin = retention(lg, qk, v, g, gn, s0, s_layer, n_batch, chunk, chunk)
    return z, s_fin


MOE_TILE = 512
MOE_GU_TN = D_FF_EXPERT // 2


def _moe_sparse(xf, xf_x, w_router_pad, w_gu, w_down, layer, g, b):
    m, mx = xf.shape[0], xf_x.shape[0]
    meta, cnt = router(xf, w_router_pad, 512)
    meta_x, cnt_x = router(xf_x, w_router_pad, 512)
    col = lambda mt, c: mt[:, c].astype(jnp.int32)
    counts_main = cnt[0, :N_EXPERTS].astype(jnp.int32)
    counts = counts_main + cnt_x[0, :N_EXPERTS].astype(jnp.int32)
    n_tile = (counts + MOE_TILE - 1) // MOE_TILE
    end_tile = jnp.cumsum(n_tile)
    start_row = (end_tile - n_tile) * MOE_TILE
    start_x = start_row + counts_main
    pos1 = start_row[col(meta, META_E1)] + col(meta, META_R1)
    pos2 = start_row[col(meta, META_E2)] + col(meta, META_R2)
    pos1_x = start_x[col(meta_x, META_E1)] + col(meta_x, META_R1)
    pos2_x = start_x[col(meta_x, META_E2)] + col(meta_x, META_R2)
    max_tiles = (2 * (m + mx) + N_EXPERTS * (MOE_TILE - 1)) // MOE_TILE
    tiles = jnp.arange(max_tiles, dtype=jnp.int32)
    tile_expert = jnp.minimum(jnp.sum(tiles[:, None] >= end_tile[None, :], axis=1), N_EXPERTS - 1)
    n_valid = end_tile[-1:].astype(jnp.int32)
    xg = moe_dispatch(pos1, pos2, pos1_x, pos2_x, start_row + counts, end_tile * MOE_TILE, n_valid,
                      xf, xf_x, MOE_TILE, max_tiles)
    te = tile_expert.astype(jnp.int32)
    hg = moe_gu_grouped(te, n_valid, xg, w_gu, layer, MOE_TILE, MOE_GU_TN)
    yg = moe_down_grouped(te, n_valid, hg, w_down, layer, MOE_TILE, 1024)
    return (moe_combine_ln(pos1, pos2, yg, meta, xf, g, b, 256),
            moe_combine_ln(pos1_x, pos2_x, yg, meta_x, xf_x, g, b, 256))


def kernel(x_prompt, x_sample, cache_k, cache_v, cache_idx_k, state_ret, page_table, rel_bias,
           w_in_attn, w_out_attn, w_in_ret, ret_gn_g, w_out_ret, w_ffn_gu, w_ffn_down,
           w_router, w_exp_gu, w_exp_down, ln_g, ln_b):
    bp, tp, _ = x_prompt.shape
    bs, ts, _ = x_sample.shape
    past = page_table.shape[1] * PAGE_SIZE
    mp, ms = bp * tp, bs * ts
    xp = x_prompt.reshape(mp, D_MODEL)
    xs = x_sample.reshape(ms, D_MODEL)
    xpb, xsb = xp.astype(BF16), xs.astype(BF16)

    rel_tiles = bias_tiles(rel_bias, ATT_T)
    rbt = rel_bias.T
    w_out_attn, w_out_ret, w_ffn_down = (w.astype(BF16) for w in (w_out_attn, w_out_ret, w_ffn_down))
    lg = jnp.log1p(-jnp.exp2(-5.0 - jnp.arange(RET_HEADS, dtype=F32)))
    cos_p, sin_p = _rot_tables(jnp.tile(jnp.arange(tp, dtype=jnp.int32), bp))
    cos_s, sin_s = _rot_tables(jnp.tile(past + jnp.arange(ts, dtype=jnp.int32), bs))
    zero_state = jnp.zeros((1, bp, RET_HEADS, RET_DK, RET_DV), F32)

    kp_l, vp_l, ikp_l, sp_l = [], [], [], []
    ks_l, vs_l, iks_l, ss_l = [], [], [], []
    for i in range(DEPTH):
        j = i // 2
        g0, b0 = ln_g[i, 0][None, :], ln_b[i, 0][None, :]
        g1, b1 = ln_g[i, 1][None, :], ln_b[i, 1][None, :]
        if i % 2 == 0:
            op, kp, vp, ikp = _attn_prompt(xpb, w_in_attn, None, j, rel_tiles, bp)
            os_, k_s, v_s, iks = _attn_sample(xsb, w_in_attn, j, cache_k, cache_v, cache_idx_k,
                                              page_table, rbt, bs)
            kp_l.append(kp); vp_l.append(vp); ikp_l.append(ikp)
            ks_l.append(k_s); vs_l.append(v_s); iks_l.append(iks)
            xp, xpb = down_ln(op, w_out_attn, j, xp, g0, b0, 512, 512, D_MODEL)
            xs, xsb = down_ln(os_, w_out_attn, j, xs, g0, b0, 512, 512, D_MODEL)
        else:
            gn = ret_gn_g[j][None, :]
            zp, sp = _retention_mixer(xpb, w_in_ret, j, gn, zero_state, 0, lg, cos_p, sin_p,
                                      bp, min(RET_CHUNK, tp), tp)
            zs, ss = _retention_mixer(xsb, w_in_ret, j, gn, state_ret, j, lg, cos_s, sin_s,
                                      bs, ts, DEC_PAD)
            sp_l.append(sp); ss_l.append(ss)
            xp, xpb = down_ln(zp, w_out_ret, j, xp, g0, b0, 512, 512, RET_VW // 2)
            xs, xsb = down_ln(zs, w_out_ret, j, xs, g0, b0, 512, 512, RET_VW // 2)
        if i % 2 == 0:
            hp = ffn_gu(xpb, w_ffn_gu, j, 512, 1024)
            hs = ffn_gu(xsb, w_ffn_gu, j, 512, 1024)
            xp, xpb = down_ln(hp, w_ffn_down, j, xp, g1, b1, 512, 512, D_FF // 2)
            xs, xsb = down_ln(hs, w_ffn_down, j, xs, g1, b1, 512, 512, D_FF // 2)
        else:
            w_r = jnp.pad(w_router[j], ((0, 0), (0, LANES - N_EXPERTS)))
            (xp, xpb), (xs, xsb) = _moe_sparse(xp, xs, w_r, w_exp_gu, w_exp_down, j, g1, b1)

    return (xp.reshape(bp, tp, D_MODEL), xs.reshape(bs, ts, D_MODEL),
            jnp.stack(kp_l), jnp.stack(vp_l), jnp.stack(ikp_l), jnp.stack(sp_l),
            jnp.stack(ks_l), jnp.stack(vs_l), jnp.stack(iks_l), jnp.stack(ss_l))
```
